```python
import math
import jax, jax.numpy as jnp
from jax import lax
import numpy as np

D_MODEL = 2048
BATCH = 1
SEQ = 16384
DEPTH = 4

GRID_W = 64
CTX_LEN = 256
HEAD_DIM = 128
NA_HEADS = 8
NA_WIDTH = NA_HEADS * HEAD_DIM
NA_KH = 8
NA_KW = 16
NA_QC = 16
SG_GROUPS = 4
SG_HEAD = 128
SG_WIDTH = SG_GROUPS * SG_HEAD
SG_CHUNK = 128
FT_GROUPS = 4
FT_HEAD = 128
FT_WIDTH = FT_GROUPS * FT_HEAD
MIX_WIDTH = NA_WIDTH + SG_WIDTH + FT_WIDTH
SG_OFF = 3 * NA_WIDTH
FT_OFF = SG_OFF + 2 * SG_WIDTH
IN_WIDTH = FT_OFF + FT_WIDTH
ROPE_BASE = 10000.0
N_EXPERTS = 32
TOP_K = 4
D_FF = D_MODEL // 4
SWIGLU_ALPHA = 1.702
SWIGLU_LIMIT = 7.0
MOE_BLOCK = 128
LN_EPS = 1e-5
NEG_INF = -1e30
DEEPNORM_ALPHA = (2 * DEPTH) ** 0.25
DEEPNORM_BETA = (8 * DEPTH) ** -0.25

kernel_name = 'hybrid_na_gmlp_fourier_moe_diffusion_trunk'


def layer_norm(x, gain=None, bias=None):
    xf = x.astype(jnp.float32)
    xc = xf - xf.mean(-1, keepdims=True)
    y = xc * lax.rsqrt((xc * xc).mean(-1, keepdims=True) + LN_EPS)
    if gain is not None:
        y = y * gain.astype(jnp.float32) + bias.astype(jnp.float32)
    return y.astype(x.dtype)


def rope_tables(seq):
    t = jnp.arange(seq, dtype=jnp.int32)
    pos_r = (t // GRID_W).astype(jnp.float32)
    pos_c = (t % GRID_W).astype(jnp.float32)
    quarter = HEAD_DIM // 4
    inv_freq = ROPE_BASE ** (-jnp.arange(quarter, dtype=jnp.float32) / quarter)
    ang_r = pos_r[:, None] * inv_freq
    ang_c = pos_c[:, None] * inv_freq
    return (jnp.cos(ang_r), jnp.sin(ang_r), jnp.cos(ang_c), jnp.sin(ang_c))


def axial_rope(x, rope):
    cos_r, sin_r, cos_c, sin_c = rope
    half = HEAD_DIM // 2
    quarter = HEAD_DIM // 4

    def rotate(seg, cos, sin):
        cos = cos[None, :, None, :].astype(x.dtype)
        sin = sin[None, :, None, :].astype(x.dtype)
        a, b = seg[..., :quarter], seg[..., quarter:]
        return jnp.concatenate([a * cos - b * sin, b * cos + a * sin], axis=-1)

    return jnp.concatenate([rotate(x[..., :half], cos_r, sin_r),
                            rotate(x[..., half:], cos_c, sin_c)], axis=-1)


def neighbourhood_attention(q_rot, q_plain, k_rot, v, k_ctx, v_ctx, rpb):
    B, S, H, Dh = q_rot.shape
    rows = S // GRID_W
    kh = min(NA_KH, rows)
    kw = NA_KW
    qr = math.gcd(rows, kh)
    qc = NA_QC
    n_rb = rows // qr
    n_cb = GRID_W // qc
    br = min(kh + qr - 1, rows)
    bc = min(kw + qc - 1, GRID_W)
    scale = HEAD_DIM ** -0.5

    r = jnp.arange(rows, dtype=jnp.int32)
    col = jnp.arange(GRID_W, dtype=jnp.int32)
    win_r = jnp.clip(r - kh // 2, 0, rows - kh)
    win_c = jnp.clip(col - kw // 2, 0, GRID_W - kw)
    q_rows = r.reshape(n_rb, qr)
    q_cols = col.reshape(n_cb, qc)
    key_rows = jnp.clip(win_r[q_rows[:, 0]], 0, rows - br)[:, None] + jnp.arange(br, dtype=jnp.int32)
    key_cols = jnp.clip(win_c[q_cols[:, 0]], 0, GRID_W - bc)[:, None] + jnp.arange(bc, dtype=jnp.int32)

    def band(t):
        g = t.reshape(B, rows, GRID_W, H, Dh)
        g = g[:, key_rows[:, None, :, None], key_cols[None, :, None, :]]
        return g.reshape(B, n_rb, n_cb, br * bc, H, Dh)

    def blocks(t):
        t = t.reshape(B, n_rb, qr, n_cb, qc, H, Dh).transpose(0, 1, 3, 2, 4, 5, 6)
        return t.reshape(B, n_rb, n_cb, qr * qc, H, Dh)

    k_band = band(k_rot)
    v_band = band(v)
    s_lat = jnp.einsum('bijqhd,bijkhd->bijhqk', blocks(q_rot), k_band,
                       preferred_element_type=jnp.float32) * scale
    s_ctx = jnp.einsum('bijqhd,bkhd->bijhqk', blocks(q_plain), k_ctx,
                       preferred_element_type=jnp.float32) * scale

    rel_r = key_rows[:, None, :] - q_rows[:, :, None]
    rel_c = key_cols[:, None, :] - q_cols[:, :, None]
    lo_r = win_r[q_rows][:, :, None]
    lo_c = win_c[q_cols][:, :, None]
    in_r = (key_rows[:, None, :] >= lo_r) & (key_rows[:, None, :] < lo_r + kh)
    in_c = (key_cols[:, None, :] >= lo_c) & (key_cols[:, None, :] < lo_c + kw)
    ri = jnp.clip(rel_r + NA_KH - 1, 0, 2 * NA_KH - 2)
    ci = jnp.clip(rel_c + NA_KW - 1, 0, 2 * NA_KW - 2)
    bias = rpb[:, ri[:, None, :, None, :, None], ci[None, :, None, :, None, :]]
    bias = bias.reshape(H, n_rb, n_cb, qr * qc, br * bc).transpose(1, 2, 0, 3, 4).astype(jnp.float32)
    inside = (in_r[:, None, :, None, :, None] & in_c[None, :, None, :, None, :]).reshape(
        n_rb, n_cb, 1, qr * qc, br * bc)
    s_lat = jnp.where(inside, s_lat + bias, NEG_INF)

    p = jax.nn.softmax(jnp.concatenate([s_lat, s_ctx], axis=-1), axis=-1).astype(v.dtype)
    nk = br * bc
    o = (jnp.einsum('bijhqk,bijkhd->bijqhd', p[..., :nk], v_band)
         + jnp.einsum('bijhqk,bkhd->bijqhd', p[..., nk:], v_ctx))
    o = o.reshape(B, n_rb, n_cb, qr, qc, H, Dh).transpose(0, 1, 3, 2, 4, 5, 6)
    return o.reshape(B, S, H * Dh)


def context_attention(q, k, v):
    s = jnp.einsum('bqhd,bkhd->bhqk', q, k, preferred_element_type=jnp.float32) * HEAD_DIM ** -0.5
    p = jax.nn.softmax(s, axis=-1).astype(v.dtype)
    return jnp.einsum('bhqk,bkhd->bqhd', p, v)


def spatial_gating(z, w_s, b_s):
    B, L, _ = z.shape
    z = jax.nn.gelu(z, approximate=False)
    u, v = z[..., :SG_WIDTH], z[..., SG_WIDTH:]
    v = layer_norm(v.reshape(B, L // SG_CHUNK, SG_CHUNK, SG_GROUPS, SG_HEAD))
    v = jnp.einsum('gpq,bnqgc->bnpgc', w_s, v) + b_s.T[:, :, None]
    return u * v.reshape(B, L, SG_WIDTH)


def fourier_mix(z):
    B, L, _ = z.shape
    zg = z.reshape(B, L, FT_GROUPS, FT_HEAD).astype(jnp.float32)
    f = jnp.fft.fft2(zg, axes=(1, 3), norm='ortho').real
    return f.reshape(B, L, FT_WIDTH).astype(z.dtype)


def hybrid_mixer(h, hc, w_in, w_out, sg_w, sg_b, rpb, rope, need_ctx_out):
    B, S, _ = h.shape
    cl = hc.shape[1]
    z = h @ w_in
    q = z[..., :NA_WIDTH].reshape(B, S, NA_HEADS, HEAD_DIM)
    k = z[..., NA_WIDTH:2 * NA_WIDTH].reshape(B, S, NA_HEADS, HEAD_DIM)
    v = z[..., 2 * NA_WIDTH:3 * NA_WIDTH].reshape(B, S, NA_HEADS, HEAD_DIM)
    if need_ctx_out:
        zc = hc @ w_in
        kv_c = zc[..., NA_WIDTH:3 * NA_WIDTH]
    else:
        kv_c = hc @ w_in[:, NA_WIDTH:3 * NA_WIDTH]
    k_c = kv_c[..., :NA_WIDTH].reshape(B, cl, NA_HEADS, HEAD_DIM)
    v_c = kv_c[..., NA_WIDTH:].reshape(B, cl, NA_HEADS, HEAD_DIM)

    o_na = neighbourhood_attention(axial_rope(q, rope), q, axial_rope(k, rope), v, k_c, v_c, rpb)
    o_sg = spatial_gating(z[..., SG_OFF:FT_OFF], sg_w, sg_b)
    o_ft = fourier_mix(z[..., FT_OFF:])
    y = jnp.concatenate([o_na, o_sg, o_ft], axis=-1) @ w_out
    if not need_ctx_out:
        return y, None

    q_c = zc[..., :NA_WIDTH].reshape(B, cl, NA_HEADS, HEAD_DIM)
    o_na_c = context_attention(q_c, k_c, v_c).reshape(B, cl, NA_WIDTH)
    o_sg_c = spatial_gating(zc[..., SG_OFF:FT_OFF], sg_w, sg_b)
    o_ft_c = fourier_mix(zc[..., FT_OFF:])
    yc = jnp.concatenate([o_na_c, o_sg_c, o_ft_c], axis=-1) @ w_out
    return y, yc


def clamped_swiglu(gu):
    glu = jnp.minimum(gu[..., ::2], SWIGLU_LIMIT)
    lin = jnp.clip(gu[..., 1::2], -SWIGLU_LIMIT, SWIGLU_LIMIT)
    return glu * jax.nn.sigmoid(SWIGLU_ALPHA * glu) * (lin + 1)


def moe_ffn(h, w_router, b_router, w_gate_up, b_gate_up, w_down, b_down):
    T, Dm = h.shape
    logits = (h @ w_router + b_router).astype(jnp.float32)
    top_v, top_e = lax.top_k(logits, TOP_K)
    gate = jax.nn.softmax(top_v, axis=-1)
    A = T * TOP_K
    flat_e = top_e.reshape(A)
    flat_t = jnp.arange(A, dtype=jnp.int32) // TOP_K
    flat_g = gate.reshape(A)
    order = jnp.argsort(flat_e)
    se = flat_e[order]
    counts = jnp.bincount(flat_e, length=N_EXPERTS)
    group_start = jnp.cumsum(counts) - counts
    blocks_per = (counts + MOE_BLOCK - 1) // MOE_BLOCK
    block_end = jnp.cumsum(blocks_per)
    block_start = block_end - blocks_per
    slot = block_start[se] * MOE_BLOCK + jnp.arange(A, dtype=jnp.int32) - group_start[se]
    n_blocks = -(-A // MOE_BLOCK) + N_EXPERTS
    n_slots = n_blocks * MOE_BLOCK
    slot_tok = jnp.zeros((n_slots,), jnp.int32).at[slot].set(flat_t[order])
    slot_gate = jnp.zeros((n_slots,), jnp.float32).at[slot].set(flat_g[order])
    block_expert = jnp.clip(jnp.searchsorted(block_end, jnp.arange(n_blocks), side='right'),
                            0, N_EXPERTS - 1)
    xb = h[slot_tok].reshape(n_blocks, MOE_BLOCK, Dm)

    def expert_block(args):
        xblk, e = args
        gu = xblk @ w_gate_up[e] + b_gate_up[e]
        return clamped_swiglu(gu) @ w_down[e] + b_down[e]

    yb = lax.map(expert_block, (xb, block_expert)).reshape(n_slots, Dm)
    yb = yb * slot_gate[:, None].astype(yb.dtype)
    return jnp.zeros_like(h).at[slot_tok].add(yb)


def setup_inputs(seed: int = 0) -> dict:
    key = jax.random.key(seed)
    ks = jax.random.split(key, 21)
    nrm = jax.random.normal
    f32 = jnp.float32
    x = nrm(ks[0], (BATCH, SEQ, D_MODEL), f32)
    c = nrm(ks[1], (BATCH, D_MODEL), f32)
    ctx = nrm(ks[2], (BATCH, CTX_LEN, D_MODEL), f32)
    c_ctx = nrm(ks[3], (D_MODEL,), f32)
    w_ada = nrm(ks[4], (DEPTH, D_MODEL, 6 * D_MODEL), f32) * (0.5 * D_MODEL ** -0.5)
    b_ada = 0.02 * nrm(ks[5], (DEPTH, 6 * D_MODEL), f32)
    w_in = nrm(ks[6], (DEPTH, D_MODEL, IN_WIDTH), f32) * D_MODEL ** -0.5
    w_out = nrm(ks[7], (DEPTH, MIX_WIDTH, D_MODEL), f32) * (MIX_WIDTH ** -0.5 * DEEPNORM_BETA)
    sg_w = nrm(ks[8], (DEPTH, SG_GROUPS, SG_CHUNK, SG_CHUNK), f32) * SG_CHUNK ** -0.5
    sg_b = 1.0 + 0.02 * nrm(ks[9], (DEPTH, SG_GROUPS, SG_CHUNK), f32)
    na_rpb = 0.05 * nrm(ks[10], (DEPTH, NA_HEADS, 2 * NA_KH - 1, 2 * NA_KW - 1), f32)
    ln1_g = 1.0 + 0.02 * nrm(ks[11], (DEPTH, D_MODEL), f32)
    ln1_b = 0.02 * nrm(ks[12], (DEPTH, D_MODEL), f32)
    ln2_g = 1.0 + 0.02 * nrm(ks[13], (DEPTH, D_MODEL), f32)
    ln2_b = 0.02 * nrm(ks[14], (DEPTH, D_MODEL), f32)
    w_router = nrm(ks[15], (DEPTH, D_MODEL, N_EXPERTS), f32) * D_MODEL ** -0.5
    b_router = 0.01 * nrm(ks[16], (DEPTH, N_EXPERTS), f32)
    w_gate_up = nrm(ks[17], (DEPTH, N_EXPERTS, D_MODEL, 2 * D_FF), f32) * D_MODEL ** -0.5
    b_gate_up = 0.02 * nrm(ks[18], (DEPTH, N_EXPERTS, 2 * D_FF), f32)
    w_down = nrm(ks[19], (DEPTH, N_EXPERTS, D_FF, D_MODEL), f32) * (D_FF ** -0.5 * DEEPNORM_BETA)
    b_down = 0.02 * nrm(ks[20], (DEPTH, N_EXPERTS, D_MODEL), f32)
    return {'x': x, 'c': c, 'ctx': ctx, 'c_ctx': c_ctx, 'w_ada': w_ada, 'b_ada': b_ada,
            'w_in': w_in, 'w_out': w_out, 'sg_w': sg_w, 'sg_b': sg_b, 'na_rpb': na_rpb,
            'ln1_g': ln1_g, 'ln1_b': ln1_b, 'ln2_g': ln2_g, 'ln2_b': ln2_b,
            'w_router': w_router, 'b_router': b_router, 'w_gate_up': w_gate_up,
            'b_gate_up': b_gate_up, 'w_down': w_down, 'b_down': b_down}


def reference(x, c, ctx, c_ctx, w_ada, b_ada, w_in, w_out, sg_w, sg_b, na_rpb,
              ln1_g, ln1_b, ln2_g, ln2_b, w_router, b_router, w_gate_up, b_gate_up,
              w_down, b_down):
    B, S, Dm = x.shape
    cl = ctx.shape[1]
    rope = rope_tables(S)
    cond_x = jax.nn.silu(c)
    cond_c = jax.nn.silu(c_ctx)
    xc = ctx
    for l in range(DEPTH):
        last = l == DEPTH - 1
        mod_x = jnp.split((cond_x @ w_ada[l] + b_ada[l])[:, None, :], 6, axis=-1)
        n_c = 2 if last else 6
        mod_c = jnp.split(cond_c @ w_ada[l][:, :n_c * Dm] + b_ada[l][:n_c * Dm], n_c, axis=-1)

        h = layer_norm(x) * (1 + mod_x[1]) + mod_x[0]
        hc = layer_norm(xc) * (1 + mod_c[1]) + mod_c[0]
        y, yc = hybrid_mixer(h, hc, w_in[l], w_out[l], sg_w[l], sg_b[l], na_rpb[l], rope,
                             need_ctx_out=not last)
        x = layer_norm(DEEPNORM_ALPHA * x + mod_x[2] * y, ln1_g[l], ln1_b[l])

        h = layer_norm(x) * (1 + mod_x[4]) + mod_x[3]
        moe_w = (w_router[l], b_router[l], w_gate_up[l], b_gate_up[l], w_down[l], b_down[l])
        if last:
            f = moe_ffn(h.reshape(B * S, Dm), *moe_w).reshape(B, S, Dm)
        else:
            xc = layer_norm(DEEPNORM_ALPHA * xc + mod_c[2] * yc, ln1_g[l], ln1_b[l])
            hc = layer_norm(xc) * (1 + mod_c[4]) + mod_c[3]
            tokens = jnp.concatenate([h.reshape(B * S, Dm), hc.reshape(B * cl, Dm)], axis=0)
            ff = moe_ffn(tokens, *moe_w)
            f = ff[:B * S].reshape(B, S, Dm)
            xc = layer_norm(DEEPNORM_ALPHA * xc + mod_c[5] * ff[B * S:].reshape(B, cl, Dm),
                            ln2_g[l], ln2_b[l])
        x = layer_norm(DEEPNORM_ALPHA * x + mod_x[5] * f, ln2_g[l], ln2_b[l])
    return x
```

```python
import functools
import math

import numpy as np
import jax
import jax.numpy as jnp
from jax import lax
from jax.experimental import pallas as pl
from jax.experimental.pallas import tpu as pltpu

D_MODEL = 2048
DEPTH_NORM = 4
GRID_W = 64
HEAD_DIM = 128
NA_HEADS = 8
NA_WIDTH = NA_HEADS * HEAD_DIM
NA_KH = 8
NA_KW = 16
SG_GROUPS = 4
SG_WIDTH = 512
SG_CHUNK = 128
FT_GROUPS = 4
FT_WIDTH = 512
SG_OFF = 3 * NA_WIDTH
FT_OFF = SG_OFF + 2 * SG_WIDTH
IN_WIDTH = FT_OFF + FT_WIDTH
ROPE_BASE = 10000.0
N_EXPERTS = 32
TOP_K = 4
D_FF = D_MODEL // 4
SWIGLU_ALPHA = 1.702
SWIGLU_LIMIT = 7.0
LN_EPS = 1e-5
NEG_INF = -1e30
ALPHA = (2 * DEPTH_NORM) ** 0.25

LANES = 128
ROW_TILE = 256
MOE_BLOCK = 128
ROW_CHUNKS = D_MODEL // LANES
KEY_ROWS = 16
Q_ROWS = 8
VMEM_LIMIT = 56 * 1024 * 1024

F32 = jnp.float32
BF16 = jnp.bfloat16


def _cparams(sem):
    return pltpu.CompilerParams(dimension_semantics=sem, vmem_limit_bytes=VMEM_LIMIT)


def _dot(a, b):
    return jnp.dot(a, b, preferred_element_type=F32)


def _dot_nt(a, b):
    return lax.dot_general(a, b, (((1,), (1,)), ((), ())), preferred_element_type=F32)


def _ln_rows(x):
    mu = jnp.mean(x, axis=-1, keepdims=True)
    xc = x - mu
    var = jnp.mean(xc * xc, axis=-1, keepdims=True)
    return xc * lax.rsqrt(var + LN_EPS)


def _ada_kernel(c_ref, w_ref, b_ref, o_ref):
    c = c_ref[...]
    s = c * jax.nn.sigmoid(c)
    o_ref[0] = jnp.dot(s, w_ref[0], preferred_element_type=F32,
                       precision=lax.Precision.HIGHEST) + b_ref[0]


def _ada_mod(cond, w_ada, b_ada):
    depth, d, n = w_ada.shape
    tn = 1536
    return pl.pallas_call(
        _ada_kernel,
        grid=(depth, n // tn),
        in_specs=[pl.BlockSpec((8, d), lambda l, j: (0, 0)),
                  pl.BlockSpec((1, d, tn), lambda l, j: (l, 0, j)),
                  pl.BlockSpec((1, 1, tn), lambda l, j: (l, 0, j))],
        out_specs=pl.BlockSpec((1, 8, tn), lambda l, j: (l, 0, j)),
        out_shape=jax.ShapeDtypeStruct((depth, 8, n), F32),
        compiler_params=_cparams(("arbitrary", "arbitrary")),
        name="ada_mod",
    )(cond, w_ada, b_ada.reshape(depth, 1, n))


def _proj_kernel(x_ref, sh_ref, sc_ref, w_ref, cos_ref, sin_ref,
                 qp_ref, qr_ref, kr_ref, v_ref, sg_ref, ft_ref):
    y = _ln_rows(x_ref[...])
    h = (y * (1.0 + sc_ref[0]) + sh_ref[0]).astype(BF16)
    cos = cos_ref[...]
    sin = sin_ref[...]
    lane = lax.broadcasted_iota(jnp.int32, cos.shape, 1)
    first = (lane % 64) < 32

    def rope(z):
        swapped = jnp.where(first, pltpu.roll(z, 96, axis=1), pltpu.roll(z, 32, axis=1))
        return z * cos + swapped * sin

    nw = 512
    for j in range(IN_WIDTH // nw):
        z = _dot(h, w_ref[:, j * nw:(j + 1) * nw])
        for p in range(nw // LANES):
            col = j * nw + p * LANES
            zp = z[:, p * LANES:(p + 1) * LANES]
            if col < NA_WIDTH:
                qp_ref[:, col:col + LANES] = zp.astype(BF16)
                qr_ref[:, col:col + LANES] = rope(zp).astype(BF16)
            elif col < 2 * NA_WIDTH:
                c0 = col - NA_WIDTH
                kr_ref[:, c0:c0 + LANES] = rope(zp).astype(BF16)
            elif col < SG_OFF:
                c0 = col - 2 * NA_WIDTH
                v_ref[:, c0:c0 + LANES] = zp.astype(BF16)
            elif col < FT_OFF:
                c0 = col - SG_OFF
                sg_ref[:, c0:c0 + LANES] = zp
            else:
                c0 = col - FT_OFF
                ft_ref[:, c0:c0 + LANES] = zp.astype(BF16)


def _proj(xall, shift, scale, w_in_bf, cos_t, sin_t, n_lat_tiles):
    t = xall.shape[0]
    tm = ROW_TILE
    typ = lambda i: (jnp.where(i >= n_lat_tiles, 1, 0), 0, 0)
    row = lambda i: (i, 0)
    return pl.pallas_call(
        _proj_kernel,
        grid=(t // tm,),
        in_specs=[pl.BlockSpec((tm, D_MODEL), row),
                  pl.BlockSpec((1, 1, D_MODEL), typ),
                  pl.BlockSpec((1, 1, D_MODEL), typ),
                  pl.BlockSpec((D_MODEL, IN_WIDTH), lambda i: (0, 0),
                               pipeline_mode=pl.Buffered(1)),
                  pl.BlockSpec((tm, LANES), row),
                  pl.BlockSpec((tm, LANES), row)],
        out_specs=[pl.BlockSpec((tm, NA_WIDTH), row)] * 4
        + [pl.BlockSpec((tm, 2 * SG_WIDTH), row), pl.BlockSpec((tm, FT_WIDTH), row)],
        out_shape=[jax.ShapeDtypeStruct((t, NA_WIDTH), BF16)] * 4
        + [jax.ShapeDtypeStruct((t, 2 * SG_WIDTH), F32),
           jax.ShapeDtypeStruct((t, FT_WIDTH), BF16)],
        compiler_params=_cparams(("arbitrary",)),
        name="ln_proj",
    )(xall, shift, scale, w_in_bf, cos_t, sin_t)


def _na_kernel(qr_ref, qp_ref, k0, k1, k2, k3, v0, v1, v2, v3, kc_ref, vc_ref, bias_ref, o_ref):
    scale = HEAD_DIM ** -0.5
    kwin = jnp.concatenate([k0[...], k1[...], k2[...], k3[...]], axis=0)
    vwin = jnp.concatenate([v0[...], v1[...], v2[...], v3[...]], axis=0)
    kc = kc_ref[...]
    vc = vc_ref[...]
    qsub = 128
    for s in range(Q_ROWS * GRID_W // qsub):
        rows = slice(s * qsub, (s + 1) * qsub)
        s_lat = _dot_nt(qr_ref[rows, :], kwin) * scale + bias_ref[0, 0, rows, :]
        s_ctx = _dot_nt(qp_ref[rows, :], kc) * scale
        m = jnp.maximum(jnp.max(s_lat, axis=-1, keepdims=True),
                        jnp.max(s_ctx, axis=-1, keepdims=True))
        p_lat = jnp.exp(s_lat - m)
        p_ctx = jnp.exp(s_ctx - m)
        den = jnp.sum(p_lat, axis=-1, keepdims=True) + jnp.sum(p_ctx, axis=-1, keepdims=True)
        o = _dot(p_lat.astype(BF16), vwin) + _dot(p_ctx.astype(BF16), vc)
        o_ref[rows, :] = (o / den).astype(BF16)


def _na_attention(qr, qp, kr, v, bias, s_len):
    rows = s_len // GRID_W
    n_rb = rows // Q_ROWS
    qb = Q_ROWS * GRID_W
    kb = 256
    last_kblock = (rows - KEY_ROWS) * GRID_W // kb
    ctx_block = s_len // kb

    def kmap(j):
        return lambda h, i: (jnp.clip(2 * i - 1, 0, last_kblock) + j, h)

    def btype(i):
        return jnp.where(i == 0, 0, jnp.where(i == n_rb - 1, 2, 1))

    qspec = pl.BlockSpec((qb, HEAD_DIM), lambda h, i: (i, h))
    kspecs = [pl.BlockSpec((kb, HEAD_DIM), kmap(j)) for j in range(4)]
    cspec = pl.BlockSpec((kb, HEAD_DIM), lambda h, i: (ctx_block, h))
    return pl.pallas_call(
        _na_kernel,
        grid=(NA_HEADS, n_rb),
        in_specs=[qspec, qspec] + kspecs + kspecs + [cspec, cspec]
        + [pl.BlockSpec((1, 1, qb, KEY_ROWS * GRID_W), lambda h, i: (h, btype(i), 0, 0))],
        out_specs=pl.BlockSpec((qb, HEAD_DIM), lambda h, i: (i, h)),
        out_shape=jax.ShapeDtypeStruct((s_len, NA_WIDTH), BF16),
        compiler_params=_cparams(("arbitrary", "arbitrary")),
        name="na_attention",
    )(qr, qp, kr, kr, kr, kr, v, v, v, v, kr, v, bias)


def _ctx_attn_kernel(q_ref, k_ref, v_ref, o_ref):
    s = _dot_nt(q_ref[...], k_ref[...]) * (HEAD_DIM ** -0.5)
    m = jnp.max(s, axis=-1, keepdims=True)
    p = jnp.exp(s - m)
    den = jnp.sum(p, axis=-1, keepdims=True)
    o_ref[...] = (_dot(p.astype(BF16), v_ref[...]) / den).astype(BF16)


def _ctx_attention(qp, kr, v, s_len, cl):
    blk = s_len // cl
    spec = pl.BlockSpec((cl, HEAD_DIM), lambda h: (blk, h))
    return pl.pallas_call(
        _ctx_attn_kernel,
        grid=(NA_HEADS,),
        in_specs=[spec, spec, spec],
        out_specs=pl.BlockSpec((cl, HEAD_DIM), lambda h: (0, h)),
        out_shape=jax.ShapeDtypeStruct((cl, NA_WIDTH), BF16),
        compiler_params=_cparams(("arbitrary",)),
        name="ctx_attention",
    )(qp, kr, v)


def _na_bias_tables(na_rpb, rows):
    n_rb = rows // Q_ROWS
    rsel = np.zeros((3, Q_ROWS, KEY_ROWS, 2 * NA_KH - 1), np.float32)
    rmask = np.zeros((3, Q_ROWS, KEY_ROWS), bool)
    for t, i in enumerate((0, 1, n_rb - 1)):
        ks = min(max(Q_ROWS * i - NA_KH // 2, 0), rows - KEY_ROWS)
        for qr in range(Q_ROWS):
            r = Q_ROWS * i + qr
            lo = min(max(r - NA_KH // 2, 0), rows - NA_KH)
            for kr in range(KEY_ROWS):
                key = ks + kr
                rmask[t, qr, kr] = lo <= key < lo + NA_KH
                rsel[t, qr, kr, min(max(key - r + NA_KH - 1, 0), 2 * NA_KH - 2)] = 1.0
    csel = np.zeros((GRID_W, GRID_W, 2 * NA_KW - 1), np.float32)
    cmask = np.zeros((GRID_W, GRID_W), bool)
    for qc in range(GRID_W):
        lo = min(max(qc - NA_KW // 2, 0), GRID_W - NA_KW)
        for kc in range(GRID_W):
            cmask[qc, kc] = lo <= kc < lo + NA_KW
            csel[qc, kc, min(max(kc - qc + NA_KW - 1, 0), 2 * NA_KW - 2)] = 1.0
    mask = rmask[:, :, None, :, None] & cmask[None, None, :, None, :]
    hi = lax.Precision.HIGHEST
    u = jnp.einsum('lhab,qkb->lhaqk', na_rpb, jnp.asarray(csel), precision=hi)
    tab = jnp.einsum('tyra,lhaqk->lhtyqrk', jnp.asarray(rsel), u, precision=hi)
    tab = jnp.where(jnp.asarray(mask)[None, None], tab, NEG_INF)
    depth = na_rpb.shape[0]
    return tab.reshape(depth, NA_HEADS, 3, Q_ROWS * GRID_W, KEY_ROWS * GRID_W)


def _sg_kernel(z_ref, w_ref, b_ref, o_ref):
    z = z_ref[...]
    g = 0.5 * z * (1.0 + lax.erf(z * (2.0 ** -0.5)))
    n_chunks = z.shape[0] // SG_CHUNK
    for gi in range(SG_GROUPS):
        u = g[:, gi * LANES:(gi + 1) * LANES]
        vn = _ln_rows(g[:, SG_WIDTH + gi * LANES:SG_WIDTH + (gi + 1) * LANES]).astype(BF16)
        for n in range(n_chunks):
            rows = slice(n * SG_CHUNK, (n + 1) * SG_CHUNK)
            t = _dot(w_ref[gi], vn[rows, :]) + b_ref[gi]
            o_ref[rows, gi * LANES:(gi + 1) * LANES] = (u[rows, :] * t).astype(BF16)


def _spatial_gating(z_sg, sgw_bf, sgb):
    t = z_sg.shape[0]
    tm = ROW_TILE
    return pl.pallas_call(
        _sg_kernel,
        grid=(t // tm,),
        in_specs=[pl.BlockSpec((tm, 2 * SG_WIDTH), lambda i: (i, 0)),
                  pl.BlockSpec((SG_GROUPS, SG_CHUNK, SG_CHUNK), lambda i: (0, 0, 0)),
                  pl.BlockSpec((SG_GROUPS, SG_CHUNK, 1), lambda i: (0, 0, 0))],
        out_specs=pl.BlockSpec((tm, SG_WIDTH), lambda i: (i, 0)),
        out_shape=jax.ShapeDtypeStruct((t, SG_WIDTH), BF16),
        compiler_params=_cparams(("arbitrary",)),
        name="spatial_gating",
    )(z_sg, sgw_bf, sgb)


def _ft1_kernel(z_ref, cs_ref, m1_ref, y_ref):
    n1 = z_ref.shape[0]
    parts = []
    for j in range(z_ref.shape[1] // LANES):
        ab = _dot(z_ref[:, j * LANES:(j + 1) * LANES], cs_ref[...])
        parts.append(jnp.concatenate([ab[:, :LANES], ab[:, LANES:]], axis=0))
    st = jnp.concatenate(parts, axis=1).astype(BF16)
    y = _dot(m1_ref[...], st)
    y_ref[0] = y[:n1].astype(BF16)
    y_ref[1] = y[n1:].astype(BF16)


def _ft2_kernel(y_ref, tw_ref, o_ref, *, scale):
    for j in range(y_ref.shape[1]):
        yy = jnp.concatenate([y_ref[0, j], y_ref[1, j]], axis=0)
        o = _dot(tw_ref[j], yy) * scale
        o_ref[:, j * FT_WIDTH:(j + 1) * FT_WIDTH] = o.astype(BF16)


def _fourier_consts(s_len, cl):
    n2 = 128
    n1 = s_len // n2
    c = np.arange(HEAD_DIM)
    ang = 2 * np.pi * ((c[:, None] * c[None, :]) % HEAD_DIM) / HEAD_DIM
    cs = np.concatenate([np.cos(ang), np.sin(ang)], axis=1)
    a = np.arange(n1)
    ang1 = 2 * np.pi * ((a[:, None] * a[None, :]) % n1) / n1
    fc, fs = np.cos(ang1), np.sin(ang1)
    m1 = np.block([[fc, -fs], [-fs, -fc]])
    ka = jnp.arange(n1, dtype=jnp.int32)[:, None, None]
    kb = jnp.arange(n2, dtype=jnp.int32)[None, :, None]
    nn = jnp.arange(n2, dtype=jnp.int32)[None, None, :]
    ph = (nn * (ka + n1 * kb)) % s_len
    th = ph.astype(F32) * (2 * np.pi / s_len)
    tw = jnp.concatenate([jnp.cos(th), jnp.sin(th)], axis=2).astype(BF16)
    p = np.arange(cl)
    angc = 2 * np.pi * ((p[:, None] * p[None, :]) % cl) / cl
    mc = np.concatenate([np.cos(angc), -np.sin(angc)], axis=1)
    return (jnp.asarray(cs, BF16), jnp.asarray(m1, BF16), tw, jnp.asarray(mc, BF16))


def _fourier_latent(z_ft, cs, m1, tw, s_len):
    n2 = 128
    n1 = s_len // n2
    width = n2 * FT_WIDTH
    cw = 2048
    y = pl.pallas_call(
        _ft1_kernel,
        grid=(width // cw,),
        in_specs=[pl.BlockSpec((n1, cw), lambda j: (0, j)),
                  pl.BlockSpec((HEAD_DIM, 2 * HEAD_DIM), lambda j: (0, 0)),
                  pl.BlockSpec((2 * n1, 2 * n1), lambda j: (0, 0))],
        out_specs=pl.BlockSpec((2, n1, cw), lambda j: (0, 0, j)),
        out_shape=jax.ShapeDtypeStruct((2, n1, width), BF16),
        compiler_params=_cparams(("arbitrary",)),
        name="fourier_stage1",
    )(z_ft[:s_len].reshape(n1, width), cs, m1)
    kab = 4
    out = pl.pallas_call(
        functools.partial(_ft2_kernel, scale=(s_len * HEAD_DIM) ** -0.5),
        grid=(n1 // kab,),
        in_specs=[pl.BlockSpec((2, kab, n2, FT_WIDTH), lambda j: (0, j, 0, 0)),
                  pl.BlockSpec((kab, n2, 2 * n2), lambda j: (j, 0, 0))],
        out_specs=pl.BlockSpec((n2, kab * FT_WIDTH), lambda j: (0, j)),
        out_shape=jax.ShapeDtypeStruct((n2, n1 * FT_WIDTH), BF16),
        compiler_params=_cparams(("arbitrary",)),
        name="fourier_stage2",
    )(y.reshape(2, n1, n2, FT_WIDTH), tw)
    return out.reshape(s_len, FT_WIDTH)


def _ft_ctx_kernel(z_ref, cs_ref, mc_ref, o_ref, *, scale):
    for g in range(FT_GROUPS):
        ab = _dot(z_ref[:, g * LANES:(g + 1) * LANES], cs_ref[...])
        st = jnp.concatenate([ab[:, :LANES], ab[:, LANES:]], axis=0).astype(BF16)
        o_ref[:, g * LANES:(g + 1) * LANES] = (_dot(mc_ref[...], st) * scale).astype(BF16)


def _fourier_ctx(z_ft, cs, mc, s_len, cl):
    blk = s_len // cl
    return pl.pallas_call(
        functools.partial(_ft_ctx_kernel, scale=(cl * HEAD_DIM) ** -0.5),
        grid=(1,),
        in_specs=[pl.BlockSpec((cl, FT_WIDTH), lambda i: (blk, 0)),
                  pl.BlockSpec((HEAD_DIM, 2 * HEAD_DIM), lambda i: (0, 0)),
                  pl.BlockSpec((cl, 2 * cl), lambda i: (0, 0))],
        out_specs=pl.BlockSpec((cl, FT_WIDTH), lambda i: (0, 0)),
        out_shape=jax.ShapeDtypeStruct((cl, FT_WIDTH), BF16),
        compiler_params=_cparams(("arbitrary",)),
        name="fourier_ctx",
    )(z_ft, cs, mc)


def _out_kernel(*refs, with_ctx, n_lat_tiles):
    if with_ctx:
        (ona_l, ona_c, osg, oft_l, oft_c, x_ref, w_ref, g_ref, lng, lnb, sh_ref, sc_ref,
         wr_ref, br_ref, x1_ref, h2_ref, lg_ref, r_scr) = refs
        is_ctx = pl.program_id(0) >= n_lat_tiles
        ona = jnp.where(is_ctx, ona_c[...], ona_l[...])
        oft = jnp.where(is_ctx, oft_c[...], oft_l[...])
    else:
        (ona_l, osg, oft_l, x_ref, w_ref, g_ref, lng, lnb, sh_ref, sc_ref,
         wr_ref, br_ref, x1_ref, h2_ref, lg_ref, r_scr) = refs
        ona = ona_l[...]
        oft = oft_l[...]
    sg = osg[...]
    nw = 512
    for n in range(D_MODEL // nw):
        cols = slice(n * nw, (n + 1) * nw)
        y = (_dot(ona, w_ref[0:NA_WIDTH, cols])
             + _dot(sg, w_ref[NA_WIDTH:NA_WIDTH + SG_WIDTH, cols])
             + _dot(oft, w_ref[NA_WIDTH + SG_WIDTH:, cols]))
        r_scr[:, cols] = ALPHA * x_ref[:, cols] + g_ref[0][:, cols] * y
    x1 = _ln_rows(r_scr[...]) * lng[...] + lnb[...]
    x1_ref[...] = x1
    h2 = _ln_rows(x1) * (1.0 + sc_ref[0]) + sh_ref[0]
    lg_ref[...] = jnp.dot(h2, wr_ref[...], preferred_element_type=F32,
                          precision=lax.Precision.HIGHEST) + br_ref[...]
    tm = h2.shape[0]
    for c in range(ROW_CHUNKS):
        h2_ref[pl.ds(c, tm, stride=ROW_CHUNKS), :] = h2[:, c * LANES:(c + 1) * LANES]


def _out_proj(ona_l, ona_c, osg, oft_l, oft_c, xall, w_out_bf, gate, lng, lnb, shift, scale,
              wr_pad, br_pad, n_rows, n_lat_tiles, with_ctx):
    tm = ROW_TILE
    row = lambda i: (i, 0)
    lat = lambda i: (jnp.minimum(i, n_lat_tiles - 1), 0)
    typ = lambda i: (jnp.where(i >= n_lat_tiles, 1, 0), 0, 0)
    const = lambda i: (0, 0)
    modspec = pl.BlockSpec((1, 1, D_MODEL), typ)
    vec = pl.BlockSpec((1, D_MODEL), const)
    specs = [pl.BlockSpec((tm, NA_WIDTH), lat)]
    args = [ona_l]
    if with_ctx:
        specs.append(pl.BlockSpec((tm, NA_WIDTH), const))
        args.append(ona_c)
    specs.append(pl.BlockSpec((tm, SG_WIDTH), row))
    args.append(osg)
    specs.append(pl.BlockSpec((tm, FT_WIDTH), lat))
    args.append(oft_l)
    if with_ctx:
        specs.append(pl.BlockSpec((tm, FT_WIDTH), const))
        args.append(oft_c)
    specs += [pl.BlockSpec((tm, D_MODEL), row),
              pl.BlockSpec((D_MODEL, D_MODEL), const, pipeline_mode=pl.Buffered(1)),
              modspec, vec, vec, modspec, modspec,
              pl.BlockSpec((D_MODEL, LANES), const),
              pl.BlockSpec((1, LANES), const)]
    args += [xall, w_out_bf, gate, lng, lnb, shift, scale, wr_pad, br_pad]
    return pl.pallas_call(
        functools.partial(_out_kernel, with_ctx=with_ctx, n_lat_tiles=n_lat_tiles),
        grid=(n_rows // tm,),
        in_specs=specs,
        out_specs=[pl.BlockSpec((tm, D_MODEL), row),
                   pl.BlockSpec((tm * ROW_CHUNKS, LANES), row),
                   pl.BlockSpec((tm, LANES), row)],
        out_shape=[jax.ShapeDtypeStruct((n_rows, D_MODEL), F32),
                   jax.ShapeDtypeStruct((n_rows * ROW_CHUNKS, LANES), F32),
                   jax.ShapeDtypeStruct((n_rows, LANES), F32)],
        scratch_shapes=[pltpu.VMEM((tm, D_MODEL), F32)],
        compiler_params=_cparams(("arbitrary",)),
        name="out_proj",
    )(*args)


def _route_kernel(lg_ref, tri_ref, eid_ref, rank_ref, gate_ref, cnt_ref, carry):
    @pl.when(pl.program_id(0) == 0)
    def _():
        carry[...] = jnp.zeros_like(carry)

    work = lg_ref[...]
    lane = lax.broadcasted_iota(jnp.int32, work.shape, 1)
    lane_f = lane.astype(F32)
    vals, ids, hots = [], [], []
    for _ in range(TOP_K):
        m = jnp.max(work, axis=-1, keepdims=True)
        idx = jnp.min(jnp.where(work == m, lane_f, float(LANES)), axis=-1, keepdims=True)
        hot = lane_f == idx
        vals.append(m)
        ids.append(idx)
        hots.append(hot)
        work = jnp.where(hot, -jnp.inf, work)
    exps = [jnp.exp(v - vals[0]) for v in vals]
    den = exps[0] + exps[1] + exps[2] + exps[3]
    multi = jnp.zeros(work.shape, F32)
    for hot in hots:
        multi = multi + hot.astype(F32)
    pref = _dot(tri_ref[...], multi.astype(BF16)) + carry[...]
    eid = jnp.zeros(work.shape, F32)
    rank = jnp.zeros(work.shape, F32)
    gate = jnp.zeros(work.shape, F32)
    for k in range(TOP_K):
        rk = jnp.sum(jnp.where(hots[k], pref, 0.0), axis=-1, keepdims=True)
        sel = lane == k
        eid = jnp.where(sel, ids[k], eid)
        rank = jnp.where(sel, rk, rank)
        gate = jnp.where(sel, exps[k] / den, gate)
    eid_ref[...] = eid.astype(jnp.int32)
    rank_ref[...] = rank.astype(jnp.int32)
    gate_ref[...] = gate
    carry[...] = carry[...] + jnp.sum(multi, axis=0, keepdims=True)
    cnt_ref[...] = jnp.broadcast_to(carry[...], cnt_ref.shape).astype(jnp.int32)


def _route(logits, tri):
    t = logits.shape[0]
    tm = ROW_TILE
    row = lambda i: (i, 0)
    return pl.pallas_call(
        _route_kernel,
        grid=(t // tm,),
        in_specs=[pl.BlockSpec((tm, LANES), row),
                  pl.BlockSpec((tm, tm), lambda i: (0, 0))],
        out_specs=[pl.BlockSpec((tm, LANES), row)] * 3
        + [pl.BlockSpec((8, LANES), lambda i: (0, 0))],
        out_shape=[jax.ShapeDtypeStruct((t, LANES), jnp.int32)] * 2
        + [jax.ShapeDtypeStruct((t, LANES), F32),
           jax.ShapeDtypeStruct((8, LANES), jnp.int32)],
        scratch_shapes=[pltpu.VMEM((1, LANES), F32)],
        compiler_params=_cparams(("arbitrary",)),
        name="route",
    )(logits, tri)


def _slot_row(base_ref, eid_ref, rank_ref, a):
    slot = base_ref[eid_ref[a]] + rank_ref[a]
    return pl.multiple_of(slot * ROW_CHUNKS, ROW_CHUNKS)


def _dispatch_kernel(base_ref, h_ref, eid_ref, rank_ref, xb_in, xb_ref, sem):
    del xb_in
    n_assign = eid_ref.shape[0]
    unroll = 8

    def body(g, carry):
        for u in range(unroll):
            a = g * unroll + u
            src = pl.multiple_of((a // TOP_K) * ROW_CHUNKS, ROW_CHUNKS)
            dst = _slot_row(base_ref, eid_ref, rank_ref, a)
            pltpu.make_async_copy(h_ref.at[pl.ds(src, ROW_CHUNKS)],
                                  xb_ref.at[pl.ds(dst, ROW_CHUNKS)], sem).start()
        return carry

    lax.fori_loop(0, n_assign // unroll, body, 0)
    rows = h_ref.shape[0]
    for _ in range(TOP_K):
        pltpu.make_async_copy(h_ref, xb_ref.at[pl.ds(0, rows)], sem).wait()


def _dispatch(base, h2_rows, eid_flat, rank_flat, n_slots):
    t = eid_flat.shape[0] // TOP_K
    tm = ROW_TILE
    xb0 = jnp.zeros((n_slots * ROW_CHUNKS, LANES), F32)
    gs = pltpu.PrefetchScalarGridSpec(
        num_scalar_prefetch=1,
        grid=(t // tm,),
        in_specs=[pl.BlockSpec((tm * ROW_CHUNKS, LANES), lambda i, b: (i, 0)),
                  pl.BlockSpec((tm * TOP_K,), lambda i, b: (i,), memory_space=pltpu.SMEM),
                  pl.BlockSpec((tm * TOP_K,), lambda i, b: (i,), memory_space=pltpu.SMEM),
                  pl.BlockSpec(memory_space=pl.ANY)],
        out_specs=pl.BlockSpec(memory_space=pl.ANY),
        scratch_shapes=[pltpu.SemaphoreType.DMA(())],
    )
    return pl.pallas_call(
        _dispatch_kernel,
        grid_spec=gs,
        out_shape=jax.ShapeDtypeStruct(xb0.shape, F32),
        input_output_aliases={4: 0},
        compiler_params=_cparams(("arbitrary",)),
        name="dispatch",
    )(base, h2_rows, eid_flat, rank_flat, xb0)


def _expert_kernel(bexp_ref, nused_ref, xb_ref, wgu_ref, bgu_ref, wd_ref, bd_ref, perm_ref,
                   yb_ref, wgu_s, wd_s, x_s):
    b = pl.program_id(0)
    active = b < nused_ref[0]
    fresh = jnp.logical_or(b == 0, bexp_ref[b] != bexp_ref[jnp.maximum(b - 1, 0)])

    @pl.when(jnp.logical_and(active, fresh))
    def _():
        wgu_s[...] = wgu_ref[0].astype(BF16)
        wd_s[...] = _dot(perm_ref[...], wd_ref[0].astype(BF16)).astype(BF16)

    @pl.when(jnp.logical_not(active))
    def _():
        yb_ref[...] = jnp.zeros_like(yb_ref)

    @pl.when(active)
    def _():
        blk = x_s.shape[0]
        for c in range(ROW_CHUNKS):
            x_s[:, c * LANES:(c + 1) * LANES] = (
                xb_ref[pl.ds(c, blk, stride=ROW_CHUNKS), :].astype(BF16))
        gu = _dot(x_s[...], wgu_s[...]) + bgu_ref[0]
        lane = lax.broadcasted_iota(jnp.int32, (blk, LANES), 1)
        even = (lane % 2) == 0
        prods = []
        for c in range(2 * D_FF // LANES):
            guc = gu[:, c * LANES:(c + 1) * LANES]
            glu = jnp.minimum(guc, SWIGLU_LIMIT)
            lin = jnp.clip(guc, -SWIGLU_LIMIT, SWIGLU_LIMIT) + 1.0
            prods.append(glu * jax.nn.sigmoid(SWIGLU_ALPHA * glu)
                         * pltpu.roll(lin, LANES - 1, axis=1))
        merged = [jnp.where(even, prods[2 * m], pltpu.roll(prods[2 * m + 1], 1, axis=1))
                  for m in range(D_FF // LANES)]
        act = jnp.concatenate(merged, axis=1).astype(BF16)
        y = _dot(act, wd_s[...]) + bd_ref[0]
        for c in range(ROW_CHUNKS):
            yb_ref[pl.ds(c, blk, stride=ROW_CHUNKS), :] = y[:, c * LANES:(c + 1) * LANES]


def _experts(bexp, nused, xb, w_gate_up, b_gate_up, w_down, b_down, n_blocks):
    blk = MOE_BLOCK
    rows = blk * ROW_CHUNKS

    def bmap(b, be, nu):
        return (jnp.minimum(b, nu[0] - 1), 0)

    def emap(b, be, nu):
        return (be[jnp.minimum(b, nu[0] - 1)], 0, 0)

    gs = pltpu.PrefetchScalarGridSpec(
        num_scalar_prefetch=2,
        grid=(n_blocks,),
        in_specs=[pl.BlockSpec((rows, LANES), bmap),
                  pl.BlockSpec((1, D_MODEL, 2 * D_FF), emap),
                  pl.BlockSpec((1, 1, 2 * D_FF), emap),
                  pl.BlockSpec((1, D_FF, D_MODEL), emap),
                  pl.BlockSpec((1, 1, D_MODEL), emap),
                  pl.BlockSpec((D_FF, D_FF), lambda b, be, nu: (0, 0))],
        out_specs=pl.BlockSpec((rows, LANES), lambda b, be, nu: (b, 0)),
        scratch_shapes=[pltpu.VMEM((D_MODEL, 2 * D_FF), BF16),
                        pltpu.VMEM((D_FF, D_MODEL), BF16),
                        pltpu.VMEM((blk, D_MODEL), BF16)],
    )
    lane = np.arange(D_FF) % LANES
    unit = (np.arange(D_FF) // LANES) * LANES + lane // 2 + (lane % 2) * (LANES // 2)
    perm = np.zeros((D_FF, D_FF), np.float32)
    perm[np.arange(D_FF), unit] = 1.0
    return pl.pallas_call(
        _expert_kernel,
        grid_spec=gs,
        out_shape=jax.ShapeDtypeStruct(xb.shape, F32),
        compiler_params=_cparams(("arbitrary",)),
        name="experts",
    )(bexp, nused, xb, w_gate_up, b_gate_up.reshape(N_EXPERTS, 1, 2 * D_FF),
      w_down, b_down.reshape(N_EXPERTS, 1, D_MODEL), jnp.asarray(perm, BF16))


def _combine_kernel(base_ref, x1_ref, eid_ref, rank_ref, gate_ref, yb_ref, g_ref, lng, lnb,
                    o_ref, gbuf, r_scr, sem):
    n_assign = eid_ref.shape[0]
    tm = x1_ref.shape[0]
    unroll = 8

    def body(g, carry):
        for u in range(unroll):
            a = g * unroll + u
            src = _slot_row(base_ref, eid_ref, rank_ref, a)
            dst = pl.multiple_of((a // TOP_K) * ROW_CHUNKS, ROW_CHUNKS)
            pltpu.make_async_copy(yb_ref.at[pl.ds(src, ROW_CHUNKS)],
                                  gbuf.at[u % TOP_K, pl.ds(dst, ROW_CHUNKS)], sem).start()
        return carry

    lax.fori_loop(0, n_assign // unroll, body, 0)
    for k in range(TOP_K):
        pltpu.make_async_copy(yb_ref.at[pl.ds(0, tm * ROW_CHUNKS)], gbuf.at[k], sem).wait()
    gate = gate_ref[...]
    for c in range(ROW_CHUNKS):
        cols = slice(c * LANES, (c + 1) * LANES)
        f = gate[:, 0:1] * gbuf[0, pl.ds(c, tm, stride=ROW_CHUNKS), :]
        for k in range(1, TOP_K):
            f = f + gate[:, k:k + 1] * gbuf[k, pl.ds(c, tm, stride=ROW_CHUNKS), :]
        r_scr[:, cols] = ALPHA * x1_ref[:, cols] + g_ref[0][:, cols] * f
    o_ref[...] = _ln_rows(r_scr[...]) * lng[...] + lnb[...]


def _combine(base, x1, eid_flat, rank_flat, gates, yb, gate_mod, lng, lnb, n_lat_tiles):
    t = x1.shape[0]
    tm = ROW_TILE
    gs = pltpu.PrefetchScalarGridSpec(
        num_scalar_prefetch=1,
        grid=(t // tm,),
        in_specs=[pl.BlockSpec((tm, D_MODEL), lambda i, b: (i, 0)),
                  pl.BlockSpec((tm * TOP_K,), lambda i, b: (i,), memory_space=pltpu.SMEM),
                  pl.BlockSpec((tm * TOP_K,), lambda i, b: (i,), memory_space=pltpu.SMEM),
                  pl.BlockSpec((tm, LANES), lambda i, b: (i, 0)),
                  pl.BlockSpec(memory_space=pl.ANY),
                  pl.BlockSpec((1, 1, D_MODEL),
                               lambda i, b: (jnp.where(i >= n_lat_tiles, 1, 0), 0, 0)),
                  pl.BlockSpec((1, D_MODEL), lambda i, b: (0, 0)),
                  pl.BlockSpec((1, D_MODEL), lambda i, b: (0, 0))],
        out_specs=pl.BlockSpec((tm, D_MODEL), lambda i, b: (i, 0)),
        scratch_shapes=[pltpu.VMEM((TOP_K, tm * ROW_CHUNKS, LANES), F32),
                        pltpu.VMEM((tm, D_MODEL), F32),
                        pltpu.SemaphoreType.DMA(())],
    )
    return pl.pallas_call(
        _combine_kernel,
        grid_spec=gs,
        out_shape=jax.ShapeDtypeStruct((t, D_MODEL), F32),
        compiler_params=_cparams(("arbitrary",)),
        name="combine",
    )(base, x1, eid_flat, rank_flat, gates, yb, gate_mod, lng, lnb)


def _rope_tables(s_len, cl):
    t = jnp.arange(s_len, dtype=jnp.int32)
    quarter = HEAD_DIM // 4
    inv = ROPE_BASE ** (-jnp.arange(quarter, dtype=F32) / quarter)
    ang_r = (t // GRID_W).astype(F32)[:, None] * inv
    ang_c = (t % GRID_W).astype(F32)[:, None] * inv
    cr, sr, cc, sc = jnp.cos(ang_r), jnp.sin(ang_r), jnp.cos(ang_c), jnp.sin(ang_c)
    cos_t = jnp.concatenate([cr, cr, cc, cc], axis=1)
    sin_t = jnp.concatenate([-sr, sr, -sc, sc], axis=1)
    cos_t = jnp.concatenate([cos_t, jnp.ones((cl, HEAD_DIM), F32)], axis=0)
    sin_t = jnp.concatenate([sin_t, jnp.zeros((cl, HEAD_DIM), F32)], axis=0)
    return cos_t, sin_t


def _block_plan(counts, n_blocks):
    blocks_per = (counts + MOE_BLOCK - 1) // MOE_BLOCK
    block_end = jnp.cumsum(blocks_per)
    base = ((block_end - blocks_per) * MOE_BLOCK).astype(jnp.int32)
    bexp = jnp.clip(jnp.searchsorted(block_end, jnp.arange(n_blocks), side='right'),
                    0, N_EXPERTS - 1).astype(jnp.int32)
    nused = block_end[-1:].astype(jnp.int32)
    return base, bexp, nused


def kernel(x, c, ctx, c_ctx, w_ada, b_ada, w_in, w_out, sg_w, sg_b, na_rpb, ln1_g, ln1_b,
           ln2_g, ln2_b, w_router, b_router, w_gate_up, b_gate_up, w_down, b_down):
    bsz, s_len, dm = x.shape
    cl = ctx.shape[1]
    depth = w_ada.shape[0]
    assert bsz == 1 and dm == D_MODEL and cl == ROW_TILE
    assert s_len % (GRID_W * KEY_ROWS) == 0 and s_len % ROW_TILE == 0
    t_all = s_len + cl
    n_lat_tiles = s_len // ROW_TILE
    rows = s_len // GRID_W

    cond = jnp.zeros((8, dm), F32).at[0].set(c[0]).at[1].set(c_ctx)
    mod = _ada_mod(cond, w_ada, b_ada)[:, :2].reshape(depth, 2, 6, 1, dm)
    cos_t, sin_t = _rope_tables(s_len, cl)
    bias_all = _na_bias_tables(na_rpb, rows)
    cs, m1, tw, mc = _fourier_consts(s_len, cl)
    tri = jnp.asarray(np.tril(np.ones((ROW_TILE, ROW_TILE), np.float32), -1), BF16)
    wr_pad = jnp.pad(w_router, ((0, 0), (0, 0), (0, LANES - N_EXPERTS)))
    br_pad = jnp.pad(b_router, ((0, 0), (0, LANES - N_EXPERTS)), constant_values=NEG_INF)

    xall = jnp.concatenate([x[0], ctx[0]], axis=0)
    for l in range(depth):
        last = l == depth - 1
        m = lambda j: mod[l, :, j]
        qp, qr, kr, v, z_sg, z_ft = _proj(xall, m(0), m(1), w_in[l].astype(BF16),
                                          cos_t, sin_t, n_lat_tiles)
        ona_l = _na_attention(qr, qp, kr, v, bias_all[l], s_len)
        osg = _spatial_gating(z_sg, sg_w[l].astype(BF16), sg_b[l][:, :, None])
        oft_l = _fourier_latent(z_ft, cs, m1, tw, s_len)
        if last:
            ona_c = oft_c = None
            n_rows = s_len
        else:
            ona_c = _ctx_attention(qp, kr, v, s_len, cl)
            oft_c = _fourier_ctx(z_ft, cs, mc, s_len, cl)
            n_rows = t_all
        x1, h2_rows, logits = _out_proj(
            ona_l, ona_c, osg, oft_l, oft_c, xall, w_out[l].astype(BF16), m(2),
            ln1_g[l][None], ln1_b[l][None], m(3), m(4), wr_pad[l], br_pad[l][None],
            n_rows, n_lat_tiles, not last)
        eid, rank, gates, counts = _route(logits, tri)
        n_assign = n_rows * TOP_K
        n_blocks = -(-n_assign // MOE_BLOCK) + N_EXPERTS
        base, bexp, nused = _block_plan(counts[0, :N_EXPERTS], n_blocks)
        eid_flat = eid[:, :TOP_K].reshape(n_assign)
        rank_flat = rank[:, :TOP_K].reshape(n_assign)
        xb = _dispatch(base, h2_rows, eid_flat, rank_flat, n_blocks * MOE_BLOCK)
        yb = _experts(bexp, nused, xb, w_gate_up[l], b_gate_up[l], w_down[l], b_down[l],
                      n_blocks)
        xall = _combine(base, x1, eid_flat, rank_flat, gates, yb, m(5),
                        ln2_g[l][None], ln2_b[l][None], n_lat_tiles)
    return xall[None]
```

```python
import functools
import math

import numpy as np
import jax
import jax.numpy as jnp
from jax import lax
from jax.experimental import pallas as pl
from jax.experimental.pallas import tpu as pltpu

D_MODEL = 2048
DEPTH_NORM = 4
GRID_W = 64
HEAD_DIM = 128
NA_HEADS = 8
NA_WIDTH = NA_HEADS * HEAD_DIM
NA_KH = 8
NA_KW = 16
SG_GROUPS = 4
SG_WIDTH = 512
SG_CHUNK = 128
FT_GROUPS = 4
FT_WIDTH = 512
SG_OFF = 3 * NA_WIDTH
FT_OFF = SG_OFF + 2 * SG_WIDTH
IN_WIDTH = FT_OFF + FT_WIDTH
ROPE_BASE = 10000.0
N_EXPERTS = 32
TOP_K = 4
D_FF = D_MODEL // 4
SWIGLU_ALPHA = 1.702
SWIGLU_LIMIT = 7.0
LN_EPS = 1e-5
NEG_INF = -1e30
ALPHA = (2 * DEPTH_NORM) ** 0.25

LANES = 128
ROW_TILE = 256
MOE_BLOCK = 256
NORM_ROWS = 32
KEY_ROWS = 16
Q_ROWS = 8
VMEM_LIMIT = 56 * 1024 * 1024

F32 = jnp.float32
BF16 = jnp.bfloat16


def _cparams(sem, row_dma=False):
    return pltpu.CompilerParams(dimension_semantics=sem, vmem_limit_bytes=VMEM_LIMIT,
                                disable_bounds_checks=row_dma)


def _dot(a, b):
    return jnp.dot(a, b, preferred_element_type=F32)


def _dot_nt(a, b):
    return lax.dot_general(a, b, (((1,), (1,)), ((), ())), preferred_element_type=F32)


def _ln_rows(x):
    mu = jnp.mean(x, axis=-1, keepdims=True)
    xc = x - mu
    var = jnp.mean(xc * xc, axis=-1, keepdims=True)
    return xc * lax.rsqrt(var + LN_EPS)


def _ada_kernel(c_ref, w_ref, b_ref, o_ref):
    c = c_ref[...]
    s = c * jax.nn.sigmoid(c)
    o_ref[0] = jnp.dot(s, w_ref[0], preferred_element_type=F32,
                       precision=lax.Precision.HIGHEST) + b_ref[0]


def _ada_mod(cond, w_ada, b_ada):
    depth, d, n = w_ada.shape
    tn = 1536
    return pl.pallas_call(
        _ada_kernel,
        grid=(depth, n // tn),
        in_specs=[pl.BlockSpec((8, d), lambda l, j: (0, 0)),
                  pl.BlockSpec((1, d, tn), lambda l, j: (l, 0, j)),
                  pl.BlockSpec((1, 1, tn), lambda l, j: (l, 0, j))],
        out_specs=pl.BlockSpec((1, 8, tn), lambda l, j: (l, 0, j)),
        out_shape=jax.ShapeDtypeStruct((depth, 8, n), F32),
        compiler_params=_cparams(("arbitrary", "arbitrary")),
        name="ada_mod",
    )(cond, w_ada, b_ada.reshape(depth, 1, n))


def _proj_kernel(x_ref, sh_ref, sc_ref, w_ref, cos_ref, sin_ref,
                 qp_ref, qr_ref, kr_ref, v_ref, sg_ref, ft_ref):
    y = _ln_rows(x_ref[...])
    h = (y * (1.0 + sc_ref[0]) + sh_ref[0]).astype(BF16)
    cos = cos_ref[...]
    sin = sin_ref[...]
    lane = lax.broadcasted_iota(jnp.int32, cos.shape, 1)
    first = (lane % 64) < 32

    def rope(z):
        swapped = jnp.where(first, pltpu.roll(z, 96, axis=1), pltpu.roll(z, 32, axis=1))
        return z * cos + swapped * sin

    nw = 512
    for j in range(IN_WIDTH // nw):
        z = _dot(h, w_ref[:, j * nw:(j + 1) * nw])
        for p in range(nw // LANES):
            col = j * nw + p * LANES
            zp = z[:, p * LANES:(p + 1) * LANES]
            if col < NA_WIDTH:
                qp_ref[:, col:col + LANES] = zp.astype(BF16)
                qr_ref[:, col:col + LANES] = rope(zp).astype(BF16)
            elif col < 2 * NA_WIDTH:
                c0 = col - NA_WIDTH
                kr_ref[:, c0:c0 + LANES] = rope(zp).astype(BF16)
            elif col < SG_OFF:
                c0 = col - 2 * NA_WIDTH
                v_ref[:, c0:c0 + LANES] = zp.astype(BF16)
            elif col < FT_OFF:
                c0 = col - SG_OFF
                sg_ref[:, c0:c0 + LANES] = zp
            else:
                c0 = col - FT_OFF
                ft_ref[:, c0:c0 + LANES] = zp.astype(BF16)


def _proj(xall, shift, scale, w_in_bf, cos_t, sin_t, layer, n_lat_tiles):
    t = xall.shape[0]
    tm = ROW_TILE
    typ = lambda i: (jnp.where(i >= n_lat_tiles, 1, 0), 0, 0)
    row = lambda i: (i, 0)
    return pl.pallas_call(
        _proj_kernel,
        grid=(t // tm,),
        in_specs=[pl.BlockSpec((tm, D_MODEL), row),
                  pl.BlockSpec((1, 1, D_MODEL), typ),
                  pl.BlockSpec((1, 1, D_MODEL), typ),
                  pl.BlockSpec((None, D_MODEL, IN_WIDTH), lambda i: (layer, 0, 0),
                               pipeline_mode=pl.Buffered(1)),
                  pl.BlockSpec((tm, LANES), row),
                  pl.BlockSpec((tm, LANES), row)],
        out_specs=[pl.BlockSpec((tm, NA_WIDTH), row)] * 4
        + [pl.BlockSpec((tm, 2 * SG_WIDTH), row), pl.BlockSpec((tm, FT_WIDTH), row)],
        out_shape=[jax.ShapeDtypeStruct((t, NA_WIDTH), BF16)] * 4
        + [jax.ShapeDtypeStruct((t, 2 * SG_WIDTH), F32),
           jax.ShapeDtypeStruct((t, FT_WIDTH), BF16)],
        compiler_params=_cparams(("arbitrary",)),
        name="ln_proj",
    )(xall, shift, scale, w_in_bf, cos_t, sin_t)


def _na_kernel(qr_ref, qp_ref, k0, k1, k2, k3, v0, v1, v2, v3, kc_ref, vc_ref, pb_ref, vrow_ref,
               o_ref, *, n_rb):
    scale = HEAD_DIM ** -0.5
    i = pl.program_id(1)
    off = jnp.where(i == 0, 0, jnp.where(i == n_rb - 1, -(KEY_ROWS - Q_ROWS), -(NA_KH // 2)))
    kwin = jnp.concatenate([k0[...], k1[...], k2[...], k3[...]], axis=0)
    vwin = jnp.concatenate([v0[...], v1[...], v2[...], v3[...]], axis=0)
    kc = kc_ref[...]
    vc = vc_ref[...]
    qsub = 128
    n_pair = KEY_ROWS // 2
    for s in range(Q_ROWS * GRID_W // qsub):
        rows = slice(s * qsub, (s + 1) * qsub)
        bias_rows = []
        for qr in range(s * qsub // GRID_W, (s + 1) * qsub // GRID_W):
            tiles = [pb_ref[0, 0, jnp.clip(2 * j - qr + off + NA_KH, 0, 2 * NA_KH - 1)]
                     for j in range(n_pair)]
            bias_rows.append(jnp.concatenate(tiles, axis=1) + vrow_ref[0, qr])
        bias = jnp.concatenate(bias_rows, axis=0)
        s_lat = _dot_nt(qr_ref[rows, :], kwin) * scale + bias
        s_ctx = _dot_nt(qp_ref[rows, :], kc) * scale
        m = jnp.maximum(jnp.max(s_lat, axis=-1, keepdims=True),
                        jnp.max(s_ctx, axis=-1, keepdims=True))
        p_lat = jnp.exp(s_lat - m)
        p_ctx = jnp.exp(s_ctx - m)
        den = jnp.sum(p_lat, axis=-1, keepdims=True) + jnp.sum(p_ctx, axis=-1, keepdims=True)
        o = _dot(p_lat.astype(BF16), vwin) + _dot(p_ctx.astype(BF16), vc)
        o_ref[rows, :] = (o / den).astype(BF16)


def _na_attention(qr, qp, kr, v, pb, vrow, layer, s_len):
    rows = s_len // GRID_W
    n_rb = rows // Q_ROWS
    qb = Q_ROWS * GRID_W
    kb = 256
    last_kblock = (rows - KEY_ROWS) * GRID_W // kb
    ctx_block = s_len // kb

    def kmap(j):
        return lambda h, i: (jnp.clip(2 * i - 1, 0, last_kblock) + j, h)

    def btype(i):
        return jnp.where(i == 0, 0, jnp.where(i == n_rb - 1, 2, 1))

    qspec = pl.BlockSpec((qb, HEAD_DIM), lambda h, i: (i, h))
    kspecs = [pl.BlockSpec((kb, HEAD_DIM), kmap(j)) for j in range(4)]
    cspec = pl.BlockSpec((kb, HEAD_DIM), lambda h, i: (ctx_block, h))
    return pl.pallas_call(
        functools.partial(_na_kernel, n_rb=n_rb),
        grid=(NA_HEADS, n_rb),
        in_specs=[qspec, qspec] + kspecs + kspecs + [cspec, cspec]
        + [pl.BlockSpec((1, 1, 2 * NA_KH, GRID_W, 2 * GRID_W), lambda h, i: (layer, h, 0, 0, 0)),
           pl.BlockSpec((1, Q_ROWS, 1, KEY_ROWS * GRID_W), lambda h, i: (btype(i), 0, 0, 0))],
        out_specs=pl.BlockSpec((qb, HEAD_DIM), lambda h, i: (i, h)),
        out_shape=jax.ShapeDtypeStruct((s_len, NA_WIDTH), BF16),
        compiler_params=_cparams(("arbitrary", "arbitrary")),
        name="na_attention",
    )(qr, qp, kr, kr, kr, kr, v, v, v, v, kr, v, pb, vrow)


def _ctx_attn_kernel(q_ref, k_ref, v_ref, o_ref):
    s = _dot_nt(q_ref[...], k_ref[...]) * (HEAD_DIM ** -0.5)
    m = jnp.max(s, axis=-1, keepdims=True)
    p = jnp.exp(s - m)
    den = jnp.sum(p, axis=-1, keepdims=True)
    o_ref[...] = (_dot(p.astype(BF16), v_ref[...]) / den).astype(BF16)


def _ctx_attention(qp, kr, v, s_len, cl):
    blk = s_len // cl
    spec = pl.BlockSpec((cl, HEAD_DIM), lambda h: (blk, h))
    return pl.pallas_call(
        _ctx_attn_kernel,
        grid=(NA_HEADS,),
        in_specs=[spec, spec, spec],
        out_specs=pl.BlockSpec((cl, HEAD_DIM), lambda h: (0, h)),
        out_shape=jax.ShapeDtypeStruct((cl, NA_WIDTH), BF16),
        compiler_params=_cparams(("arbitrary",)),
        name="ctx_attention",
    )(qp, kr, v)


def _na_bias_tables(na_rpb, rows):
    n_rb = rows // Q_ROWS
    rmask = np.zeros((3, Q_ROWS, KEY_ROWS), bool)
    for t, i in enumerate((0, 1, n_rb - 1)):
        ks = min(max(Q_ROWS * i - NA_KH // 2, 0), rows - KEY_ROWS)
        for qr in range(Q_ROWS):
            r = Q_ROWS * i + qr
            lo = min(max(r - NA_KH // 2, 0), rows - NA_KH)
            for kr in range(KEY_ROWS):
                rmask[t, qr, kr] = lo <= ks + kr < lo + NA_KH
    vrow = np.where(np.repeat(rmask, GRID_W, axis=2), 0.0, NEG_INF).astype(np.float32)
    csel = np.zeros((GRID_W, GRID_W, 2 * NA_KW - 1), np.float32)
    cmask = np.zeros((GRID_W, GRID_W), bool)
    for qc in range(GRID_W):
        lo = min(max(qc - NA_KW // 2, 0), GRID_W - NA_KW)
        for kc in range(GRID_W):
            cmask[qc, kc] = lo <= kc < lo + NA_KW
            csel[qc, kc, min(max(kc - qc + NA_KW - 1, 0), 2 * NA_KW - 2)] = 1.0
    b = jnp.einsum('lhab,qkb->lhaqk', na_rpb, jnp.asarray(csel), precision=lax.Precision.HIGHEST)
    b = jnp.where(jnp.asarray(cmask), b, NEG_INF)
    b = jnp.pad(b, ((0, 0), (0, 0), (1, 1), (0, 0), (0, 0)))
    pb = jnp.concatenate([b[:, :, :-1], b[:, :, 1:]], axis=-1)
    return pb, jnp.asarray(vrow)[:, :, None, :]


def _sg_kernel(z_ref, w_ref, b_ref, o_ref):
    z = z_ref[...]
    g = 0.5 * z * (1.0 + lax.erf(z * (2.0 ** -0.5)))
    n_chunks = z.shape[0] // SG_CHUNK
    for gi in range(SG_GROUPS):
        u = g[:, gi * LANES:(gi + 1) * LANES]
        vn = _ln_rows(g[:, SG_WIDTH + gi * LANES:SG_WIDTH + (gi + 1) * LANES]).astype(BF16)
        for n in range(n_chunks):
            rows = slice(n * SG_CHUNK, (n + 1) * SG_CHUNK)
            t = _dot(w_ref[gi], vn[rows, :]) + b_ref[gi]
            o_ref[rows, gi * LANES:(gi + 1) * LANES] = (u[rows, :] * t).astype(BF16)


def _spatial_gating(z_sg, sgw_bf, sgb):
    t = z_sg.shape[0]
    tm = ROW_TILE
    return pl.pallas_call(
        _sg_kernel,
        grid=(t // tm,),
        in_specs=[pl.BlockSpec((tm, 2 * SG_WIDTH), lambda i: (i, 0)),
                  pl.BlockSpec((SG_GROUPS, SG_CHUNK, SG_CHUNK), lambda i: (0, 0, 0)),
                  pl.BlockSpec((SG_GROUPS, SG_CHUNK, 1), lambda i: (0, 0, 0))],
        out_specs=pl.BlockSpec((tm, SG_WIDTH), lambda i: (i, 0)),
        out_shape=jax.ShapeDtypeStruct((t, SG_WIDTH), BF16),
        compiler_params=_cparams(("arbitrary",)),
        name="spatial_gating",
    )(z_sg, sgw_bf, sgb)


def _ft1_kernel(z_ref, cs_ref, m1_ref, y_ref):
    n1 = z_ref.shape[0]
    parts = []
    for j in range(z_ref.shape[1] // LANES):
        ab = _dot(z_ref[:, j * LANES:(j + 1) * LANES], cs_ref[...])
        parts.append(jnp.concatenate([ab[:, :LANES], ab[:, LANES:]], axis=0))
    st = jnp.concatenate(parts, axis=1).astype(BF16)
    y = _dot(m1_ref[...], st)
    y_ref[0] = y[:n1].astype(BF16)
    y_ref[1] = y[n1:].astype(BF16)


def _ft2_kernel(y_ref, tw_ref, o_ref, *, scale):
    for j in range(y_ref.shape[1]):
        yy = jnp.concatenate([y_ref[0, j], y_ref[1, j]], axis=0)
        o = _dot(tw_ref[j], yy) * scale
        o_ref[:, j * FT_WIDTH:(j + 1) * FT_WIDTH] = o.astype(BF16)


def _fourier_consts(s_len, cl):
    n2 = 128
    n1 = s_len // n2
    c = np.arange(HEAD_DIM)
    ang = 2 * np.pi * ((c[:, None] * c[None, :]) % HEAD_DIM) / HEAD_DIM
    cs = np.concatenate([np.cos(ang), np.sin(ang)], axis=1)
    a = np.arange(n1)
    ang1 = 2 * np.pi * ((a[:, None] * a[None, :]) % n1) / n1
    fc, fs = np.cos(ang1), np.sin(ang1)
    m1 = np.block([[fc, -fs], [-fs, -fc]])
    ka = jnp.arange(n1, dtype=jnp.int32)[:, None, None]
    kb = jnp.arange(n2, dtype=jnp.int32)[None, :, None]
    nn = jnp.arange(n2, dtype=jnp.int32)[None, None, :]
    ph = (nn * (ka + n1 * kb)) % s_len
    th = ph.astype(F32) * (2 * np.pi / s_len)
    tw = jnp.concatenate([jnp.cos(th), jnp.sin(th)], axis=2).astype(BF16)
    p = np.arange(cl)
    angc = 2 * np.pi * ((p[:, None] * p[None, :]) % cl) / cl
    mc = np.concatenate([np.cos(angc), -np.sin(angc)], axis=1)
    return (jnp.asarray(cs, BF16), jnp.asarray(m1, BF16), tw, jnp.asarray(mc, BF16))


def _fourier_latent(z_ft, cs, m1, tw, s_len):
    n2 = 128
    n1 = s_len // n2
    width = n2 * FT_WIDTH
    cw = 2048
    y = pl.pallas_call(
        _ft1_kernel,
        grid=(width // cw,),
        in_specs=[pl.BlockSpec((n1, cw), lambda j: (0, j)),
                  pl.BlockSpec((HEAD_DIM, 2 * HEAD_DIM), lambda j: (0, 0)),
                  pl.BlockSpec((2 * n1, 2 * n1), lambda j: (0, 0))],
        out_specs=pl.BlockSpec((2, n1, cw), lambda j: (0, 0, j)),
        out_shape=jax.ShapeDtypeStruct((2, n1, width), BF16),
        compiler_params=_cparams(("arbitrary",)),
        name="fourier_stage1",
    )(z_ft[:s_len].reshape(n1, width), cs, m1)
    kab = 4
    out = pl.pallas_call(
        functools.partial(_ft2_kernel, scale=(s_len * HEAD_DIM) ** -0.5),
        grid=(n1 // kab,),
        in_specs=[pl.BlockSpec((2, kab, n2, FT_WIDTH), lambda j: (0, j, 0, 0)),
                  pl.BlockSpec((kab, n2, 2 * n2), lambda j: (j, 0, 0))],
        out_specs=pl.BlockSpec((n2, kab * FT_WIDTH), lambda j: (0, j)),
        out_shape=jax.ShapeDtypeStruct((n2, n1 * FT_WIDTH), BF16),
        compiler_params=_cparams(("arbitrary",)),
        name="fourier_stage2",
    )(y.reshape(2, n1, n2, FT_WIDTH), tw)
    return out.reshape(s_len, FT_WIDTH)


def _ft_ctx_kernel(z_ref, cs_ref, mc_ref, o_ref, *, scale):
    for g in range(FT_GROUPS):
        ab = _dot(z_ref[:, g * LANES:(g + 1) * LANES], cs_ref[...])
        st = jnp.concatenate([ab[:, :LANES], ab[:, LANES:]], axis=0).astype(BF16)
        o_ref[:, g * LANES:(g + 1) * LANES] = (_dot(mc_ref[...], st) * scale).astype(BF16)


def _fourier_ctx(z_ft, cs, mc, s_len, cl):
    blk = s_len // cl
    return pl.pallas_call(
        functools.partial(_ft_ctx_kernel, scale=(cl * HEAD_DIM) ** -0.5),
        grid=(1,),
        in_specs=[pl.BlockSpec((cl, FT_WIDTH), lambda i: (blk, 0)),
                  pl.BlockSpec((HEAD_DIM, 2 * HEAD_DIM), lambda i: (0, 0)),
                  pl.BlockSpec((cl, 2 * cl), lambda i: (0, 0))],
        out_specs=pl.BlockSpec((cl, FT_WIDTH), lambda i: (0, 0)),
        out_shape=jax.ShapeDtypeStruct((cl, FT_WIDTH), BF16),
        compiler_params=_cparams(("arbitrary",)),
        name="fourier_ctx",
    )(z_ft, cs, mc)


def _out_kernel(*refs, with_ctx, n_lat_tiles):
    if with_ctx:
        (ona_l, ona_c, osg, oft_l, oft_c, x_ref, w_ref, g_ref, lng, lnb, sh_ref, sc_ref,
         wr_ref, br_ref, x1_ref, h2_ref, lg_ref, r_scr, hs_scr) = refs
        is_ctx = pl.program_id(0) >= n_lat_tiles
        ona = jnp.where(is_ctx, ona_c[...], ona_l[...])
        oft = jnp.where(is_ctx, oft_c[...], oft_l[...])
    else:
        (ona_l, osg, oft_l, x_ref, w_ref, g_ref, lng, lnb, sh_ref, sc_ref,
         wr_ref, br_ref, x1_ref, h2_ref, lg_ref, r_scr, hs_scr) = refs
        ona = ona_l[...]
        oft = oft_l[...]
    sg = osg[...]
    nw = 512
    for n in range(D_MODEL // nw):
        cols = slice(n * nw, (n + 1) * nw)
        y = (_dot(ona, w_ref[0:NA_WIDTH, cols])
             + _dot(sg, w_ref[NA_WIDTH:NA_WIDTH + SG_WIDTH, cols])
             + _dot(oft, w_ref[NA_WIDTH + SG_WIDTH:, cols]))
        r_scr[:, cols] = ALPHA * x_ref[:, cols] + g_ref[0][:, cols] * y

    def norm_rows(ci, carry):
        rows = pl.ds(pl.multiple_of(ci * NORM_ROWS, NORM_ROWS), NORM_ROWS)
        x1 = _ln_rows(r_scr[rows, :]) * lng[...] + lnb[...]
        x1_ref[rows, :] = x1
        h2 = _ln_rows(x1) * (1.0 + sc_ref[0]) + sh_ref[0]
        h2_ref[rows, :] = h2
        hi = h2.astype(BF16)
        hs_scr[0, rows, :] = hi
        hs_scr[1, rows, :] = (h2 - hi.astype(F32)).astype(BF16)
        return carry

    lax.fori_loop(0, r_scr.shape[0] // NORM_ROWS, norm_rows, 0)
    lg_ref[...] = (_dot(hs_scr[0], wr_ref[0]) + _dot(hs_scr[1], wr_ref[0])
                   + _dot(hs_scr[0], wr_ref[1]) + br_ref[...])


def _out_proj(ona_l, ona_c, osg, oft_l, oft_c, xall, w_out_bf, gate, lng, lnb, shift, scale,
              wr_split, br_pad, layer, n_rows, n_lat_tiles, with_ctx):
    tm = ROW_TILE
    row = lambda i: (i, 0)
    lat = lambda i: (jnp.minimum(i, n_lat_tiles - 1), 0)
    typ = lambda i: (jnp.where(i >= n_lat_tiles, 1, 0), 0, 0)
    const = lambda i: (0, 0)
    modspec = pl.BlockSpec((1, 1, D_MODEL), typ)
    vec = pl.BlockSpec((1, D_MODEL), const)
    specs = [pl.BlockSpec((tm, NA_WIDTH), lat)]
    args = [ona_l]
    if with_ctx:
        specs.append(pl.BlockSpec((tm, NA_WIDTH), const))
        args.append(ona_c)
    specs.append(pl.BlockSpec((tm, SG_WIDTH), row))
    args.append(osg)
    specs.append(pl.BlockSpec((tm, FT_WIDTH), lat))
    args.append(oft_l)
    if with_ctx:
        specs.append(pl.BlockSpec((tm, FT_WIDTH), const))
        args.append(oft_c)
    specs += [pl.BlockSpec((tm, D_MODEL), row),
              pl.BlockSpec((None, D_MODEL, D_MODEL), lambda i: (layer, 0, 0),
                           pipeline_mode=pl.Buffered(1)),
              modspec, vec, vec, modspec, modspec,
              pl.BlockSpec((None, 2, D_MODEL, LANES), lambda i: (layer, 0, 0, 0)),
              pl.BlockSpec((1, LANES), const)]
    args += [xall, w_out_bf, gate, lng, lnb, shift, scale, wr_split, br_pad]
    return pl.pallas_call(
        functools.partial(_out_kernel, with_ctx=with_ctx, n_lat_tiles=n_lat_tiles),
        grid=(n_rows // tm,),
        in_specs=specs,
        out_specs=[pl.BlockSpec((tm, D_MODEL), row),
                   pl.BlockSpec((tm, D_MODEL), row),
                   pl.BlockSpec((tm, LANES), row)],
        out_shape=[jax.ShapeDtypeStruct((n_rows, D_MODEL), F32),
                   jax.ShapeDtypeStruct((n_rows, D_MODEL), F32),
                   jax.ShapeDtypeStruct((n_rows, LANES), F32)],
        scratch_shapes=[pltpu.VMEM((tm, D_MODEL), F32),
                        pltpu.VMEM((2, tm, D_MODEL), BF16)],
        compiler_params=_cparams(("arbitrary",)),
        name="out_proj",
    )(*args)


def _route_kernel(lg_ref, tri_ref, eid_ref, rank_ref, gate_ref, cnt_ref, carry):
    @pl.when(pl.program_id(0) == 0)
    def _():
        carry[...] = jnp.zeros_like(carry)

    work = lg_ref[...]
    lane = lax.broadcasted_iota(jnp.int32, work.shape, 1)
    lane_f = lane.astype(F32)
    vals, ids, hots = [], [], []
    for _ in range(TOP_K):
        m = jnp.max(work, axis=-1, keepdims=True)
        idx = jnp.min(jnp.where(work == m, lane_f, float(LANES)), axis=-1, keepdims=True)
        hot = lane_f == idx
        vals.append(m)
        ids.append(idx)
        hots.append(hot)
        work = jnp.where(hot, -jnp.inf, work)
    exps = [jnp.exp(v - vals[0]) for v in vals]
    den = exps[0] + exps[1] + exps[2] + exps[3]
    multi = jnp.zeros(work.shape, F32)
    for hot in hots:
        multi = multi + hot.astype(F32)
    pref = _dot(tri_ref[...], multi.astype(BF16)) + carry[...]
    eid = jnp.zeros(work.shape, F32)
    rank = jnp.zeros(work.shape, F32)
    gate = jnp.zeros(work.shape, F32)
    for k in range(TOP_K):
        rk = jnp.sum(jnp.where(hots[k], pref, 0.0), axis=-1, keepdims=True)
        sel = lane == k
        eid = jnp.where(sel, ids[k], eid)
        rank = jnp.where(sel, rk, rank)
        gate = jnp.where(sel, exps[k] / den, gate)
    eid_ref[...] = eid.astype(jnp.int32)
    rank_ref[...] = rank.astype(jnp.int32)
    gate_ref[...] = gate
    carry[...] = carry[...] + jnp.sum(multi, axis=0, keepdims=True)
    cnt_ref[...] = jnp.broadcast_to(carry[...], cnt_ref.shape).astype(jnp.int32)


def _route(logits, tri):
    t = logits.shape[0]
    tm = ROW_TILE
    row = lambda i: (i, 0)
    return pl.pallas_call(
        _route_kernel,
        grid=(t // tm,),
        in_specs=[pl.BlockSpec((tm, LANES), row),
                  pl.BlockSpec((tm, tm), lambda i: (0, 0))],
        out_specs=[pl.BlockSpec((tm, LANES), row)] * 3
        + [pl.BlockSpec((8, LANES), lambda i: (0, 0))],
        out_shape=[jax.ShapeDtypeStruct((t, LANES), jnp.int32)] * 2
        + [jax.ShapeDtypeStruct((t, LANES), F32),
           jax.ShapeDtypeStruct((8, LANES), jnp.int32)],
        scratch_shapes=[pltpu.VMEM((1, LANES), F32)],
        compiler_params=_cparams(("arbitrary",)),
        name="route",
    )(logits, tri)


def _dispatch_kernel(pstart_ref, pcount_ref, h_ref, slot_ref, xb_ref, sem, pad_sem):
    tm = h_ref.shape[0]
    first = pl.program_id(0) == 0

    def row_copy(src_row, dst_row, s):
        return pltpu.make_async_copy(h_ref.at[pl.ds(src_row, 1)], xb_ref.at[pl.ds(dst_row, 1)], s)

    def tail_copy(b):
        return pltpu.make_async_copy(h_ref, xb_ref.at[pl.ds(b * MOE_BLOCK, MOE_BLOCK)], pad_sem)

    n_blocks = xb_ref.shape[0] // MOE_BLOCK
    n_used = pcount_ref[N_EXPERTS + 1]

    @pl.when(first)
    def _():
        def per_expert(e, carry):
            def per_row(j, c2):
                row_copy(0, pstart_ref[e] + j, pad_sem).start()
                return c2
            return lax.fori_loop(0, pcount_ref[e], per_row, carry)
        lax.fori_loop(0, N_EXPERTS, per_expert, 0)

        def per_tail(b, carry):
            tail_copy(b).start()
            return carry
        lax.fori_loop(n_used, n_blocks, per_tail, 0)

    def per_token(g, carry):
        for u in range(2):
            t = g * 2 + u
            for k in range(TOP_K):
                row_copy(t, slot_ref[t * TOP_K + k], sem).start()
        return carry

    lax.fori_loop(0, tm // 2, per_token, 0)
    for _ in range(TOP_K):
        pltpu.make_async_copy(h_ref, xb_ref.at[pl.ds(0, tm)], sem).wait()

    @pl.when(first)
    def _():
        def wait_row(j, carry):
            row_copy(0, 0, pad_sem).wait()
            return carry
        lax.fori_loop(0, pcount_ref[N_EXPERTS], wait_row, 0)

        def wait_tail(b, carry):
            tail_copy(b).wait()
            return carry
        lax.fori_loop(n_used, n_blocks, wait_tail, 0)


def _dispatch(pstart, pcount, h2, slot_flat, n_slots):
    t = h2.shape[0]
    tm = ROW_TILE
    gs = pltpu.PrefetchScalarGridSpec(
        num_scalar_prefetch=2,
        grid=(t // tm,),
        in_specs=[pl.BlockSpec((tm, D_MODEL), lambda i, a, b: (i, 0)),
                  pl.BlockSpec((tm * TOP_K,), lambda i, a, b: (i,), memory_space=pltpu.SMEM)],
        out_specs=pl.BlockSpec(memory_space=pl.ANY),
        scratch_shapes=[pltpu.SemaphoreType.DMA(()), pltpu.SemaphoreType.DMA(())],
    )
    return pl.pallas_call(
        _dispatch_kernel,
        grid_spec=gs,
        out_shape=jax.ShapeDtypeStruct((n_slots, D_MODEL), F32),
        compiler_params=_cparams(("arbitrary",), row_dma=True),
        name="dispatch",
    )(pstart, pcount, h2, slot_flat)


def _expert_kernel(bexp_ref, nused_ref, xb_ref, wgu_ref, bgu_ref, wd_ref, bd_ref, perm_ref,
                   yb_ref, wgu_s, wd_s):
    b = pl.program_id(0)
    active = b < nused_ref[0]
    fresh = jnp.logical_or(b == 0, bexp_ref[b] != bexp_ref[jnp.maximum(b - 1, 0)])

    @pl.when(jnp.logical_and(active, fresh))
    def _():
        wgu_s[...] = wgu_ref[...].astype(BF16)
        wd_s[...] = _dot(perm_ref[...], wd_ref[...].astype(BF16)).astype(BF16)

    @pl.when(jnp.logical_not(active))
    def _():
        yb_ref[...] = jnp.zeros_like(yb_ref)

    @pl.when(active)
    def _():
        blk = xb_ref.shape[0]
        gu = _dot(xb_ref[...].astype(BF16), wgu_s[...]) + bgu_ref[...]
        lane = lax.broadcasted_iota(jnp.int32, (blk, LANES), 1)
        even = (lane % 2) == 0
        prods = []
        for c in range(2 * D_FF // LANES):
            guc = gu[:, c * LANES:(c + 1) * LANES]
            glu = jnp.minimum(guc, SWIGLU_LIMIT)
            lin = jnp.clip(guc, -SWIGLU_LIMIT, SWIGLU_LIMIT) + 1.0
            prods.append(glu * jax.nn.sigmoid(SWIGLU_ALPHA * glu)
                         * pltpu.roll(lin, LANES - 1, axis=1))
        merged = [jnp.where(even, prods[2 * m], pltpu.roll(prods[2 * m + 1], 1, axis=1))
                  for m in range(D_FF // LANES)]
        act = jnp.concatenate(merged, axis=1).astype(BF16)
        yb_ref[...] = _dot(act, wd_s[...]) + bd_ref[...]


def _experts(bexp, nused, xb, w_gate_up, b_gate_up, w_down, b_down, layer, n_blocks):
    blk = MOE_BLOCK

    def bmap(b, be, nu):
        return (jnp.minimum(b, nu[0] - 1), 0)

    def emap(b, be, nu):
        return (layer, be[jnp.minimum(b, nu[0] - 1)], 0, 0)

    gs = pltpu.PrefetchScalarGridSpec(
        num_scalar_prefetch=2,
        grid=(n_blocks,),
        in_specs=[pl.BlockSpec((blk, D_MODEL), bmap),
                  pl.BlockSpec((None, None, D_MODEL, 2 * D_FF), emap),
                  pl.BlockSpec((None, None, 1, 2 * D_FF), emap),
                  pl.BlockSpec((None, None, D_FF, D_MODEL), emap),
                  pl.BlockSpec((None, None, 1, D_MODEL), emap),
                  pl.BlockSpec((D_FF, D_FF), lambda b, be, nu: (0, 0))],
        out_specs=pl.BlockSpec((blk, D_MODEL), lambda b, be, nu: (b, 0)),
        scratch_shapes=[pltpu.VMEM((D_MODEL, 2 * D_FF), BF16),
                        pltpu.VMEM((D_FF, D_MODEL), BF16)],
    )
    lane = np.arange(D_FF) % LANES
    unit = (np.arange(D_FF) // LANES) * LANES + lane // 2 + (lane % 2) * (LANES // 2)
    perm = np.zeros((D_FF, D_FF), np.float32)
    perm[np.arange(D_FF), unit] = 1.0
    return pl.pallas_call(
        _expert_kernel,
        grid_spec=gs,
        out_shape=jax.ShapeDtypeStruct(xb.shape, F32),
        compiler_params=_cparams(("arbitrary",)),
        name="experts",
    )(bexp, nused, xb, w_gate_up, b_gate_up[:, :, None, :], w_down, b_down[:, :, None, :],
      jnp.asarray(perm, BF16))


def _combine_kernel(x1_ref, slot_ref, slot_next_ref, gate_ref, yb_ref, g_ref, lng, lnb,
                    o_ref, gbuf, sems):
    tm = x1_ref.shape[0]
    i = pl.program_id(0)
    cur = i % 2

    def row_copy(src_row, buf, k, t):
        return pltpu.make_async_copy(yb_ref.at[pl.ds(src_row, 1)],
                                     gbuf.at[buf, k, pl.ds(t, 1)], sems.at[buf])

    def gather(slots, buf):
        def per_token(g, carry):
            for u in range(2):
                t = g * 2 + u
                for k in range(TOP_K):
                    row_copy(slots[t * TOP_K + k], buf, k, t).start()
            return carry
        lax.fori_loop(0, tm // 2, per_token, 0)

    @pl.when(i == 0)
    def _():
        gather(slot_ref, 0)

    @pl.when(i + 1 < pl.num_programs(0))
    def _():
        gather(slot_next_ref, 1 - cur)

    for k in range(TOP_K):
        pltpu.make_async_copy(yb_ref.at[pl.ds(0, tm)], gbuf.at[cur, k], sems.at[cur]).wait()

    def norm_rows(ci, carry):
        rows = pl.ds(pl.multiple_of(ci * NORM_ROWS, NORM_ROWS), NORM_ROWS)
        gate = gate_ref[rows, :]
        f = gate[:, 0:1] * gbuf[cur, 0, rows, :]
        for k in range(1, TOP_K):
            f = f + gate[:, k:k + 1] * gbuf[cur, k, rows, :]
        r = ALPHA * x1_ref[rows, :] + g_ref[0] * f
        o_ref[rows, :] = _ln_rows(r) * lng[...] + lnb[...]
        return carry

    lax.fori_loop(0, tm // NORM_ROWS, norm_rows, 0)


def _combine(x1, slot_flat, gates, yb, gate_mod, lng, lnb, n_lat_tiles):
    t = x1.shape[0]
    tm = ROW_TILE
    n_tiles = t // tm
    return pl.pallas_call(
        _combine_kernel,
        grid=(n_tiles,),
        in_specs=[pl.BlockSpec((tm, D_MODEL), lambda i: (i, 0)),
                  pl.BlockSpec((tm * TOP_K,), lambda i: (i,), memory_space=pltpu.SMEM),
                  pl.BlockSpec((tm * TOP_K,), lambda i: (jnp.minimum(i + 1, n_tiles - 1),),
                               memory_space=pltpu.SMEM),
                  pl.BlockSpec((tm, LANES), lambda i: (i, 0)),
                  pl.BlockSpec(memory_space=pl.ANY),
                  pl.BlockSpec((1, 1, D_MODEL),
                               lambda i: (jnp.where(i >= n_lat_tiles, 1, 0), 0, 0)),
                  pl.BlockSpec((1, D_MODEL), lambda i: (0, 0)),
                  pl.BlockSpec((1, D_MODEL), lambda i: (0, 0))],
        out_specs=pl.BlockSpec((tm, D_MODEL), lambda i: (i, 0)),
        out_shape=jax.ShapeDtypeStruct((t, D_MODEL), F32),
        scratch_shapes=[pltpu.VMEM((2, TOP_K, tm, D_MODEL), F32),
                        pltpu.SemaphoreType.DMA((2,))],
        compiler_params=_cparams(("arbitrary",), row_dma=True),
        name="combine",
    )(x1, slot_flat, slot_flat, gates, yb, gate_mod, lng, lnb)


def _rope_tables(s_len, cl):
    t = jnp.arange(s_len, dtype=jnp.int32)
    quarter = HEAD_DIM // 4
    inv = ROPE_BASE ** (-jnp.arange(quarter, dtype=F32) / quarter)
    ang_r = (t // GRID_W).astype(F32)[:, None] * inv
    ang_c = (t % GRID_W).astype(F32)[:, None] * inv
    cr, sr, cc, sc = jnp.cos(ang_r), jnp.sin(ang_r), jnp.cos(ang_c), jnp.sin(ang_c)
    cos_t = jnp.concatenate([cr, cr, cc, cc], axis=1)
    sin_t = jnp.concatenate([-sr, sr, -sc, sc], axis=1)
    cos_t = jnp.concatenate([cos_t, jnp.ones((cl, HEAD_DIM), F32)], axis=0)
    sin_t = jnp.concatenate([sin_t, jnp.zeros((cl, HEAD_DIM), F32)], axis=0)
    return cos_t, sin_t


def _block_plan(counts, eid, rank, n_blocks):
    blocks_per = (counts + MOE_BLOCK - 1) // MOE_BLOCK
    block_end = jnp.cumsum(blocks_per)
    base = (block_end - blocks_per) * MOE_BLOCK
    bexp = jnp.minimum(jnp.sum(block_end[None, :] <= jnp.arange(n_blocks)[:, None], axis=1),
                       N_EXPERTS - 1).astype(jnp.int32)
    nused = block_end[-1:].astype(jnp.int32)
    experts = jnp.arange(N_EXPERTS, dtype=jnp.int32)
    slot = rank + jnp.sum(jnp.where(eid[:, :, None] == experts, base, 0), axis=-1)
    pad = blocks_per * MOE_BLOCK - counts
    pcount = jnp.concatenate([pad, jnp.sum(pad)[None], nused]).astype(jnp.int32)
    pstart = (base + counts).astype(jnp.int32)
    return bexp, nused, slot.reshape(-1).astype(jnp.int32), pstart, pcount


def kernel(x, c, ctx, c_ctx, w_ada, b_ada, w_in, w_out, sg_w, sg_b, na_rpb, ln1_g, ln1_b,
           ln2_g, ln2_b, w_router, b_router, w_gate_up, b_gate_up, w_down, b_down):
    bsz, s_len, dm = x.shape
    cl = ctx.shape[1]
    depth = w_ada.shape[0]
    assert bsz == 1 and dm == D_MODEL and cl == ROW_TILE and MOE_BLOCK == ROW_TILE
    assert s_len % (GRID_W * KEY_ROWS) == 0 and s_len % ROW_TILE == 0
    t_all = s_len + cl
    n_lat_tiles = s_len // ROW_TILE
    rows = s_len // GRID_W

    cond = jnp.zeros((8, dm), F32).at[0].set(c[0]).at[1].set(c_ctx)
    mod = _ada_mod(cond, w_ada, b_ada)[:, :2].reshape(depth, 2, 6, 1, dm)
    cos_t, sin_t = _rope_tables(s_len, cl)
    pb, vrow = _na_bias_tables(na_rpb, rows)
    cs, m1, tw, mc = _fourier_consts(s_len, cl)
    tri = jnp.asarray(np.tril(np.ones((ROW_TILE, ROW_TILE), np.float32), -1), BF16)
    wr_pad = jnp.pad(w_router, ((0, 0), (0, 0), (0, LANES - N_EXPERTS)))
    wr_hi = wr_pad.astype(BF16)
    wr_split = jnp.stack([wr_hi, (wr_pad - wr_hi.astype(F32)).astype(BF16)], axis=1)
    br_pad = jnp.pad(b_router, ((0, 0), (0, LANES - N_EXPERTS)), constant_values=NEG_INF)
    w_in_bf = w_in.astype(BF16)
    w_out_bf = w_out.astype(BF16)
    sgw_bf = sg_w.astype(BF16)

    xall = jnp.concatenate([x[0], ctx[0]], axis=0)
    for l in range(depth):
        last = l == depth - 1
        m = lambda j: mod[l, :, j]
        qp, qr, kr, v, z_sg, z_ft = _proj(xall, m(0), m(1), w_in_bf, cos_t, sin_t, l,
                                          n_lat_tiles)
        ona_l = _na_attention(qr, qp, kr, v, pb, vrow, l, s_len)
        osg = _spatial_gating(z_sg, sgw_bf[l], sg_b[l][:, :, None])
        oft_l = _fourier_latent(z_ft, cs, m1, tw, s_len)
        if last:
            ona_c = oft_c = None
            n_rows = s_len
        else:
            ona_c = _ctx_attention(qp, kr, v, s_len, cl)
            oft_c = _fourier_ctx(z_ft, cs, mc, s_len, cl)
            n_rows = t_all
        x1, h2, logits = _out_proj(
            ona_l, ona_c, osg, oft_l, oft_c, xall, w_out_bf, m(2),
            ln1_g[l][None], ln1_b[l][None], m(3), m(4), wr_split, br_pad[l][None],
            l, n_rows, n_lat_tiles, not last)
        eid, rank, gates, counts = _route(logits, tri)
        n_blocks = -(-n_rows * TOP_K // MOE_BLOCK) + N_EXPERTS
        bexp, nused, slot_flat, pstart, pcount = _block_plan(
            counts[0, :N_EXPERTS], eid[:, :TOP_K], rank[:, :TOP_K], n_blocks)
        xb = _dispatch(pstart, pcount, h2, slot_flat, n_blocks * MOE_BLOCK)
        yb = _experts(bexp, nused, xb, w_gate_up, b_gate_up, w_down, b_down, l, n_blocks)
        xall = _combine(x1, slot_flat, gates, yb, m(5), ln2_g[l][None], ln2_b[l][None],
                        n_lat_tiles)
    return xall[None]
```

```python
import functools
import math

import numpy as np
import jax
import jax.numpy as jnp
from jax import lax
from jax.experimental import pallas as pl
from jax.experimental.pallas import tpu as pltpu

D_MODEL = 2048
DEPTH_NORM = 4
GRID_W = 64
HEAD_DIM = 128
NA_HEADS = 8
NA_WIDTH = NA_HEADS * HEAD_DIM
NA_KH = 8
NA_KW = 16
SG_GROUPS = 4
SG_WIDTH = 512
SG_CHUNK = 128
FT_GROUPS = 4
FT_WIDTH = 512
SG_OFF = 3 * NA_WIDTH
FT_OFF = SG_OFF + 2 * SG_WIDTH
IN_WIDTH = FT_OFF + FT_WIDTH
ROPE_BASE = 10000.0
N_EXPERTS = 32
TOP_K = 4
D_FF = D_MODEL // 4
SWIGLU_ALPHA = 1.702
SWIGLU_LIMIT = 7.0
LN_EPS = 1e-5
NEG_INF = -1e30
ALPHA = (2 * DEPTH_NORM) ** 0.25

LANES = 128
ROW_TILE = 256
MOE_BLOCK = 256
NORM_ROWS = 32
KEY_ROWS = 16
Q_ROWS = 8
NA_PAIRS = 5
VMEM_LIMIT = 56 * 1024 * 1024

F32 = jnp.float32
BF16 = jnp.bfloat16


def _cparams(sem, row_dma=False):
    return pltpu.CompilerParams(dimension_semantics=sem, vmem_limit_bytes=VMEM_LIMIT,
                                disable_bounds_checks=row_dma)


def _dot(a, b):
    return jnp.dot(a, b, preferred_element_type=F32)


def _dot_nt(a, b):
    return lax.dot_general(a, b, (((1,), (1,)), ((), ())), preferred_element_type=F32)


def _ln_rows(x):
    mu = jnp.mean(x, axis=-1, keepdims=True)
    xc = x - mu
    var = jnp.mean(xc * xc, axis=-1, keepdims=True)
    return xc * lax.rsqrt(var + LN_EPS)


def _ada_kernel(c_ref, w_ref, b_ref, o_ref):
    c = c_ref[...]
    s = c * jax.nn.sigmoid(c)
    o_ref[0] = jnp.dot(s, w_ref[0], preferred_element_type=F32,
                       precision=lax.Precision.HIGHEST) + b_ref[0]


def _ada_mod(cond, w_ada, b_ada):
    depth, d, n = w_ada.shape
    tn = 1536
    return pl.pallas_call(
        _ada_kernel,
        grid=(depth, n // tn),
        in_specs=[pl.BlockSpec((8, d), lambda l, j: (0, 0)),
                  pl.BlockSpec((1, d, tn), lambda l, j: (l, 0, j)),
                  pl.BlockSpec((1, 1, tn), lambda l, j: (l, 0, j))],
        out_specs=pl.BlockSpec((1, 8, tn), lambda l, j: (l, 0, j)),
        out_shape=jax.ShapeDtypeStruct((depth, 8, n), F32),
        compiler_params=_cparams(("arbitrary", "arbitrary")),
        name="ada_mod",
    )(cond, w_ada, b_ada.reshape(depth, 1, n))


def _proj_kernel(x_ref, sh_ref, sc_ref, w_ref, cos_ref, sin_ref,
                 qp_ref, qr_ref, kr_ref, v_ref, sg_ref, ft_ref):
    y = _ln_rows(x_ref[...])
    h = (y * (1.0 + sc_ref[0]) + sh_ref[0]).astype(BF16)
    cos = cos_ref[...]
    sin = sin_ref[...]
    lane = lax.broadcasted_iota(jnp.int32, cos.shape, 1)
    first = (lane % 64) < 32

    def rope(z):
        swapped = jnp.where(first, pltpu.roll(z, 96, axis=1), pltpu.roll(z, 32, axis=1))
        return z * cos + swapped * sin

    nw = 512
    for j in range(IN_WIDTH // nw):
        z = _dot(h, w_ref[:, j * nw:(j + 1) * nw])
        for p in range(nw // LANES):
            col = j * nw + p * LANES
            zp = z[:, p * LANES:(p + 1) * LANES]
            if col < NA_WIDTH:
                zp = zp * (HEAD_DIM ** -0.5)
                qp_ref[:, col:col + LANES] = zp.astype(BF16)
                qr_ref[:, col:col + LANES] = rope(zp).astype(BF16)
            elif col < 2 * NA_WIDTH:
                c0 = col - NA_WIDTH
                kr_ref[:, c0:c0 + LANES] = rope(zp).astype(BF16)
            elif col < SG_OFF:
                c0 = col - 2 * NA_WIDTH
                v_ref[:, c0:c0 + LANES] = zp.astype(BF16)
            elif col < FT_OFF:
                c0 = col - SG_OFF
                sg_ref[:, c0:c0 + LANES] = zp
            else:
                c0 = col - FT_OFF
                ft_ref[:, c0:c0 + LANES] = zp.astype(BF16)


def _proj(xall, shift, scale, w_in_bf, cos_t, sin_t, layer, n_lat_tiles):
    t = xall.shape[0]
    tm = ROW_TILE
    typ = lambda i: (jnp.where(i >= n_lat_tiles, 1, 0), 0, 0)
    row = lambda i: (i, 0)
    return pl.pallas_call(
        _proj_kernel,
        grid=(t // tm,),
        in_specs=[pl.BlockSpec((tm, D_MODEL), row),
                  pl.BlockSpec((1, 1, D_MODEL), typ),
                  pl.BlockSpec((1, 1, D_MODEL), typ),
                  pl.BlockSpec((None, D_MODEL, IN_WIDTH), lambda i: (layer, 0, 0),
                               pipeline_mode=pl.Buffered(1)),
                  pl.BlockSpec((tm, LANES), row),
                  pl.BlockSpec((tm, LANES), row)],
        out_specs=[pl.BlockSpec((tm, NA_WIDTH), row)] * 4
        + [pl.BlockSpec((tm, 2 * SG_WIDTH), row), pl.BlockSpec((tm, FT_WIDTH), row)],
        out_shape=[jax.ShapeDtypeStruct((t, NA_WIDTH), BF16)] * 4
        + [jax.ShapeDtypeStruct((t, 2 * SG_WIDTH), F32),
           jax.ShapeDtypeStruct((t, FT_WIDTH), BF16)],
        compiler_params=_cparams(("arbitrary",)),
        name="ln_proj",
    )(xall, shift, scale, w_in_bf, cos_t, sin_t)


def _na_kernel(qr_ref, qp_ref, k0, k1, k2, k3, v0, v1, v2, v3, kc_ref, vc_ref, pb_ref, vrow_ref,
               o_ref, k_scr, v_scr, s_scr, p_scr, den_scr, *, n_rb):
    i = pl.program_id(1)
    is_first = i == 0
    is_last = i == n_rb - 1
    off = jnp.where(is_first, 0, jnp.where(is_last, -(KEY_ROWS - Q_ROWS), -(NA_KH // 2)))
    kb = k0.shape[0]
    for j, (kj, vj) in enumerate(((k0, v0), (k1, v1), (k2, v2), (k3, v3))):
        k_scr[j * kb:(j + 1) * kb, :] = kj[...]
        v_scr[j * kb:(j + 1) * kb, :] = vj[...]
    kc = kc_ref[...]
    vc = vc_ref[...]
    qsub = 128
    pair = 2 * GRID_W
    starts = ((0, 0, 0, 1), (0, 1, 2, 3), (2, 3, 3, 3))
    n_sub = Q_ROWS * GRID_W // qsub
    n_lat = NA_PAIRS * pair
    key0 = []
    for s in range(n_sub):
        rows = slice(s * qsub, (s + 1) * qsub)
        p0 = jnp.where(is_first, starts[0][s], jnp.where(is_last, starts[2][s], starts[1][s]))
        key0.append(pl.multiple_of(p0 * pair, pair))
        bias_rows = []
        for qr in range(s * qsub // GRID_W, (s + 1) * qsub // GRID_W):
            tiles = [pb_ref[0, 0, jnp.clip(2 * (p0 + j) - qr + off + NA_KH, 0, 2 * NA_KH - 1)]
                     + vrow_ref[0, qr, p0 + j]
                     for j in range(NA_PAIRS)]
            bias_rows.append(jnp.concatenate(tiles, axis=1))
        bias = jnp.concatenate(bias_rows, axis=0)
        s_scr[s, :, 0:n_lat] = _dot_nt(qr_ref[rows, :], k_scr[pl.ds(key0[s], n_lat), :]) + bias
        s_scr[s, :, n_lat:] = _dot_nt(qp_ref[rows, :], kc)
    for s in range(n_sub):
        sc = s_scr[s]
        p = jnp.exp(sc - jnp.max(sc, axis=-1, keepdims=True))
        p_scr[s] = p.astype(BF16)
        den_scr[s] = jnp.sum(p, axis=-1, keepdims=True)
    for s in range(n_sub):
        rows = slice(s * qsub, (s + 1) * qsub)
        o = (_dot(p_scr[s, :, 0:n_lat], v_scr[pl.ds(key0[s], n_lat), :])
             + _dot(p_scr[s, :, n_lat:], vc))
        o_ref[rows, :] = (o / den_scr[s]).astype(BF16)


def _na_attention(qr, qp, kr, v, pb, vrow, layer, s_len):
    rows = s_len // GRID_W
    n_rb = rows // Q_ROWS
    qb = Q_ROWS * GRID_W
    kb = 256
    last_kblock = (rows - KEY_ROWS) * GRID_W // kb
    ctx_block = s_len // kb

    def kmap(j):
        return lambda h, i: (jnp.clip(2 * i - 1, 0, last_kblock) + j, h)

    def btype(i):
        return jnp.where(i == 0, 0, jnp.where(i == n_rb - 1, 2, 1))

    qspec = pl.BlockSpec((qb, HEAD_DIM), lambda h, i: (i, h))
    kspecs = [pl.BlockSpec((kb, HEAD_DIM), kmap(j)) for j in range(4)]
    cspec = pl.BlockSpec((kb, HEAD_DIM), lambda h, i: (ctx_block, h))
    return pl.pallas_call(
        functools.partial(_na_kernel, n_rb=n_rb),
        grid=(NA_HEADS, n_rb),
        in_specs=[qspec, qspec] + kspecs + kspecs + [cspec, cspec]
        + [pl.BlockSpec((1, 1, 2 * NA_KH, GRID_W, 2 * GRID_W), lambda h, i: (layer, h, 0, 0, 0)),
           pl.BlockSpec((1, Q_ROWS, KEY_ROWS // 2, 1, 2 * GRID_W),
                        lambda h, i: (btype(i), 0, 0, 0, 0))],
        out_specs=pl.BlockSpec((qb, HEAD_DIM), lambda h, i: (i, h)),
        out_shape=jax.ShapeDtypeStruct((s_len, NA_WIDTH), BF16),
        scratch_shapes=[pltpu.VMEM((KEY_ROWS * GRID_W, HEAD_DIM), BF16),
                        pltpu.VMEM((KEY_ROWS * GRID_W, HEAD_DIM), BF16),
                        pltpu.VMEM((qb // 128, 128, NA_PAIRS * 2 * GRID_W + kb), F32),
                        pltpu.VMEM((qb // 128, 128, NA_PAIRS * 2 * GRID_W + kb), BF16),
                        pltpu.VMEM((qb // 128, 128, 1), F32)],
        compiler_params=_cparams(("arbitrary", "arbitrary")),
        name="na_attention",
    )(qr, qp, kr, kr, kr, kr, v, v, v, v, kr, v, pb, vrow)


def _ctx_attn_kernel(q_ref, k_ref, v_ref, o_ref):
    s = _dot_nt(q_ref[...], k_ref[...])
    m = jnp.max(s, axis=-1, keepdims=True)
    p = jnp.exp(s - m)
    den = jnp.sum(p, axis=-1, keepdims=True)
    o_ref[...] = (_dot(p.astype(BF16), v_ref[...]) / den).astype(BF16)


def _ctx_attention(qp, kr, v, s_len, cl):
    blk = s_len // cl
    spec = pl.BlockSpec((cl, HEAD_DIM), lambda h: (blk, h))
    return pl.pallas_call(
        _ctx_attn_kernel,
        grid=(NA_HEADS,),
        in_specs=[spec, spec, spec],
        out_specs=pl.BlockSpec((cl, HEAD_DIM), lambda h: (0, h)),
        out_shape=jax.ShapeDtypeStruct((cl, NA_WIDTH), BF16),
        compiler_params=_cparams(("arbitrary",)),
        name="ctx_attention",
    )(qp, kr, v)


def _na_bias_tables(na_rpb, rows):
    n_rb = rows // Q_ROWS
    rmask = np.zeros((3, Q_ROWS, KEY_ROWS), bool)
    for t, i in enumerate((0, 1, n_rb - 1)):
        ks = min(max(Q_ROWS * i - NA_KH // 2, 0), rows - KEY_ROWS)
        for qr in range(Q_ROWS):
            r = Q_ROWS * i + qr
            lo = min(max(r - NA_KH // 2, 0), rows - NA_KH)
            for kr in range(KEY_ROWS):
                rmask[t, qr, kr] = lo <= ks + kr < lo + NA_KH
    vrow = np.where(np.repeat(rmask, GRID_W, axis=2), 0.0, NEG_INF).astype(np.float32)
    csel = np.zeros((GRID_W, GRID_W, 2 * NA_KW - 1), np.float32)
    cmask = np.zeros((GRID_W, GRID_W), bool)
    for qc in range(GRID_W):
        lo = min(max(qc - NA_KW // 2, 0), GRID_W - NA_KW)
        for kc in range(GRID_W):
            cmask[qc, kc] = lo <= kc < lo + NA_KW
            csel[qc, kc, min(max(kc - qc + NA_KW - 1, 0), 2 * NA_KW - 2)] = 1.0
    b = jnp.einsum('lhab,qkb->lhaqk', na_rpb, jnp.asarray(csel), precision=lax.Precision.HIGHEST)
    b = jnp.where(jnp.asarray(cmask), b, NEG_INF)
    b = jnp.pad(b, ((0, 0), (0, 0), (1, 1), (0, 0), (0, 0)))
    pb = jnp.concatenate([b[:, :, :-1], b[:, :, 1:]], axis=-1)
    return pb, jnp.asarray(vrow).reshape(3, Q_ROWS, KEY_ROWS // 2, 1, 2 * GRID_W)


def _sg_kernel(z_ref, w_ref, b_ref, o_ref):
    z = z_ref[...]
    g = 0.5 * z * (1.0 + lax.erf(z * (2.0 ** -0.5)))
    n_chunks = z.shape[0] // SG_CHUNK
    for gi in range(SG_GROUPS):
        u = g[:, gi * LANES:(gi + 1) * LANES]
        vn = _ln_rows(g[:, SG_WIDTH + gi * LANES:SG_WIDTH + (gi + 1) * LANES]).astype(BF16)
        for n in range(n_chunks):
            rows = slice(n * SG_CHUNK, (n + 1) * SG_CHUNK)
            t = _dot(w_ref[gi], vn[rows, :]) + b_ref[gi]
            o_ref[rows, gi * LANES:(gi + 1) * LANES] = (u[rows, :] * t).astype(BF16)


def _spatial_gating(z_sg, sgw_bf, sgb):
    t = z_sg.shape[0]
    tm = ROW_TILE
    return pl.pallas_call(
        _sg_kernel,
        grid=(t // tm,),
        in_specs=[pl.BlockSpec((tm, 2 * SG_WIDTH), lambda i: (i, 0)),
                  pl.BlockSpec((SG_GROUPS, SG_CHUNK, SG_CHUNK), lambda i: (0, 0, 0)),
                  pl.BlockSpec((SG_GROUPS, SG_CHUNK, 1), lambda i: (0, 0, 0))],
        out_specs=pl.BlockSpec((tm, SG_WIDTH), lambda i: (i, 0)),
        out_shape=jax.ShapeDtypeStruct((t, SG_WIDTH), BF16),
        compiler_params=_cparams(("arbitrary",)),
        name="spatial_gating",
    )(z_sg, sgw_bf, sgb)


def _ft1_kernel(z_ref, cs_ref, m1_ref, y_ref):
    n1 = z_ref.shape[0]
    parts = []
    for j in range(z_ref.shape[1] // LANES):
        ab = _dot(z_ref[:, j * LANES:(j + 1) * LANES], cs_ref[...])
        parts.append(jnp.concatenate([ab[:, :LANES], ab[:, LANES:]], axis=0))
    st = jnp.concatenate(parts, axis=1).astype(BF16)
    y = _dot(m1_ref[...], st)
    y_ref[0] = y[:n1].astype(BF16)
    y_ref[1] = y[n1:].astype(BF16)


def _ft2_kernel(y_ref, tw_ref, o_ref, *, scale):
    for j in range(y_ref.shape[1]):
        yy = jnp.concatenate([y_ref[0, j], y_ref[1, j]], axis=0)
        o = _dot(tw_ref[j], yy) * scale
        o_ref[:, j * FT_WIDTH:(j + 1) * FT_WIDTH] = o.astype(BF16)


def _fourier_consts(s_len, cl):
    n2 = 128
    n1 = s_len // n2
    c = np.arange(HEAD_DIM)
    ang = 2 * np.pi * ((c[:, None] * c[None, :]) % HEAD_DIM) / HEAD_DIM
    cs = np.concatenate([np.cos(ang), np.sin(ang)], axis=1)
    a = np.arange(n1)
    ang1 = 2 * np.pi * ((a[:, None] * a[None, :]) % n1) / n1
    fc, fs = np.cos(ang1), np.sin(ang1)
    m1 = np.block([[fc, -fs], [-fs, -fc]])
    ka = jnp.arange(n1, dtype=jnp.int32)[:, None, None]
    kb = jnp.arange(n2, dtype=jnp.int32)[None, :, None]
    nn = jnp.arange(n2, dtype=jnp.int32)[None, None, :]
    ph = (nn * (ka + n1 * kb)) % s_len
    th = ph.astype(F32) * (2 * np.pi / s_len)
    tw = jnp.concatenate([jnp.cos(th), jnp.sin(th)], axis=2).astype(BF16)
    p = np.arange(cl)
    angc = 2 * np.pi * ((p[:, None] * p[None, :]) % cl) / cl
    mc = np.concatenate([np.cos(angc), -np.sin(angc)], axis=1)
    return (jnp.asarray(cs, BF16), jnp.asarray(m1, BF16), tw, jnp.asarray(mc, BF16))


def _fourier_latent(z_ft, cs, m1, tw, s_len):
    n2 = 128
    n1 = s_len // n2
    width = n2 * FT_WIDTH
    cw = 2048
    y = pl.pallas_call(
        _ft1_kernel,
        grid=(width // cw,),
        in_specs=[pl.BlockSpec((n1, cw), lambda j: (0, j)),
                  pl.BlockSpec((HEAD_DIM, 2 * HEAD_DIM), lambda j: (0, 0)),
                  pl.BlockSpec((2 * n1, 2 * n1), lambda j: (0, 0))],
        out_specs=pl.BlockSpec((2, n1, cw), lambda j: (0, 0, j)),
        out_shape=jax.ShapeDtypeStruct((2, n1, width), BF16),
        compiler_params=_cparams(("arbitrary",)),
        name="fourier_stage1",
    )(z_ft[:s_len].reshape(n1, width), cs, m1)
    kab = 4
    out = pl.pallas_call(
        functools.partial(_ft2_kernel, scale=(s_len * HEAD_DIM) ** -0.5),
        grid=(n1 // kab,),
        in_specs=[pl.BlockSpec((2, kab, n2, FT_WIDTH), lambda j: (0, j, 0, 0)),
                  pl.BlockSpec((kab, n2, 2 * n2), lambda j: (j, 0, 0))],
        out_specs=pl.BlockSpec((n2, kab * FT_WIDTH), lambda j: (0, j)),
        out_shape=jax.ShapeDtypeStruct((n2, n1 * FT_WIDTH), BF16),
        compiler_params=_cparams(("arbitrary",)),
        name="fourier_stage2",
    )(y.reshape(2, n1, n2, FT_WIDTH), tw)
    return out.reshape(s_len, FT_WIDTH)


def _ft_ctx_kernel(z_ref, cs_ref, mc_ref, o_ref, *, scale):
    for g in range(FT_GROUPS):
        ab = _dot(z_ref[:, g * LANES:(g + 1) * LANES], cs_ref[...])
        st = jnp.concatenate([ab[:, :LANES], ab[:, LANES:]], axis=0).astype(BF16)
        o_ref[:, g * LANES:(g + 1) * LANES] = (_dot(mc_ref[...], st) * scale).astype(BF16)


def _fourier_ctx(z_ft, cs, mc, s_len, cl):
    blk = s_len // cl
    return pl.pallas_call(
        functools.partial(_ft_ctx_kernel, scale=(cl * HEAD_DIM) ** -0.5),
        grid=(1,),
        in_specs=[pl.BlockSpec((cl, FT_WIDTH), lambda i: (blk, 0)),
                  pl.BlockSpec((HEAD_DIM, 2 * HEAD_DIM), lambda i: (0, 0)),
                  pl.BlockSpec((cl, 2 * cl), lambda i: (0, 0))],
        out_specs=pl.BlockSpec((cl, FT_WIDTH), lambda i: (0, 0)),
        out_shape=jax.ShapeDtypeStruct((cl, FT_WIDTH), BF16),
        compiler_params=_cparams(("arbitrary",)),
        name="fourier_ctx",
    )(z_ft, cs, mc)


def _out_kernel(*refs, with_ctx, n_lat_tiles):
    if with_ctx:
        (ona_l, ona_c, osg, oft_l, oft_c, x_ref, w_ref, g_ref, lng, lnb, sh_ref, sc_ref,
         wr_ref, br_ref, x1_ref, h2_ref, lg_ref, r_scr, hs_scr) = refs
        is_ctx = pl.program_id(0) >= n_lat_tiles
        ona = jnp.where(is_ctx, ona_c[...], ona_l[...])
        oft = jnp.where(is_ctx, oft_c[...], oft_l[...])
    else:
        (ona_l, osg, oft_l, x_ref, w_ref, g_ref, lng, lnb, sh_ref, sc_ref,
         wr_ref, br_ref, x1_ref, h2_ref, lg_ref, r_scr, hs_scr) = refs
        ona = ona_l[...]
        oft = oft_l[...]
    sg = osg[...]
    nw = 512
    for n in range(D_MODEL // nw):
        cols = slice(n * nw, (n + 1) * nw)
        y = (_dot(ona, w_ref[0:NA_WIDTH, cols])
             + _dot(sg, w_ref[NA_WIDTH:NA_WIDTH + SG_WIDTH, cols])
             + _dot(oft, w_ref[NA_WIDTH + SG_WIDTH:, cols]))
        r_scr[:, cols] = ALPHA * x_ref[:, cols] + g_ref[0][:, cols] * y

    def norm_rows(ci, carry):
        rows = pl.ds(pl.multiple_of(ci * NORM_ROWS, NORM_ROWS), NORM_ROWS)
        x1 = _ln_rows(r_scr[rows, :]) * lng[...] + lnb[...]
        x1_ref[rows, :] = x1
        h2 = _ln_rows(x1) * (1.0 + sc_ref[0]) + sh_ref[0]
        h2_ref[rows, :] = h2
        hi = h2.astype(BF16)
        hs_scr[0, rows, :] = hi
        hs_scr[1, rows, :] = (h2 - hi.astype(F32)).astype(BF16)
        return carry

    lax.fori_loop(0, r_scr.shape[0] // NORM_ROWS, norm_rows, 0, unroll=True)
    lg_ref[...] = (_dot(hs_scr[0], wr_ref[0]) + _dot(hs_scr[1], wr_ref[0])
                   + _dot(hs_scr[0], wr_ref[1]) + br_ref[...])


def _out_proj(ona_l, ona_c, osg, oft_l, oft_c, xall, w_out_bf, gate, lng, lnb, shift, scale,
              wr_split, br_pad, layer, n_rows, n_lat_tiles, with_ctx):
    tm = ROW_TILE
    row = lambda i: (i, 0)
    lat = lambda i: (jnp.minimum(i, n_lat_tiles - 1), 0)
    typ = lambda i: (jnp.where(i >= n_lat_tiles, 1, 0), 0, 0)
    const = lambda i: (0, 0)
    modspec = pl.BlockSpec((1, 1, D_MODEL), typ)
    vec = pl.BlockSpec((1, D_MODEL), const)
    specs = [pl.BlockSpec((tm, NA_WIDTH), lat)]
    args = [ona_l]
    if with_ctx:
        specs.append(pl.BlockSpec((tm, NA_WIDTH), const))
        args.append(ona_c)
    specs.append(pl.BlockSpec((tm, SG_WIDTH), row))
    args.append(osg)
    specs.append(pl.BlockSpec((tm, FT_WIDTH), lat))
    args.append(oft_l)
    if with_ctx:
        specs.append(pl.BlockSpec((tm, FT_WIDTH), const))
        args.append(oft_c)
    specs += [pl.BlockSpec((tm, D_MODEL), row),
              pl.BlockSpec((None, D_MODEL, D_MODEL), lambda i: (layer, 0, 0),
                           pipeline_mode=pl.Buffered(1)),
              modspec, vec, vec, modspec, modspec,
              pl.BlockSpec((None, 2, D_MODEL, LANES), lambda i: (layer, 0, 0, 0)),
              pl.BlockSpec((1, LANES), const)]
    args += [xall, w_out_bf, gate, lng, lnb, shift, scale, wr_split, br_pad]
    return pl.pallas_call(
        functools.partial(_out_kernel, with_ctx=with_ctx, n_lat_tiles=n_lat_tiles),
        grid=(n_rows // tm,),
        in_specs=specs,
        out_specs=[pl.BlockSpec((tm, D_MODEL), row),
                   pl.BlockSpec((tm, D_MODEL), row),
                   pl.BlockSpec((tm, LANES), row)],
        out_shape=[jax.ShapeDtypeStruct((n_rows, D_MODEL), F32),
                   jax.ShapeDtypeStruct((n_rows, D_MODEL), F32),
                   jax.ShapeDtypeStruct((n_rows, LANES), F32)],
        scratch_shapes=[pltpu.VMEM((tm, D_MODEL), F32),
                        pltpu.VMEM((2, tm, D_MODEL), BF16)],
        compiler_params=_cparams(("arbitrary",)),
        name="out_proj",
    )(*args)


def _route_kernel(lg_ref, tri_ref, eid_ref, rank_ref, gate_ref, cnt_ref, carry):
    @pl.when(pl.program_id(0) == 0)
    def _():
        carry[...] = jnp.zeros_like(carry)

    work = lg_ref[...]
    lane = lax.broadcasted_iota(jnp.int32, work.shape, 1)
    lane_f = lane.astype(F32)
    vals, ids, hots = [], [], []
    for _ in range(TOP_K):
        m = jnp.max(work, axis=-1, keepdims=True)
        idx = jnp.min(jnp.where(work == m, lane_f, float(LANES)), axis=-1, keepdims=True)
        hot = lane_f == idx
        vals.append(m)
        ids.append(idx)
        hots.append(hot)
        work = jnp.where(hot, -jnp.inf, work)
    exps = [jnp.exp(v - vals[0]) for v in vals]
    den = exps[0] + exps[1] + exps[2] + exps[3]
    multi = jnp.zeros(work.shape, F32)
    for hot in hots:
        multi = multi + hot.astype(F32)
    pref = _dot(tri_ref[...], multi.astype(BF16)) + carry[...]
    eid = jnp.zeros(work.shape, F32)
    rank = jnp.zeros(work.shape, F32)
    gate = jnp.zeros(work.shape, F32)
    for k in range(TOP_K):
        rk = jnp.sum(jnp.where(hots[k], pref, 0.0), axis=-1, keepdims=True)
        sel = lane == k
        eid = jnp.where(sel, ids[k], eid)
        rank = jnp.where(sel, rk, rank)
        gate = jnp.where(sel, exps[k] / den, gate)
    eid_ref[...] = eid.astype(jnp.int32)
    rank_ref[...] = rank.astype(jnp.int32)
    gate_ref[...] = gate
    carry[...] = carry[...] + jnp.sum(multi, axis=0, keepdims=True)
    cnt_ref[...] = jnp.broadcast_to(carry[...], cnt_ref.shape).astype(jnp.int32)


def _route(logits, tri):
    t = logits.shape[0]
    tm = ROW_TILE
    row = lambda i: (i, 0)
    return pl.pallas_call(
        _route_kernel,
        grid=(t // tm,),
        in_specs=[pl.BlockSpec((tm, LANES), row),
                  pl.BlockSpec((tm, tm), lambda i: (0, 0))],
        out_specs=[pl.BlockSpec((tm, LANES), row)] * 3
        + [pl.BlockSpec((8, LANES), lambda i: (0, 0))],
        out_shape=[jax.ShapeDtypeStruct((t, LANES), jnp.int32)] * 2
        + [jax.ShapeDtypeStruct((t, LANES), F32),
           jax.ShapeDtypeStruct((8, LANES), jnp.int32)],
        scratch_shapes=[pltpu.VMEM((1, LANES), F32)],
        compiler_params=_cparams(("arbitrary",)),
        name="route",
    )(logits, tri)


def _dispatch_kernel(pstart_ref, pcount_ref, h_ref, slot_ref, xb_ref, sem, pad_sem):
    tm = h_ref.shape[0]
    first = pl.program_id(0) == 0

    def row_copy(src_row, dst_row, s):
        return pltpu.make_async_copy(h_ref.at[pl.ds(src_row, 1)], xb_ref.at[pl.ds(dst_row, 1)], s)

    def tail_copy(b):
        return pltpu.make_async_copy(h_ref, xb_ref.at[pl.ds(b * MOE_BLOCK, MOE_BLOCK)], pad_sem)

    n_blocks = xb_ref.shape[0] // MOE_BLOCK
    n_used = pcount_ref[N_EXPERTS + 1]

    @pl.when(first)
    def _():
        def per_expert(e, carry):
            def per_row(j, c2):
                row_copy(0, pstart_ref[e] + j, pad_sem).start()
                return c2
            return lax.fori_loop(0, pcount_ref[e], per_row, carry)
        lax.fori_loop(0, N_EXPERTS, per_expert, 0)

        def per_tail(b, carry):
            tail_copy(b).start()
            return carry
        lax.fori_loop(n_used, n_blocks, per_tail, 0)

    def per_token(g, carry):
        for u in range(2):
            t = g * 2 + u
            for k in range(TOP_K):
                row_copy(t, slot_ref[t * TOP_K + k], sem).start()
        return carry

    lax.fori_loop(0, tm // 2, per_token, 0)
    for _ in range(TOP_K):
        pltpu.make_async_copy(h_ref, xb_ref.at[pl.ds(0, tm)], sem).wait()

    @pl.when(first)
    def _():
        def wait_row(j, carry):
            row_copy(0, 0, pad_sem).wait()
            return carry
        lax.fori_loop(0, pcount_ref[N_EXPERTS], wait_row, 0)

        def wait_tail(b, carry):
            tail_copy(b).wait()
            return carry
        lax.fori_loop(n_used, n_blocks, wait_tail, 0)


def _dispatch(pstart, pcount, h2, slot_flat, n_slots):
    t = h2.shape[0]
    tm = ROW_TILE
    gs = pltpu.PrefetchScalarGridSpec(
        num_scalar_prefetch=2,
        grid=(t // tm,),
        in_specs=[pl.BlockSpec((tm, D_MODEL), lambda i, a, b: (i, 0)),
                  pl.BlockSpec((tm * TOP_K,), lambda i, a, b: (i,), memory_space=pltpu.SMEM)],
        out_specs=pl.BlockSpec(memory_space=pl.ANY),
        scratch_shapes=[pltpu.SemaphoreType.DMA(()), pltpu.SemaphoreType.DMA(())],
    )
    return pl.pallas_call(
        _dispatch_kernel,
        grid_spec=gs,
        out_shape=jax.ShapeDtypeStruct((n_slots, D_MODEL), F32),
        compiler_params=_cparams(("arbitrary",), row_dma=True),
        name="dispatch",
    )(pstart, pcount, h2, slot_flat)


def _expert_kernel(bexp_ref, nused_ref, xb_ref, wgu_ref, bgu_ref, wd_ref, bd_ref, perm_ref,
                   yb_ref, wgu_s, wd_s):
    b = pl.program_id(0)
    active = b < nused_ref[0]
    fresh = jnp.logical_or(b == 0, bexp_ref[b] != bexp_ref[jnp.maximum(b - 1, 0)])

    @pl.when(jnp.logical_and(active, fresh))
    def _():
        wgu_s[...] = wgu_ref[...].astype(BF16)
        wd_s[...] = _dot(perm_ref[...], wd_ref[...].astype(BF16)).astype(BF16)

    @pl.when(jnp.logical_not(active))
    def _():
        yb_ref[...] = jnp.zeros_like(yb_ref)

    @pl.when(active)
    def _():
        blk = xb_ref.shape[0]
        gu = _dot(xb_ref[...].astype(BF16), wgu_s[...]) + bgu_ref[...]
        lane = lax.broadcasted_iota(jnp.int32, (blk, LANES), 1)
        even = (lane % 2) == 0
        prods = []
        for c in range(2 * D_FF // LANES):
            guc = gu[:, c * LANES:(c + 1) * LANES]
            glu = jnp.minimum(guc, SWIGLU_LIMIT)
            lin = jnp.clip(guc, -SWIGLU_LIMIT, SWIGLU_LIMIT) + 1.0
            prods.append(glu * jax.nn.sigmoid(SWIGLU_ALPHA * glu)
                         * pltpu.roll(lin, LANES - 1, axis=1))
        merged = [jnp.where(even, prods[2 * m], pltpu.roll(prods[2 * m + 1], 1, axis=1))
                  for m in range(D_FF // LANES)]
        act = jnp.concatenate(merged, axis=1).astype(BF16)
        yb_ref[...] = _dot(act, wd_s[...]) + bd_ref[...]


def _experts(bexp, nused, xb, w_gate_up, b_gate_up, w_down, b_down, layer, n_blocks):
    blk = MOE_BLOCK

    def bmap(b, be, nu):
        return (jnp.minimum(b, nu[0] - 1), 0)

    def emap(b, be, nu):
        return (layer, be[jnp.minimum(b, nu[0] - 1)], 0, 0)

    gs = pltpu.PrefetchScalarGridSpec(
        num_scalar_prefetch=2,
        grid=(n_blocks,),
        in_specs=[pl.BlockSpec((blk, D_MODEL), bmap),
                  pl.BlockSpec((None, None, D_MODEL, 2 * D_FF), emap),
                  pl.BlockSpec((None, None, 1, 2 * D_FF), emap),
                  pl.BlockSpec((None, None, D_FF, D_MODEL), emap),
                  pl.BlockSpec((None, None, 1, D_MODEL), emap),
                  pl.BlockSpec((D_FF, D_FF), lambda b, be, nu: (0, 0))],
        out_specs=pl.BlockSpec((blk, D_MODEL), lambda b, be, nu: (b, 0)),
        scratch_shapes=[pltpu.VMEM((D_MODEL, 2 * D_FF), BF16),
                        pltpu.VMEM((D_FF, D_MODEL), BF16)],
    )
    lane = np.arange(D_FF) % LANES
    unit = (np.arange(D_FF) // LANES) * LANES + lane // 2 + (lane % 2) * (LANES // 2)
    perm = np.zeros((D_FF, D_FF), np.float32)
    perm[np.arange(D_FF), unit] = 1.0
    return pl.pallas_call(
        _expert_kernel,
        grid_spec=gs,
        out_shape=jax.ShapeDtypeStruct(xb.shape, F32),
        compiler_params=_cparams(("arbitrary",)),
        name="experts",
    )(bexp, nused, xb, w_gate_up, b_gate_up[:, :, None, :], w_down, b_down[:, :, None, :],
      jnp.asarray(perm, BF16))


def _combine_kernel(x1_ref, slot_ref, slot_next_ref, gate_ref, yb_ref, g_ref, lng, lnb,
                    o_ref, gbuf, sems):
    tm = x1_ref.shape[0]
    i = pl.program_id(0)
    cur = i % 2

    def row_copy(src_row, buf, k, t):
        return pltpu.make_async_copy(yb_ref.at[pl.ds(src_row, 1)],
                                     gbuf.at[buf, k, pl.ds(t, 1)], sems.at[buf])

    def gather(slots, buf):
        def per_token(g, carry):
            for u in range(2):
                t = g * 2 + u
                for k in range(TOP_K):
                    row_copy(slots[t * TOP_K + k], buf, k, t).start()
            return carry
        lax.fori_loop(0, tm // 2, per_token, 0)

    @pl.when(i == 0)
    def _():
        gather(slot_ref, 0)

    @pl.when(i + 1 < pl.num_programs(0))
    def _():
        gather(slot_next_ref, 1 - cur)

    for k in range(TOP_K):
        pltpu.make_async_copy(yb_ref.at[pl.ds(0, tm)], gbuf.at[cur, k], sems.at[cur]).wait()

    def norm_rows(ci, carry):
        rows = pl.ds(pl.multiple_of(ci * NORM_ROWS, NORM_ROWS), NORM_ROWS)
        gate = gate_ref[rows, :]
        f = gate[:, 0:1] * gbuf[cur, 0, rows, :]
        for k in range(1, TOP_K):
            f = f + gate[:, k:k + 1] * gbuf[cur, k, rows, :]
        r = ALPHA * x1_ref[rows, :] + g_ref[0] * f
        o_ref[rows, :] = _ln_rows(r) * lng[...] + lnb[...]
        return carry

    lax.fori_loop(0, tm // NORM_ROWS, norm_rows, 0, unroll=True)


def _combine(x1, slot_flat, gates, yb, gate_mod, lng, lnb, n_lat_tiles):
    t = x1.shape[0]
    tm = ROW_TILE
    n_tiles = t // tm
    return pl.pallas_call(
        _combine_kernel,
        grid=(n_tiles,),
        in_specs=[pl.BlockSpec((tm, D_MODEL), lambda i: (i, 0)),
                  pl.BlockSpec((tm * TOP_K,), lambda i: (i,), memory_space=pltpu.SMEM),
                  pl.BlockSpec((tm * TOP_K,), lambda i: (jnp.minimum(i + 1, n_tiles - 1),),
                               memory_space=pltpu.SMEM),
                  pl.BlockSpec((tm, LANES), lambda i: (i, 0)),
                  pl.BlockSpec(memory_space=pl.ANY),
                  pl.BlockSpec((1, 1, D_MODEL),
                               lambda i: (jnp.where(i >= n_lat_tiles, 1, 0), 0, 0)),
                  pl.BlockSpec((1, D_MODEL), lambda i: (0, 0)),
                  pl.BlockSpec((1, D_MODEL), lambda i: (0, 0))],
        out_specs=pl.BlockSpec((tm, D_MODEL), lambda i: (i, 0)),
        out_shape=jax.ShapeDtypeStruct((t, D_MODEL), F32),
        scratch_shapes=[pltpu.VMEM((2, TOP_K, tm, D_MODEL), F32),
                        pltpu.SemaphoreType.DMA((2,))],
        compiler_params=_cparams(("arbitrary",), row_dma=True),
        name="combine",
    )(x1, slot_flat, slot_flat, gates, yb, gate_mod, lng, lnb)


def _rope_tables(s_len, cl):
    t = jnp.arange(s_len, dtype=jnp.int32)
    quarter = HEAD_DIM // 4
    inv = ROPE_BASE ** (-jnp.arange(quarter, dtype=F32) / quarter)
    ang_r = (t // GRID_W).astype(F32)[:, None] * inv
    ang_c = (t % GRID_W).astype(F32)[:, None] * inv
    cr, sr, cc, sc = jnp.cos(ang_r), jnp.sin(ang_r), jnp.cos(ang_c), jnp.sin(ang_c)
    cos_t = jnp.concatenate([cr, cr, cc, cc], axis=1)
    sin_t = jnp.concatenate([-sr, sr, -sc, sc], axis=1)
    cos_t = jnp.concatenate([cos_t, jnp.ones((cl, HEAD_DIM), F32)], axis=0)
    sin_t = jnp.concatenate([sin_t, jnp.zeros((cl, HEAD_DIM), F32)], axis=0)
    return cos_t, sin_t


def _block_plan(counts, eid, rank, n_blocks):
    blocks_per = (counts + MOE_BLOCK - 1) // MOE_BLOCK
    block_end = jnp.cumsum(blocks_per)
    base = (block_end - blocks_per) * MOE_BLOCK
    bexp = jnp.minimum(jnp.sum(block_end[None, :] <= jnp.arange(n_blocks)[:, None], axis=1),
                       N_EXPERTS - 1).astype(jnp.int32)
    nused = block_end[-1:].astype(jnp.int32)
    experts = jnp.arange(N_EXPERTS, dtype=jnp.int32)
    slot = rank + jnp.sum(jnp.where(eid[:, :, None] == experts, base, 0), axis=-1)
    pad = blocks_per * MOE_BLOCK - counts
    pcount = jnp.concatenate([pad, jnp.sum(pad)[None], nused]).astype(jnp.int32)
    pstart = (base + counts).astype(jnp.int32)
    return bexp, nused, slot.reshape(-1).astype(jnp.int32), pstart, pcount


def kernel(x, c, ctx, c_ctx, w_ada, b_ada, w_in, w_out, sg_w, sg_b, na_rpb, ln1_g, ln1_b,
           ln2_g, ln2_b, w_router, b_router, w_gate_up, b_gate_up, w_down, b_down):
    bsz, s_len, dm = x.shape
    cl = ctx.shape[1]
    depth = w_ada.shape[0]
    assert bsz == 1 and dm == D_MODEL and cl == ROW_TILE and MOE_BLOCK == ROW_TILE
    assert s_len % (GRID_W * KEY_ROWS) == 0 and s_len % ROW_TILE == 0
    t_all = s_len + cl
    n_lat_tiles = s_len // ROW_TILE
    rows = s_len // GRID_W

    cond = jnp.zeros((8, dm), F32).at[0].set(c[0]).at[1].set(c_ctx)
    mod = _ada_mod(cond, w_ada, b_ada)[:, :2].reshape(depth, 2, 6, 1, dm)
    cos_t, sin_t = _rope_tables(s_len, cl)
    pb, vrow = _na_bias_tables(na_rpb, rows)
    cs, m1, tw, mc = _fourier_consts(s_len, cl)
    tri = jnp.asarray(np.tril(np.ones((ROW_TILE, ROW_TILE), np.float32), -1), BF16)
    wr_pad = jnp.pad(w_router, ((0, 0), (0, 0), (0, LANES - N_EXPERTS)))
    wr_hi = wr_pad.astype(BF16)
    wr_split = jnp.stack([wr_hi, (wr_pad - wr_hi.astype(F32)).astype(BF16)], axis=1)
    br_pad = jnp.pad(b_router, ((0, 0), (0, LANES - N_EXPERTS)), constant_values=NEG_INF)
    w_in_bf = w_in.astype(BF16)
    w_out_bf = w_out.astype(BF16)
    sgw_bf = sg_w.astype(BF16)

    xall = jnp.concatenate([x[0], ctx[0]], axis=0)
    for l in range(depth):
        last = l == depth - 1
        m = lambda j: mod[l, :, j]
        qp, qr, kr, v, z_sg, z_ft = _proj(xall, m(0), m(1), w_in_bf, cos_t, sin_t, l,
                                          n_lat_tiles)
        ona_l = _na_attention(qr, qp, kr, v, pb, vrow, l, s_len)
        osg = _spatial_gating(z_sg, sgw_bf[l], sg_b[l][:, :, None])
        oft_l = _fourier_latent(z_ft, cs, m1, tw, s_len)
        if last:
            ona_c = oft_c = None
            n_rows = s_len
        else:
            ona_c = _ctx_attention(qp, kr, v, s_len, cl)
            oft_c = _fourier_ctx(z_ft, cs, mc, s_len, cl)
            n_rows = t_all
        x1, h2, logits = _out_proj(
            ona_l, ona_c, osg, oft_l, oft_c, xall, w_out_bf, m(2),
            ln1_g[l][None], ln1_b[l][None], m(3), m(4), wr_split, br_pad[l][None],
            l, n_rows, n_lat_tiles, not last)
        eid, rank, gates, counts = _route(logits, tri)
        n_blocks = -(-n_rows * TOP_K // MOE_BLOCK) + N_EXPERTS
        bexp, nused, slot_flat, pstart, pcount = _block_plan(
            counts[0, :N_EXPERTS], eid[:, :TOP_K], rank[:, :TOP_K], n_blocks)
        xb = _dispatch(pstart, pcount, h2, slot_flat, n_blocks * MOE_BLOCK)
        yb = _experts(bexp, nused, xb, w_gate_up, b_gate_up, w_down, b_down, l, n_blocks)
        xall = _combine(x1, slot_flat, gates, yb, m(5), ln2_g[l][None], ln2_b[l][None],
                        n_lat_tiles)
    return xall[None]
```

```python
import functools
import math

import numpy as np
import jax
import jax.numpy as jnp
from jax import lax
from jax.experimental import pallas as pl
from jax.experimental.pallas import tpu as pltpu

D_MODEL = 2048
DEPTH_NORM = 4
GRID_W = 64
HEAD_DIM = 128
NA_HEADS = 8
NA_WIDTH = NA_HEADS * HEAD_DIM
NA_KH = 8
NA_KW = 16
SG_GROUPS = 4
SG_WIDTH = 512
SG_CHUNK = 128
FT_GROUPS = 4
FT_WIDTH = 512
SG_OFF = 3 * NA_WIDTH
FT_OFF = SG_OFF + 2 * SG_WIDTH
IN_WIDTH = FT_OFF + FT_WIDTH
ROPE_BASE = 10000.0
N_EXPERTS = 32
TOP_K = 4
D_FF = D_MODEL // 4
SWIGLU_ALPHA = 1.702
SWIGLU_LIMIT = 7.0
LN_EPS = 1e-5
NEG_INF = -1e30
ALPHA = (2 * DEPTH_NORM) ** 0.25

LANES = 128
ROW_TILE = 256
MOE_BLOCK = 256
SLOT_BLOCKS = 4
NORM_ROWS = 32
KEY_ROWS = 16
Q_ROWS = 8
NA_PAIRS = 5
VMEM_LIMIT = 56 * 1024 * 1024

F32 = jnp.float32
BF16 = jnp.bfloat16


def _cparams(sem, row_dma=False):
    return pltpu.CompilerParams(dimension_semantics=sem, vmem_limit_bytes=VMEM_LIMIT,
                                disable_bounds_checks=row_dma)


def _dot(a, b):
    return jnp.dot(a, b, preferred_element_type=F32)


def _dot_nt(a, b):
    return lax.dot_general(a, b, (((1,), (1,)), ((), ())), preferred_element_type=F32)


def _ln_rows(x):
    mu = jnp.mean(x, axis=-1, keepdims=True)
    xc = x - mu
    var = jnp.mean(xc * xc, axis=-1, keepdims=True)
    return xc * lax.rsqrt(var + LN_EPS)


def _ada_kernel(c_ref, w_ref, b_ref, o_ref):
    c = c_ref[...]
    s = c * jax.nn.sigmoid(c)
    o_ref[0] = jnp.dot(s, w_ref[0], preferred_element_type=F32,
                       precision=lax.Precision.HIGHEST) + b_ref[0]


def _ada_mod(cond, w_ada, b_ada):
    depth, d, n = w_ada.shape
    tn = 1536
    return pl.pallas_call(
        _ada_kernel,
        grid=(depth, n // tn),
        in_specs=[pl.BlockSpec((8, d), lambda l, j: (0, 0)),
                  pl.BlockSpec((1, d, tn), lambda l, j: (l, 0, j)),
                  pl.BlockSpec((1, 1, tn), lambda l, j: (l, 0, j))],
        out_specs=pl.BlockSpec((1, 8, tn), lambda l, j: (l, 0, j)),
        out_shape=jax.ShapeDtypeStruct((depth, 8, n), F32),
        compiler_params=_cparams(("arbitrary", "arbitrary")),
        name="ada_mod",
    )(cond, w_ada, b_ada.reshape(depth, 1, n))


def _proj_kernel(x_ref, sh_ref, sc_ref, w_ref, cos_ref, sin_ref,
                 qp_ref, qr_ref, kr_ref, v_ref, sg_ref, ft_ref):
    y = _ln_rows(x_ref[...])
    h = (y * (1.0 + sc_ref[0]) + sh_ref[0]).astype(BF16)
    cos = cos_ref[...]
    sin = sin_ref[...]
    lane = lax.broadcasted_iota(jnp.int32, cos.shape, 1)
    first = (lane % 64) < 32

    def rope(z):
        swapped = jnp.where(first, pltpu.roll(z, 96, axis=1), pltpu.roll(z, 32, axis=1))
        return z * cos + swapped * sin

    nw = 512
    for j in range(IN_WIDTH // nw):
        z = _dot(h, w_ref[:, j * nw:(j + 1) * nw])
        for p in range(nw // LANES):
            col = j * nw + p * LANES
            zp = z[:, p * LANES:(p + 1) * LANES]
            if col < NA_WIDTH:
                zp = zp * (HEAD_DIM ** -0.5)
                qp_ref[:, col:col + LANES] = zp.astype(BF16)
                qr_ref[:, col:col + LANES] = rope(zp).astype(BF16)
            elif col < 2 * NA_WIDTH:
                c0 = col - NA_WIDTH
                kr_ref[:, c0:c0 + LANES] = rope(zp).astype(BF16)
            elif col < SG_OFF:
                c0 = col - 2 * NA_WIDTH
                v_ref[:, c0:c0 + LANES] = zp.astype(BF16)
            elif col < FT_OFF:
                c0 = col - SG_OFF
                sg_ref[:, c0:c0 + LANES] = zp
            else:
                c0 = col - FT_OFF
                ft_ref[:, c0:c0 + LANES] = zp.astype(BF16)


def _proj(xall, shift, scale, w_in_bf, cos_t, sin_t, layer, n_lat_tiles):
    t = xall.shape[0]
    tm = ROW_TILE
    typ = lambda i: (jnp.where(i >= n_lat_tiles, 1, 0), 0, 0)
    row = lambda i: (i, 0)
    return pl.pallas_call(
        _proj_kernel,
        grid=(t // tm,),
        in_specs=[pl.BlockSpec((tm, D_MODEL), row),
                  pl.BlockSpec((1, 1, D_MODEL), typ),
                  pl.BlockSpec((1, 1, D_MODEL), typ),
                  pl.BlockSpec((None, D_MODEL, IN_WIDTH), lambda i: (layer, 0, 0),
                               pipeline_mode=pl.Buffered(1)),
                  pl.BlockSpec((tm, LANES), row),
                  pl.BlockSpec((tm, LANES), row)],
        out_specs=[pl.BlockSpec((tm, NA_WIDTH), row)] * 4
        + [pl.BlockSpec((tm, 2 * SG_WIDTH), row), pl.BlockSpec((tm, FT_WIDTH), row)],
        out_shape=[jax.ShapeDtypeStruct((t, NA_WIDTH), BF16)] * 4
        + [jax.ShapeDtypeStruct((t, 2 * SG_WIDTH), F32),
           jax.ShapeDtypeStruct((t, FT_WIDTH), BF16)],
        compiler_params=_cparams(("arbitrary",)),
        name="ln_proj",
    )(xall, shift, scale, w_in_bf, cos_t, sin_t)


def _na_kernel(qr_ref, qp_ref, k0, k1, k2, k3, v0, v1, v2, v3, kc_ref, vc_ref, pb_ref, vrow_ref,
               o_ref, k_scr, v_scr, s_scr, p_scr, den_scr, *, n_rb):
    i = pl.program_id(1)
    is_first = i == 0
    is_last = i == n_rb - 1
    off = jnp.where(is_first, 0, jnp.where(is_last, -(KEY_ROWS - Q_ROWS), -(NA_KH // 2)))
    kb = k0.shape[0]
    for j, (kj, vj) in enumerate(((k0, v0), (k1, v1), (k2, v2), (k3, v3))):
        k_scr[j * kb:(j + 1) * kb, :] = kj[...]
        v_scr[j * kb:(j + 1) * kb, :] = vj[...]
    kc = kc_ref[...]
    vc = vc_ref[...]
    qsub = 128
    pair = 2 * GRID_W
    starts = ((0, 0, 0, 1), (0, 1, 2, 3), (2, 3, 3, 3))
    n_sub = Q_ROWS * GRID_W // qsub
    n_lat = NA_PAIRS * pair
    key0 = []
    for s in range(n_sub):
        rows = slice(s * qsub, (s + 1) * qsub)
        p0 = jnp.where(is_first, starts[0][s], jnp.where(is_last, starts[2][s], starts[1][s]))
        key0.append(pl.multiple_of(p0 * pair, pair))
        bias_rows = []
        for qr in range(s * qsub // GRID_W, (s + 1) * qsub // GRID_W):
            tiles = [pb_ref[0, 0, jnp.clip(2 * (p0 + j) - qr + off + NA_KH, 0, 2 * NA_KH - 1)]
                     + vrow_ref[0, qr, p0 + j]
                     for j in range(NA_PAIRS)]
            bias_rows.append(jnp.concatenate(tiles, axis=1))
        bias = jnp.concatenate(bias_rows, axis=0)
        s_scr[s, :, 0:n_lat] = _dot_nt(qr_ref[rows, :], k_scr[pl.ds(key0[s], n_lat), :]) + bias
        s_scr[s, :, n_lat:] = _dot_nt(qp_ref[rows, :], kc)
    for s in range(n_sub):
        sc = s_scr[s]
        p = jnp.exp(sc - jnp.max(sc, axis=-1, keepdims=True))
        p_scr[s] = p.astype(BF16)
        den_scr[s] = jnp.sum(p, axis=-1, keepdims=True)
    for s in range(n_sub):
        rows = slice(s * qsub, (s + 1) * qsub)
        o = (_dot(p_scr[s, :, 0:n_lat], v_scr[pl.ds(key0[s], n_lat), :])
             + _dot(p_scr[s, :, n_lat:], vc))
        o_ref[rows, :] = (o / den_scr[s]).astype(BF16)


def _na_attention(qr, qp, kr, v, pb, vrow, layer, s_len):
    rows = s_len // GRID_W
    n_rb = rows // Q_ROWS
    qb = Q_ROWS * GRID_W
    kb = 256
    last_kblock = (rows - KEY_ROWS) * GRID_W // kb
    ctx_block = s_len // kb

    def kmap(j):
        return lambda h, i: (jnp.clip(2 * i - 1, 0, last_kblock) + j, h)

    def btype(i):
        return jnp.where(i == 0, 0, jnp.where(i == n_rb - 1, 2, 1))

    qspec = pl.BlockSpec((qb, HEAD_DIM), lambda h, i: (i, h))
    kspecs = [pl.BlockSpec((kb, HEAD_DIM), kmap(j)) for j in range(4)]
    cspec = pl.BlockSpec((kb, HEAD_DIM), lambda h, i: (ctx_block, h))
    return pl.pallas_call(
        functools.partial(_na_kernel, n_rb=n_rb),
        grid=(NA_HEADS, n_rb),
        in_specs=[qspec, qspec] + kspecs + kspecs + [cspec, cspec]
        + [pl.BlockSpec((1, 1, 2 * NA_KH, GRID_W, 2 * GRID_W), lambda h, i: (layer, h, 0, 0, 0)),
           pl.BlockSpec((1, Q_ROWS, KEY_ROWS // 2, 1, 2 * GRID_W),
                        lambda h, i: (btype(i), 0, 0, 0, 0))],
        out_specs=pl.BlockSpec((qb, HEAD_DIM), lambda h, i: (i, h)),
        out_shape=jax.ShapeDtypeStruct((s_len, NA_WIDTH), BF16),
        scratch_shapes=[pltpu.VMEM((KEY_ROWS * GRID_W, HEAD_DIM), BF16),
                        pltpu.VMEM((KEY_ROWS * GRID_W, HEAD_DIM), BF16),
                        pltpu.VMEM((qb // 128, 128, NA_PAIRS * 2 * GRID_W + kb), F32),
                        pltpu.VMEM((qb // 128, 128, NA_PAIRS * 2 * GRID_W + kb), BF16),
                        pltpu.VMEM((qb // 128, 128, 1), F32)],
        compiler_params=_cparams(("arbitrary", "arbitrary")),
        name="na_attention",
    )(qr, qp, kr, kr, kr, kr, v, v, v, v, kr, v, pb, vrow)


def _ctx_attn_kernel(q_ref, k_ref, v_ref, o_ref):
    s = _dot_nt(q_ref[...], k_ref[...])
    m = jnp.max(s, axis=-1, keepdims=True)
    p = jnp.exp(s - m)
    den = jnp.sum(p, axis=-1, keepdims=True)
    o_ref[...] = (_dot(p.astype(BF16), v_ref[...]) / den).astype(BF16)


def _ctx_attention(qp, kr, v, s_len, cl):
    blk = s_len // cl
    spec = pl.BlockSpec((cl, HEAD_DIM), lambda h: (blk, h))
    return pl.pallas_call(
        _ctx_attn_kernel,
        grid=(NA_HEADS,),
        in_specs=[spec, spec, spec],
        out_specs=pl.BlockSpec((cl, HEAD_DIM), lambda h: (0, h)),
        out_shape=jax.ShapeDtypeStruct((cl, NA_WIDTH), BF16),
        compiler_params=_cparams(("arbitrary",)),
        name="ctx_attention",
    )(qp, kr, v)


def _na_bias_tables(na_rpb, rows):
    n_rb = rows // Q_ROWS
    rmask = np.zeros((3, Q_ROWS, KEY_ROWS), bool)
    for t, i in enumerate((0, 1, n_rb - 1)):
        ks = min(max(Q_ROWS * i - NA_KH // 2, 0), rows - KEY_ROWS)
        for qr in range(Q_ROWS):
            r = Q_ROWS * i + qr
            lo = min(max(r - NA_KH // 2, 0), rows - NA_KH)
            for kr in range(KEY_ROWS):
                rmask[t, qr, kr] = lo <= ks + kr < lo + NA_KH
    vrow = np.where(np.repeat(rmask, GRID_W, axis=2), 0.0, NEG_INF).astype(np.float32)
    csel = np.zeros((GRID_W, GRID_W, 2 * NA_KW - 1), np.float32)
    cmask = np.zeros((GRID_W, GRID_W), bool)
    for qc in range(GRID_W):
        lo = min(max(qc - NA_KW // 2, 0), GRID_W - NA_KW)
        for kc in range(GRID_W):
            cmask[qc, kc] = lo <= kc < lo + NA_KW
            csel[qc, kc, min(max(kc - qc + NA_KW - 1, 0), 2 * NA_KW - 2)] = 1.0
    b = jnp.einsum('lhab,qkb->lhaqk', na_rpb, jnp.asarray(csel), precision=lax.Precision.HIGHEST)
    b = jnp.where(jnp.asarray(cmask), b, NEG_INF)
    b = jnp.pad(b, ((0, 0), (0, 0), (1, 1), (0, 0), (0, 0)))
    pb = jnp.concatenate([b[:, :, :-1], b[:, :, 1:]], axis=-1)
    return pb, jnp.asarray(vrow).reshape(3, Q_ROWS, KEY_ROWS // 2, 1, 2 * GRID_W)


def _sg_kernel(z_ref, w_ref, b_ref, o_ref):
    z = z_ref[...]
    g = 0.5 * z * (1.0 + lax.erf(z * (2.0 ** -0.5)))
    n_chunks = z.shape[0] // SG_CHUNK
    for gi in range(SG_GROUPS):
        u = g[:, gi * LANES:(gi + 1) * LANES]
        vn = _ln_rows(g[:, SG_WIDTH + gi * LANES:SG_WIDTH + (gi + 1) * LANES]).astype(BF16)
        for n in range(n_chunks):
            rows = slice(n * SG_CHUNK, (n + 1) * SG_CHUNK)
            t = _dot(w_ref[gi], vn[rows, :]) + b_ref[gi]
            o_ref[rows, gi * LANES:(gi + 1) * LANES] = (u[rows, :] * t).astype(BF16)


def _spatial_gating(z_sg, sgw_bf, sgb):
    t = z_sg.shape[0]
    tm = ROW_TILE
    return pl.pallas_call(
        _sg_kernel,
        grid=(t // tm,),
        in_specs=[pl.BlockSpec((tm, 2 * SG_WIDTH), lambda i: (i, 0)),
                  pl.BlockSpec((SG_GROUPS, SG_CHUNK, SG_CHUNK), lambda i: (0, 0, 0)),
                  pl.BlockSpec((SG_GROUPS, SG_CHUNK, 1), lambda i: (0, 0, 0))],
        out_specs=pl.BlockSpec((tm, SG_WIDTH), lambda i: (i, 0)),
        out_shape=jax.ShapeDtypeStruct((t, SG_WIDTH), BF16),
        compiler_params=_cparams(("arbitrary",)),
        name="spatial_gating",
    )(z_sg, sgw_bf, sgb)


def _ft1_kernel(z_ref, cs_ref, m1_ref, y_ref):
    n1 = z_ref.shape[0]
    parts = []
    for j in range(z_ref.shape[1] // LANES):
        ab = _dot(z_ref[:, j * LANES:(j + 1) * LANES], cs_ref[...])
        parts.append(jnp.concatenate([ab[:, :LANES], ab[:, LANES:]], axis=0))
    st = jnp.concatenate(parts, axis=1).astype(BF16)
    y = _dot(m1_ref[...], st)
    y_ref[0] = y[:n1].astype(BF16)
    y_ref[1] = y[n1:].astype(BF16)


def _ft2_kernel(y_ref, tw_ref, o_ref, *, scale):
    for j in range(y_ref.shape[1]):
        yy = jnp.concatenate([y_ref[0, j], y_ref[1, j]], axis=0)
        o = _dot(tw_ref[j], yy) * scale
        o_ref[:, j * FT_WIDTH:(j + 1) * FT_WIDTH] = o.astype(BF16)


def _fourier_consts(s_len, cl):
    n2 = 128
    n1 = s_len // n2
    c = np.arange(HEAD_DIM)
    ang = 2 * np.pi * ((c[:, None] * c[None, :]) % HEAD_DIM) / HEAD_DIM
    cs = np.concatenate([np.cos(ang), np.sin(ang)], axis=1)
    a = np.arange(n1)
    ang1 = 2 * np.pi * ((a[:, None] * a[None, :]) % n1) / n1
    fc, fs = np.cos(ang1), np.sin(ang1)
    m1 = np.block([[fc, -fs], [-fs, -fc]])
    ka = jnp.arange(n1, dtype=jnp.int32)[:, None, None]
    kb = jnp.arange(n2, dtype=jnp.int32)[None, :, None]
    nn = jnp.arange(n2, dtype=jnp.int32)[None, None, :]
    ph = (nn * (ka + n1 * kb)) % s_len
    th = ph.astype(F32) * (2 * np.pi / s_len)
    tw = jnp.concatenate([jnp.cos(th), jnp.sin(th)], axis=2).astype(BF16)
    p = np.arange(cl)
    angc = 2 * np.pi * ((p[:, None] * p[None, :]) % cl) / cl
    mc = np.concatenate([np.cos(angc), -np.sin(angc)], axis=1)
    return (jnp.asarray(cs, BF16), jnp.asarray(m1, BF16), tw, jnp.asarray(mc, BF16))


def _fourier_latent(z_ft, cs, m1, tw, s_len):
    n2 = 128
    n1 = s_len // n2
    width = n2 * FT_WIDTH
    cw = 2048
    y = pl.pallas_call(
        _ft1_kernel,
        grid=(width // cw,),
        in_specs=[pl.BlockSpec((n1, cw), lambda j: (0, j)),
                  pl.BlockSpec((HEAD_DIM, 2 * HEAD_DIM), lambda j: (0, 0)),
                  pl.BlockSpec((2 * n1, 2 * n1), lambda j: (0, 0))],
        out_specs=pl.BlockSpec((2, n1, cw), lambda j: (0, 0, j)),
        out_shape=jax.ShapeDtypeStruct((2, n1, width), BF16),
        compiler_params=_cparams(("arbitrary",)),
        name="fourier_stage1",
    )(z_ft[:s_len].reshape(n1, width), cs, m1)
    kab = 4
    out = pl.pallas_call(
        functools.partial(_ft2_kernel, scale=(s_len * HEAD_DIM) ** -0.5),
        grid=(n1 // kab,),
        in_specs=[pl.BlockSpec((2, kab, n2, FT_WIDTH), lambda j: (0, j, 0, 0)),
                  pl.BlockSpec((kab, n2, 2 * n2), lambda j: (j, 0, 0))],
        out_specs=pl.BlockSpec((n2, kab * FT_WIDTH), lambda j: (0, j)),
        out_shape=jax.ShapeDtypeStruct((n2, n1 * FT_WIDTH), BF16),
        compiler_params=_cparams(("arbitrary",)),
        name="fourier_stage2",
    )(y.reshape(2, n1, n2, FT_WIDTH), tw)
    return out.reshape(s_len, FT_WIDTH)


def _ft_ctx_kernel(z_ref, cs_ref, mc_ref, o_ref, *, scale):
    for g in range(FT_GROUPS):
        ab = _dot(z_ref[:, g * LANES:(g + 1) * LANES], cs_ref[...])
        st = jnp.concatenate([ab[:, :LANES], ab[:, LANES:]], axis=0).astype(BF16)
        o_ref[:, g * LANES:(g + 1) * LANES] = (_dot(mc_ref[...], st) * scale).astype(BF16)


def _fourier_ctx(z_ft, cs, mc, s_len, cl):
    blk = s_len // cl
    return pl.pallas_call(
        functools.partial(_ft_ctx_kernel, scale=(cl * HEAD_DIM) ** -0.5),
        grid=(1,),
        in_specs=[pl.BlockSpec((cl, FT_WIDTH), lambda i: (blk, 0)),
                  pl.BlockSpec((HEAD_DIM, 2 * HEAD_DIM), lambda i: (0, 0)),
                  pl.BlockSpec((cl, 2 * cl), lambda i: (0, 0))],
        out_specs=pl.BlockSpec((cl, FT_WIDTH), lambda i: (0, 0)),
        out_shape=jax.ShapeDtypeStruct((cl, FT_WIDTH), BF16),
        compiler_params=_cparams(("arbitrary",)),
        name="fourier_ctx",
    )(z_ft, cs, mc)


def _out_kernel(*refs, with_ctx, n_lat_tiles):
    if with_ctx:
        (ona_l, ona_c, osg, oft_l, oft_c, x_ref, w_ref, g_ref, lng, lnb, sh_ref, sc_ref,
         wr_ref, br_ref, x1_ref, h2_ref, lg_ref, r_scr, hs_scr) = refs
        is_ctx = pl.program_id(0) >= n_lat_tiles
        ona = jnp.where(is_ctx, ona_c[...], ona_l[...])
        oft = jnp.where(is_ctx, oft_c[...], oft_l[...])
    else:
        (ona_l, osg, oft_l, x_ref, w_ref, g_ref, lng, lnb, sh_ref, sc_ref,
         wr_ref, br_ref, x1_ref, h2_ref, lg_ref, r_scr, hs_scr) = refs
        ona = ona_l[...]
        oft = oft_l[...]
    sg = osg[...]
    nw = 512
    for n in range(D_MODEL // nw):
        cols = slice(n * nw, (n + 1) * nw)
        y = (_dot(ona, w_ref[0:NA_WIDTH, cols])
             + _dot(sg, w_ref[NA_WIDTH:NA_WIDTH + SG_WIDTH, cols])
             + _dot(oft, w_ref[NA_WIDTH + SG_WIDTH:, cols]))
        r_scr[:, cols] = ALPHA * x_ref[:, cols] + g_ref[0][:, cols] * y

    def norm_rows(ci, carry):
        rows = pl.ds(pl.multiple_of(ci * NORM_ROWS, NORM_ROWS), NORM_ROWS)
        x1 = _ln_rows(r_scr[rows, :]) * lng[...] + lnb[...]
        x1_ref[rows, :] = x1
        h2 = _ln_rows(x1) * (1.0 + sc_ref[0]) + sh_ref[0]
        h2_ref[rows, :] = h2
        hi = h2.astype(BF16)
        hs_scr[0, rows, :] = hi
        hs_scr[1, rows, :] = (h2 - hi.astype(F32)).astype(BF16)
        return carry

    lax.fori_loop(0, r_scr.shape[0] // NORM_ROWS, norm_rows, 0, unroll=True)
    lg_ref[...] = (_dot(hs_scr[0], wr_ref[0]) + _dot(hs_scr[1], wr_ref[0])
                   + _dot(hs_scr[0], wr_ref[1]) + br_ref[...])


def _out_proj(ona_l, ona_c, osg, oft_l, oft_c, xall, w_out_bf, gate, lng, lnb, shift, scale,
              wr_split, br_pad, layer, n_rows, n_lat_tiles, with_ctx):
    tm = ROW_TILE
    row = lambda i: (i, 0)
    lat = lambda i: (jnp.minimum(i, n_lat_tiles - 1), 0)
    typ = lambda i: (jnp.where(i >= n_lat_tiles, 1, 0), 0, 0)
    const = lambda i: (0, 0)
    modspec = pl.BlockSpec((1, 1, D_MODEL), typ)
    vec = pl.BlockSpec((1, D_MODEL), const)
    specs = [pl.BlockSpec((tm, NA_WIDTH), lat)]
    args = [ona_l]
    if with_ctx:
        specs.append(pl.BlockSpec((tm, NA_WIDTH), const))
        args.append(ona_c)
    specs.append(pl.BlockSpec((tm, SG_WIDTH), row))
    args.append(osg)
    specs.append(pl.BlockSpec((tm, FT_WIDTH), lat))
    args.append(oft_l)
    if with_ctx:
        specs.append(pl.BlockSpec((tm, FT_WIDTH), const))
        args.append(oft_c)
    specs += [pl.BlockSpec((tm, D_MODEL), row),
              pl.BlockSpec((None, D_MODEL, D_MODEL), lambda i: (layer, 0, 0),
                           pipeline_mode=pl.Buffered(1)),
              modspec, vec, vec, modspec, modspec,
              pl.BlockSpec((None, 2, D_MODEL, LANES), lambda i: (layer, 0, 0, 0)),
              pl.BlockSpec((1, LANES), const)]
    args += [xall, w_out_bf, gate, lng, lnb, shift, scale, wr_split, br_pad]
    return pl.pallas_call(
        functools.partial(_out_kernel, with_ctx=with_ctx, n_lat_tiles=n_lat_tiles),
        grid=(n_rows // tm,),
        in_specs=specs,
        out_specs=[pl.BlockSpec((tm, D_MODEL), row),
                   pl.BlockSpec((tm, D_MODEL), row),
                   pl.BlockSpec((tm, LANES), row)],
        out_shape=[jax.ShapeDtypeStruct((n_rows, D_MODEL), F32),
                   jax.ShapeDtypeStruct((n_rows, D_MODEL), F32),
                   jax.ShapeDtypeStruct((n_rows, LANES), F32)],
        scratch_shapes=[pltpu.VMEM((tm, D_MODEL), F32),
                        pltpu.VMEM((2, tm, D_MODEL), BF16)],
        compiler_params=_cparams(("arbitrary",)),
        name="out_proj",
    )(*args)


def _route_kernel(lg_ref, tri_ref, eid_ref, rank_ref, gate_ref, cnt_ref, carry):
    @pl.when(pl.program_id(0) == 0)
    def _():
        carry[...] = jnp.zeros_like(carry)

    work = lg_ref[...]
    lane = lax.broadcasted_iota(jnp.int32, work.shape, 1)
    lane_f = lane.astype(F32)
    vals, ids, hots = [], [], []
    for _ in range(TOP_K):
        m = jnp.max(work, axis=-1, keepdims=True)
        idx = jnp.min(jnp.where(work == m, lane_f, float(LANES)), axis=-1, keepdims=True)
        hot = lane_f == idx
        vals.append(m)
        ids.append(idx)
        hots.append(hot)
        work = jnp.where(hot, -jnp.inf, work)
    exps = [jnp.exp(v - vals[0]) for v in vals]
    den = exps[0] + exps[1] + exps[2] + exps[3]
    multi = jnp.zeros(work.shape, F32)
    for hot in hots:
        multi = multi + hot.astype(F32)
    pref = _dot(tri_ref[...], multi.astype(BF16)) + carry[...]
    eid = jnp.zeros(work.shape, F32)
    rank = jnp.zeros(work.shape, F32)
    gate = jnp.zeros(work.shape, F32)
    for k in range(TOP_K):
        rk = jnp.sum(jnp.where(hots[k], pref, 0.0), axis=-1, keepdims=True)
        sel = lane == k
        eid = jnp.where(sel, ids[k], eid)
        rank = jnp.where(sel, rk, rank)
        gate = jnp.where(sel, exps[k] / den, gate)
    eid_ref[...] = eid.astype(jnp.int32)
    rank_ref[...] = rank.astype(jnp.int32)
    gate_ref[...] = gate
    carry[...] = carry[...] + jnp.sum(multi, axis=0, keepdims=True)
    cnt_ref[...] = jnp.broadcast_to(carry[...], cnt_ref.shape).astype(jnp.int32)


def _route(logits, tri):
    t = logits.shape[0]
    tm = ROW_TILE
    row = lambda i: (i, 0)
    return pl.pallas_call(
        _route_kernel,
        grid=(t // tm,),
        in_specs=[pl.BlockSpec((tm, LANES), row),
                  pl.BlockSpec((tm, tm), lambda i: (0, 0))],
        out_specs=[pl.BlockSpec((tm, LANES), row)] * 3
        + [pl.BlockSpec((8, LANES), lambda i: (0, 0))],
        out_shape=[jax.ShapeDtypeStruct((t, LANES), jnp.int32)] * 2
        + [jax.ShapeDtypeStruct((t, LANES), F32),
           jax.ShapeDtypeStruct((8, LANES), jnp.int32)],
        scratch_shapes=[pltpu.VMEM((1, LANES), F32)],
        compiler_params=_cparams(("arbitrary",)),
        name="route",
    )(logits, tri)


def _expert_kernel(bexp_ref, nused_ref, tok_ref, tok_next_ref, dst_prev_ref, dst_ref, h2_ref,
                   wgu_ref, bgu_ref, wd_ref, bd_ref, perm_ref, y_ref, wgu_s, wd_s, xb0, xb1, yb0,
                   yb1, gsem, ssem):
    b = pl.program_id(0)
    n_used = nused_ref[0]
    active = b < n_used
    xbufs = (xb0, xb1)
    ybufs = (yb0, yb1)
    blk = xb0.shape[0]
    n_real = y_ref.shape[0] - 2 * blk

    def gather(idx_ref, block, p):
        off = (block % SLOT_BLOCKS) * blk
        for r in range(blk):
            pltpu.make_async_copy(h2_ref.at[pl.ds(idx_ref[off + r], 1)],
                                  xbufs[p].at[pl.ds(r, 1)], gsem.at[p]).start()

    def gather_wait(p):
        pltpu.make_async_copy(h2_ref.at[pl.ds(0, blk)], xbufs[p], gsem.at[p]).wait()

    def scatter(idx_ref, block, p):
        off = (block % SLOT_BLOCKS) * blk
        for r in range(blk):
            pltpu.make_async_copy(ybufs[p].at[pl.ds(r, 1)],
                                  y_ref.at[pl.ds(idx_ref[off + r], 1)], ssem.at[p]).start()

    def scatter_wait(p):
        pltpu.make_async_copy(ybufs[p], y_ref.at[pl.ds(0, blk)], ssem.at[p]).wait()

    @pl.when(b == 0)
    def _():
        yb0[...] = jnp.zeros(yb0.shape, F32)
        for j in range(2):
            fill = pltpu.make_async_copy(yb0, y_ref.at[pl.ds(n_real + j * blk, blk)], ssem.at[0])
            fill.start()
            fill.wait()
        gather(tok_ref, b, 0)

    fresh = jnp.logical_or(b == 0, bexp_ref[b] != bexp_ref[jnp.maximum(b - 1, 0)])

    @pl.when(jnp.logical_and(active, fresh))
    def _():
        wgu_s[...] = wgu_ref[...].astype(BF16)
        wd_s[...] = _dot(perm_ref[...], wd_ref[...].astype(BF16)).astype(BF16)

    def compute(p):
        gu = _dot(xbufs[p][...].astype(BF16), wgu_s[...]) + bgu_ref[...]
        lane = lax.broadcasted_iota(jnp.int32, (blk, LANES), 1)
        even = (lane % 2) == 0
        prods = []
        for c in range(2 * D_FF // LANES):
            guc = gu[:, c * LANES:(c + 1) * LANES]
            glu = jnp.minimum(guc, SWIGLU_LIMIT)
            lin = jnp.clip(guc, -SWIGLU_LIMIT, SWIGLU_LIMIT) + 1.0
            prods.append(glu * jax.nn.sigmoid(SWIGLU_ALPHA * glu)
                         * pltpu.roll(lin, LANES - 1, axis=1))
        merged = [jnp.where(even, prods[2 * m], pltpu.roll(prods[2 * m + 1], 1, axis=1))
                  for m in range(D_FF // LANES)]
        act = jnp.concatenate(merged, axis=1).astype(BF16)
        ybufs[p][...] = _dot(act, wd_s[...]) + bd_ref[...]

    nxt = jnp.minimum(b + 1, n_used - 1)
    is_last = b == n_used - 1

    @pl.when(jnp.logical_and(active, b == 0))
    def _():
        gather_wait(0)
        gather(tok_next_ref, nxt, 1)
        compute(0)

    for p in range(2):
        mine = jnp.logical_and(b > 0, b % 2 == p)

        @pl.when(jnp.logical_and(active, mine))
        def _():
            @pl.when(b >= 2)
            def _():
                scatter_wait(p)
            gather_wait(p)
            gather(tok_next_ref, nxt, 1 - p)
            scatter(dst_prev_ref, b - 1, 1 - p)
            compute(p)

    for p in range(2):
        @pl.when(jnp.logical_and(is_last, b % 2 == p))
        def _():
            scatter(dst_ref, b, p)
            gather_wait(1 - p)
            scatter_wait(p)

            @pl.when(b >= 1)
            def _():
                scatter_wait(1 - p)


def _experts(bexp, nused, slot_tok, slot_dst, h2, w_gate_up, b_gate_up, w_down, b_down, layer,
             n_blocks):
    blk = MOE_BLOCK
    n_assign = h2.shape[0] * TOP_K

    def emap(b, be, nu):
        return (layer, be[jnp.minimum(b, nu[0] - 1)], 0, 0)

    def slots(shift):
        def index(b, be, nu):
            return (jnp.clip(b + shift, 0, nu[0] - 1) // SLOT_BLOCKS,)
        return pl.BlockSpec((SLOT_BLOCKS * blk,), index, memory_space=pltpu.SMEM)

    gs = pltpu.PrefetchScalarGridSpec(
        num_scalar_prefetch=2,
        grid=(n_blocks,),
        in_specs=[slots(0), slots(1), slots(-1), slots(0),
                  pl.BlockSpec(memory_space=pl.ANY),
                  pl.BlockSpec((None, None, D_MODEL, 2 * D_FF), emap),
                  pl.BlockSpec((None, None, 1, 2 * D_FF), emap),
                  pl.BlockSpec((None, None, D_FF, D_MODEL), emap),
                  pl.BlockSpec((None, None, 1, D_MODEL), emap),
                  pl.BlockSpec((D_FF, D_FF), lambda b, be, nu: (0, 0))],
        out_specs=pl.BlockSpec(memory_space=pl.ANY),
        scratch_shapes=[pltpu.VMEM((D_MODEL, 2 * D_FF), BF16),
                        pltpu.VMEM((D_FF, D_MODEL), BF16),
                        pltpu.VMEM((blk, D_MODEL), F32), pltpu.VMEM((blk, D_MODEL), F32),
                        pltpu.VMEM((blk, D_MODEL), F32), pltpu.VMEM((blk, D_MODEL), F32),
                        pltpu.SemaphoreType.DMA((2,)),
                        pltpu.SemaphoreType.DMA((2,))],
    )
    lane = np.arange(D_FF) % LANES
    unit = (np.arange(D_FF) // LANES) * LANES + lane // 2 + (lane % 2) * (LANES // 2)
    perm = np.zeros((D_FF, D_FF), np.float32)
    perm[np.arange(D_FF), unit] = 1.0
    return pl.pallas_call(
        _expert_kernel,
        grid_spec=gs,
        out_shape=jax.ShapeDtypeStruct((n_assign + 2 * blk, D_MODEL), F32),
        compiler_params=_cparams(("arbitrary",), row_dma=True),
        name="experts",
    )(bexp, nused, slot_tok, slot_tok, slot_dst, slot_dst, h2, w_gate_up, b_gate_up[:, :, None, :],
      w_down, b_down[:, :, None, :], jnp.asarray(perm, BF16))


def _combine_kernel(x1_ref, y4_ref, gate_ref, g_ref, lng, lnb, o_ref):
    tm = x1_ref.shape[0]

    def norm_rows(ci, carry):
        rows = pl.ds(pl.multiple_of(ci * NORM_ROWS, NORM_ROWS), NORM_ROWS)
        gate = gate_ref[rows, :]
        f = gate[:, 0:1] * y4_ref[rows, 0:D_MODEL]
        for k in range(1, TOP_K):
            f = f + gate[:, k:k + 1] * y4_ref[rows, k * D_MODEL:(k + 1) * D_MODEL]
        r = ALPHA * x1_ref[rows, :] + g_ref[0] * f
        o_ref[rows, :] = _ln_rows(r) * lng[...] + lnb[...]
        return carry

    lax.fori_loop(0, tm // NORM_ROWS, norm_rows, 0, unroll=True)


def _combine(x1, y_rows, gates, gate_mod, lng, lnb, n_lat_tiles):
    t = x1.shape[0]
    tm = ROW_TILE
    y4 = y_rows.reshape(y_rows.shape[0] // TOP_K, TOP_K * D_MODEL)
    return pl.pallas_call(
        _combine_kernel,
        grid=(t // tm,),
        in_specs=[pl.BlockSpec((tm, D_MODEL), lambda i: (i, 0)),
                  pl.BlockSpec((tm, TOP_K * D_MODEL), lambda i: (i, 0)),
                  pl.BlockSpec((tm, LANES), lambda i: (i, 0)),
                  pl.BlockSpec((1, 1, D_MODEL),
                               lambda i: (jnp.where(i >= n_lat_tiles, 1, 0), 0, 0)),
                  pl.BlockSpec((1, D_MODEL), lambda i: (0, 0)),
                  pl.BlockSpec((1, D_MODEL), lambda i: (0, 0))],
        out_specs=pl.BlockSpec((tm, D_MODEL), lambda i: (i, 0)),
        out_shape=jax.ShapeDtypeStruct((t, D_MODEL), F32),
        compiler_params=_cparams(("arbitrary",)),
        name="combine",
    )(x1, y4, gates, gate_mod, lng, lnb)


def _rope_tables(s_len, cl):
    t = jnp.arange(s_len, dtype=jnp.int32)
    quarter = HEAD_DIM // 4
    inv = ROPE_BASE ** (-jnp.arange(quarter, dtype=F32) / quarter)
    ang_r = (t // GRID_W).astype(F32)[:, None] * inv
    ang_c = (t % GRID_W).astype(F32)[:, None] * inv
    cr, sr, cc, sc = jnp.cos(ang_r), jnp.sin(ang_r), jnp.cos(ang_c), jnp.sin(ang_c)
    cos_t = jnp.concatenate([cr, cr, cc, cc], axis=1)
    sin_t = jnp.concatenate([-sr, sr, -sc, sc], axis=1)
    cos_t = jnp.concatenate([cos_t, jnp.ones((cl, HEAD_DIM), F32)], axis=0)
    sin_t = jnp.concatenate([sin_t, jnp.zeros((cl, HEAD_DIM), F32)], axis=0)
    return cos_t, sin_t


def _block_plan(counts, eid, rank, n_blocks):
    n_assign = eid.size
    n_slots = n_blocks * MOE_BLOCK
    blocks_per = (counts + MOE_BLOCK - 1) // MOE_BLOCK
    block_end = jnp.cumsum(blocks_per)
    base = (block_end - blocks_per) * MOE_BLOCK
    bexp = jnp.minimum(jnp.sum(block_end[None, :] <= jnp.arange(n_blocks)[:, None], axis=1),
                       N_EXPERTS - 1).astype(jnp.int32)
    nused = block_end[-1:].astype(jnp.int32)
    experts = jnp.arange(N_EXPERTS, dtype=jnp.int32)
    slot = rank + jnp.sum(jnp.where(eid[:, :, None] == experts, base, 0), axis=-1)
    pad_j = jnp.arange(MOE_BLOCK, dtype=jnp.int32)[None, :]
    pad_slot = jnp.where(pad_j < (blocks_per * MOE_BLOCK - counts)[:, None],
                         (base + counts)[:, None] + pad_j, n_slots)
    keys = jnp.concatenate([slot.reshape(-1), pad_slot.reshape(-1)]).astype(jnp.int32)
    vals = jnp.concatenate([jnp.arange(n_assign, dtype=jnp.int32),
                            jnp.full((N_EXPERTS * MOE_BLOCK,), -1, jnp.int32)])
    assign = lax.sort((keys, vals), num_keys=1)[1]
    is_pad = assign < 0
    slot_tok = jnp.where(is_pad, 0, assign // TOP_K)
    spill = n_assign + jnp.arange(n_slots, dtype=jnp.int32) % (2 * MOE_BLOCK)
    slot_dst = jnp.where(is_pad, spill, assign)
    return bexp, nused, slot_tok, slot_dst


def kernel(x, c, ctx, c_ctx, w_ada, b_ada, w_in, w_out, sg_w, sg_b, na_rpb, ln1_g, ln1_b,
           ln2_g, ln2_b, w_router, b_router, w_gate_up, b_gate_up, w_down, b_down):
    bsz, s_len, dm = x.shape
    cl = ctx.shape[1]
    depth = w_ada.shape[0]
    assert bsz == 1 and dm == D_MODEL and cl == ROW_TILE and MOE_BLOCK == ROW_TILE
    assert s_len % (GRID_W * KEY_ROWS) == 0 and s_len % ROW_TILE == 0
    t_all = s_len + cl
    n_lat_tiles = s_len // ROW_TILE
    rows = s_len // GRID_W

    cond = jnp.zeros((8, dm), F32).at[0].set(c[0]).at[1].set(c_ctx)
    mod = _ada_mod(cond, w_ada, b_ada)[:, :2].reshape(depth, 2, 6, 1, dm)
    cos_t, sin_t = _rope_tables(s_len, cl)
    pb, vrow = _na_bias_tables(na_rpb, rows)
    cs, m1, tw, mc = _fourier_consts(s_len, cl)
    tri = jnp.asarray(np.tril(np.ones((ROW_TILE, ROW_TILE), np.float32), -1), BF16)
    wr_pad = jnp.pad(w_router, ((0, 0), (0, 0), (0, LANES - N_EXPERTS)))
    wr_hi = wr_pad.astype(BF16)
    wr_split = jnp.stack([wr_hi, (wr_pad - wr_hi.astype(F32)).astype(BF16)], axis=1)
    br_pad = jnp.pad(b_router, ((0, 0), (0, LANES - N_EXPERTS)), constant_values=NEG_INF)
    w_in_bf = w_in.astype(BF16)
    w_out_bf = w_out.astype(BF16)
    sgw_bf = sg_w.astype(BF16)

    xall = jnp.concatenate([x[0], ctx[0]], axis=0)
    for l in range(depth):
        last = l == depth - 1
        m = lambda j: mod[l, :, j]
        qp, qr, kr, v, z_sg, z_ft = _proj(xall, m(0), m(1), w_in_bf, cos_t, sin_t, l,
                                          n_lat_tiles)
        ona_l = _na_attention(qr, qp, kr, v, pb, vrow, l, s_len)
        osg = _spatial_gating(z_sg, sgw_bf[l], sg_b[l][:, :, None])
        oft_l = _fourier_latent(z_ft, cs, m1, tw, s_len)
        if last:
            ona_c = oft_c = None
            n_rows = s_len
        else:
            ona_c = _ctx_attention(qp, kr, v, s_len, cl)
            oft_c = _fourier_ctx(z_ft, cs, mc, s_len, cl)
            n_rows = t_all
        x1, h2, logits = _out_proj(
            ona_l, ona_c, osg, oft_l, oft_c, xall, w_out_bf, m(2),
            ln1_g[l][None], ln1_b[l][None], m(3), m(4), wr_split, br_pad[l][None],
            l, n_rows, n_lat_tiles, not last)
        eid, rank, gates, counts = _route(logits, tri)
        n_blocks = -(-n_rows * TOP_K // MOE_BLOCK) + N_EXPERTS
        bexp, nused, slot_tok, slot_dst = _block_plan(
            counts[0, :N_EXPERTS], eid[:, :TOP_K], rank[:, :TOP_K], n_blocks)
        y_rows = _experts(bexp, nused, slot_tok, slot_dst, h2, w_gate_up, b_gate_up, w_down,
                          b_down, l, n_blocks)
        xall = _combine(x1, y_rows, gates, m(5), ln2_g[l][None], ln2_b[l][None], n_lat_tiles)
    return xall[None]
```

```python
import functools
import math

import numpy as np
import jax
import jax.numpy as jnp
from jax import lax
from jax.experimental import pallas as pl
from jax.experimental.pallas import tpu as pltpu

D_MODEL = 2048
DEPTH_NORM = 4
GRID_W = 64
HEAD_DIM = 128
NA_HEADS = 8
NA_WIDTH = NA_HEADS * HEAD_DIM
NA_KH = 8
NA_KW = 16
SG_GROUPS = 4
SG_WIDTH = 512
SG_CHUNK = 128
FT_GROUPS = 4
FT_WIDTH = 512
SG_OFF = 3 * NA_WIDTH
FT_OFF = SG_OFF + 2 * SG_WIDTH
IN_WIDTH = FT_OFF + FT_WIDTH
ROPE_BASE = 10000.0
N_EXPERTS = 32
TOP_K = 4
D_FF = D_MODEL // 4
SWIGLU_ALPHA = 1.702
SWIGLU_LIMIT = 7.0
LN_EPS = 1e-5
NEG_INF = -1e30
ALPHA = (2 * DEPTH_NORM) ** 0.25

LANES = 128
ROW_TILE = 256
MOE_BLOCK = 256
SLOT_BLOCKS = 4
NORM_ROWS = 32
KEY_ROWS = 16
Q_ROWS = 8
NA_PAIRS = 5
VMEM_LIMIT = 56 * 1024 * 1024

F32 = jnp.float32
BF16 = jnp.bfloat16


def _cparams(sem, row_dma=False):
    return pltpu.CompilerParams(dimension_semantics=sem, vmem_limit_bytes=VMEM_LIMIT,
                                disable_bounds_checks=row_dma)


def _dot(a, b):
    return jnp.dot(a, b, preferred_element_type=F32)


def _dot_nt(a, b):
    return lax.dot_general(a, b, (((1,), (1,)), ((), ())), preferred_element_type=F32)


def _ln_rows(x):
    mu = jnp.mean(x, axis=-1, keepdims=True)
    xc = x - mu
    var = jnp.mean(xc * xc, axis=-1, keepdims=True)
    return xc * lax.rsqrt(var + LN_EPS)


def _ada_kernel(c_ref, w_ref, b_ref, o_ref):
    c = c_ref[...]
    s = c * jax.nn.sigmoid(c)
    o_ref[0] = jnp.dot(s, w_ref[0], preferred_element_type=F32,
                       precision=lax.Precision.HIGHEST) + b_ref[0]


def _ada_mod(cond, w_ada, b_ada):
    depth, d, n = w_ada.shape
    tn = 1536
    return pl.pallas_call(
        _ada_kernel,
        grid=(depth, n // tn),
        in_specs=[pl.BlockSpec((8, d), lambda l, j: (0, 0)),
                  pl.BlockSpec((1, d, tn), lambda l, j: (l, 0, j)),
                  pl.BlockSpec((1, 1, tn), lambda l, j: (l, 0, j))],
        out_specs=pl.BlockSpec((1, 8, tn), lambda l, j: (l, 0, j)),
        out_shape=jax.ShapeDtypeStruct((depth, 8, n), F32),
        compiler_params=_cparams(("arbitrary", "arbitrary")),
        name="ada_mod",
    )(cond, w_ada, b_ada.reshape(depth, 1, n))


def _proj_kernel(x_ref, sh_ref, sc_ref, w_ref, cos_ref, sin_ref,
                 qp_ref, qr_ref, kr_ref, v_ref, sg_ref, ft_ref):
    y = _ln_rows(x_ref[...])
    h = (y * (1.0 + sc_ref[0]) + sh_ref[0]).astype(BF16)
    cos = cos_ref[...]
    sin = sin_ref[...]
    lane = lax.broadcasted_iota(jnp.int32, cos.shape, 1)
    first = (lane % 64) < 32

    def rope(z):
        swapped = jnp.where(first, pltpu.roll(z, 96, axis=1), pltpu.roll(z, 32, axis=1))
        return z * cos + swapped * sin

    nw = 512
    for j in range(IN_WIDTH // nw):
        z = _dot(h, w_ref[:, j * nw:(j + 1) * nw])
        for p in range(nw // LANES):
            col = j * nw + p * LANES
            zp = z[:, p * LANES:(p + 1) * LANES]
            if col < NA_WIDTH:
                zp = zp * (HEAD_DIM ** -0.5)
                qp_ref[:, col:col + LANES] = zp.astype(BF16)
                qr_ref[:, col:col + LANES] = rope(zp).astype(BF16)
            elif col < 2 * NA_WIDTH:
                c0 = col - NA_WIDTH
                kr_ref[:, c0:c0 + LANES] = rope(zp).astype(BF16)
            elif col < SG_OFF:
                c0 = col - 2 * NA_WIDTH
                v_ref[:, c0:c0 + LANES] = zp.astype(BF16)
            elif col < FT_OFF:
                c0 = col - SG_OFF
                sg_ref[:, c0:c0 + LANES] = zp
            else:
                c0 = col - FT_OFF
                ft_ref[:, c0:c0 + LANES] = zp.astype(BF16)


def _proj(xall, shift, scale, w_in_bf, cos_t, sin_t, layer, n_lat_tiles):
    t = xall.shape[0]
    tm = ROW_TILE
    typ = lambda i: (jnp.where(i >= n_lat_tiles, 1, 0), 0, 0)
    row = lambda i: (i, 0)
    return pl.pallas_call(
        _proj_kernel,
        grid=(t // tm,),
        in_specs=[pl.BlockSpec((tm, D_MODEL), row),
                  pl.BlockSpec((1, 1, D_MODEL), typ),
                  pl.BlockSpec((1, 1, D_MODEL), typ),
                  pl.BlockSpec((None, D_MODEL, IN_WIDTH), lambda i: (layer, 0, 0),
                               pipeline_mode=pl.Buffered(1)),
                  pl.BlockSpec((tm, LANES), row),
                  pl.BlockSpec((tm, LANES), row)],
        out_specs=[pl.BlockSpec((tm, NA_WIDTH), row)] * 4
        + [pl.BlockSpec((tm, 2 * SG_WIDTH), row), pl.BlockSpec((tm, FT_WIDTH), row)],
        out_shape=[jax.ShapeDtypeStruct((t, NA_WIDTH), BF16)] * 4
        + [jax.ShapeDtypeStruct((t, 2 * SG_WIDTH), F32),
           jax.ShapeDtypeStruct((t, FT_WIDTH), BF16)],
        compiler_params=_cparams(("arbitrary",)),
        name="ln_proj",
    )(xall, shift, scale, w_in_bf, cos_t, sin_t)


def _na_kernel(qr_ref, qp_ref, k0, k1, k2, k3, v0, v1, v2, v3, kc_ref, vc_ref, pb_ref, vrow_ref,
               o_ref, k_scr, v_scr, s_scr, p_scr, den_scr, *, n_rb):
    i = pl.program_id(1)
    is_first = i == 0
    is_last = i == n_rb - 1
    off = jnp.where(is_first, 0, jnp.where(is_last, -(KEY_ROWS - Q_ROWS), -(NA_KH // 2)))
    kb = k0.shape[0]
    for j, (kj, vj) in enumerate(((k0, v0), (k1, v1), (k2, v2), (k3, v3))):
        k_scr[j * kb:(j + 1) * kb, :] = kj[...]
        v_scr[j * kb:(j + 1) * kb, :] = vj[...]
    kc = kc_ref[...]
    vc = vc_ref[...]
    qsub = 128
    pair = 2 * GRID_W
    starts = ((0, 0, 0, 1), (0, 1, 2, 3), (2, 3, 3, 3))
    n_sub = Q_ROWS * GRID_W // qsub
    n_lat = NA_PAIRS * pair
    key0 = []
    for s in range(n_sub):
        rows = slice(s * qsub, (s + 1) * qsub)
        p0 = jnp.where(is_first, starts[0][s], jnp.where(is_last, starts[2][s], starts[1][s]))
        key0.append(pl.multiple_of(p0 * pair, pair))
        bias_rows = []
        for qr in range(s * qsub // GRID_W, (s + 1) * qsub // GRID_W):
            tiles = [pb_ref[0, 0, jnp.clip(2 * (p0 + j) - qr + off + NA_KH, 0, 2 * NA_KH - 1)]
                     + vrow_ref[0, qr, p0 + j]
                     for j in range(NA_PAIRS)]
            bias_rows.append(jnp.concatenate(tiles, axis=1))
        bias = jnp.concatenate(bias_rows, axis=0)
        s_scr[s, :, 0:n_lat] = _dot_nt(qr_ref[rows, :], k_scr[pl.ds(key0[s], n_lat), :]) + bias
        s_scr[s, :, n_lat:] = _dot_nt(qp_ref[rows, :], kc)
    for s in range(n_sub):
        sc = s_scr[s]
        p = jnp.exp(sc - jnp.max(sc, axis=-1, keepdims=True))
        p_scr[s] = p.astype(BF16)
        den_scr[s] = jnp.sum(p, axis=-1, keepdims=True)
    for s in range(n_sub):
        rows = slice(s * qsub, (s + 1) * qsub)
        o = (_dot(p_scr[s, :, 0:n_lat], v_scr[pl.ds(key0[s], n_lat), :])
             + _dot(p_scr[s, :, n_lat:], vc))
        o_ref[rows, :] = (o / den_scr[s]).astype(BF16)


def _na_attention(qr, qp, kr, v, pb, vrow, layer, s_len):
    rows = s_len // GRID_W
    n_rb = rows // Q_ROWS
    qb = Q_ROWS * GRID_W
    kb = 256
    last_kblock = (rows - KEY_ROWS) * GRID_W // kb
    ctx_block = s_len // kb

    def kmap(j):
        return lambda h, i: (jnp.clip(2 * i - 1, 0, last_kblock) + j, h)

    def btype(i):
        return jnp.where(i == 0, 0, jnp.where(i == n_rb - 1, 2, 1))

    qspec = pl.BlockSpec((qb, HEAD_DIM), lambda h, i: (i, h))
    kspecs = [pl.BlockSpec((kb, HEAD_DIM), kmap(j)) for j in range(4)]
    cspec = pl.BlockSpec((kb, HEAD_DIM), lambda h, i: (ctx_block, h))
    return pl.pallas_call(
        functools.partial(_na_kernel, n_rb=n_rb),
        grid=(NA_HEADS, n_rb),
        in_specs=[qspec, qspec] + kspecs + kspecs + [cspec, cspec]
        + [pl.BlockSpec((1, 1, 2 * NA_KH, GRID_W, 2 * GRID_W), lambda h, i: (layer, h, 0, 0, 0)),
           pl.BlockSpec((1, Q_ROWS, KEY_ROWS // 2, 1, 2 * GRID_W),
                        lambda h, i: (btype(i), 0, 0, 0, 0))],
        out_specs=pl.BlockSpec((qb, HEAD_DIM), lambda h, i: (i, h)),
        out_shape=jax.ShapeDtypeStruct((s_len, NA_WIDTH), BF16),
        scratch_shapes=[pltpu.VMEM((KEY_ROWS * GRID_W, HEAD_DIM), BF16),
                        pltpu.VMEM((KEY_ROWS * GRID_W, HEAD_DIM), BF16),
                        pltpu.VMEM((qb // 128, 128, NA_PAIRS * 2 * GRID_W + kb), F32),
                        pltpu.VMEM((qb // 128, 128, NA_PAIRS * 2 * GRID_W + kb), BF16),
                        pltpu.VMEM((qb // 128, 128, 1), F32)],
        compiler_params=_cparams(("arbitrary", "arbitrary")),
        name="na_attention",
    )(qr, qp, kr, kr, kr, kr, v, v, v, v, kr, v, pb, vrow)


def _ctx_attn_kernel(q_ref, k_ref, v_ref, o_ref):
    s = _dot_nt(q_ref[...], k_ref[...])
    m = jnp.max(s, axis=-1, keepdims=True)
    p = jnp.exp(s - m)
    den = jnp.sum(p, axis=-1, keepdims=True)
    o_ref[...] = (_dot(p.astype(BF16), v_ref[...]) / den).astype(BF16)


def _ctx_attention(qp, kr, v, s_len, cl):
    blk = s_len // cl
    spec = pl.BlockSpec((cl, HEAD_DIM), lambda h: (blk, h))
    return pl.pallas_call(
        _ctx_attn_kernel,
        grid=(NA_HEADS,),
        in_specs=[spec, spec, spec],
        out_specs=pl.BlockSpec((cl, HEAD_DIM), lambda h: (0, h)),
        out_shape=jax.ShapeDtypeStruct((cl, NA_WIDTH), BF16),
        compiler_params=_cparams(("arbitrary",)),
        name="ctx_attention",
    )(qp, kr, v)


def _na_bias_tables(na_rpb, rows):
    n_rb = rows // Q_ROWS
    rmask = np.zeros((3, Q_ROWS, KEY_ROWS), bool)
    for t, i in enumerate((0, 1, n_rb - 1)):
        ks = min(max(Q_ROWS * i - NA_KH // 2, 0), rows - KEY_ROWS)
        for qr in range(Q_ROWS):
            r = Q_ROWS * i + qr
            lo = min(max(r - NA_KH // 2, 0), rows - NA_KH)
            for kr in range(KEY_ROWS):
                rmask[t, qr, kr] = lo <= ks + kr < lo + NA_KH
    vrow = np.where(np.repeat(rmask, GRID_W, axis=2), 0.0, NEG_INF).astype(np.float32)
    csel = np.zeros((GRID_W, GRID_W, 2 * NA_KW - 1), np.float32)
    cmask = np.zeros((GRID_W, GRID_W), bool)
    for qc in range(GRID_W):
        lo = min(max(qc - NA_KW // 2, 0), GRID_W - NA_KW)
        for kc in range(GRID_W):
            cmask[qc, kc] = lo <= kc < lo + NA_KW
            csel[qc, kc, min(max(kc - qc + NA_KW - 1, 0), 2 * NA_KW - 2)] = 1.0
    b = jnp.einsum('lhab,qkb->lhaqk', na_rpb, jnp.asarray(csel), precision=lax.Precision.HIGHEST)
    b = jnp.where(jnp.asarray(cmask), b, NEG_INF)
    b = jnp.pad(b, ((0, 0), (0, 0), (1, 1), (0, 0), (0, 0)))
    pb = jnp.concatenate([b[:, :, :-1], b[:, :, 1:]], axis=-1)
    return pb, jnp.asarray(vrow).reshape(3, Q_ROWS, KEY_ROWS // 2, 1, 2 * GRID_W)


def _sg_kernel(z_ref, w_ref, b_ref, o_ref):
    z = z_ref[...]
    g = 0.5 * z * (1.0 + lax.erf(z * (2.0 ** -0.5)))
    n_chunks = z.shape[0] // SG_CHUNK
    for gi in range(SG_GROUPS):
        u = g[:, gi * LANES:(gi + 1) * LANES]
        vn = _ln_rows(g[:, SG_WIDTH + gi * LANES:SG_WIDTH + (gi + 1) * LANES]).astype(BF16)
        for n in range(n_chunks):
            rows = slice(n * SG_CHUNK, (n + 1) * SG_CHUNK)
            t = _dot(w_ref[gi], vn[rows, :]) + b_ref[gi]
            o_ref[rows, gi * LANES:(gi + 1) * LANES] = (u[rows, :] * t).astype(BF16)


def _spatial_gating(z_sg, sgw_bf, sgb):
    t = z_sg.shape[0]
    tm = ROW_TILE
    return pl.pallas_call(
        _sg_kernel,
        grid=(t // tm,),
        in_specs=[pl.BlockSpec((tm, 2 * SG_WIDTH), lambda i: (i, 0)),
                  pl.BlockSpec((SG_GROUPS, SG_CHUNK, SG_CHUNK), lambda i: (0, 0, 0)),
                  pl.BlockSpec((SG_GROUPS, SG_CHUNK, 1), lambda i: (0, 0, 0))],
        out_specs=pl.BlockSpec((tm, SG_WIDTH), lambda i: (i, 0)),
        out_shape=jax.ShapeDtypeStruct((t, SG_WIDTH), BF16),
        compiler_params=_cparams(("arbitrary",)),
        name="spatial_gating",
    )(z_sg, sgw_bf, sgb)


def _ft1_kernel(z_ref, cs_ref, m1_ref, y_ref):
    n1 = z_ref.shape[0]
    parts = []
    for j in range(z_ref.shape[1] // LANES):
        ab = _dot(z_ref[:, j * LANES:(j + 1) * LANES], cs_ref[...])
        parts.append(jnp.concatenate([ab[:, :LANES], ab[:, LANES:]], axis=0))
    st = jnp.concatenate(parts, axis=1).astype(BF16)
    y = _dot(m1_ref[...], st)
    y_ref[0] = y[:n1].astype(BF16)
    y_ref[1] = y[n1:].astype(BF16)


def _ft2_kernel(y_ref, tw_ref, o_ref, *, scale):
    for j in range(y_ref.shape[1]):
        yy = jnp.concatenate([y_ref[0, j], y_ref[1, j]], axis=0)
        o = _dot(tw_ref[j], yy) * scale
        o_ref[:, j * FT_WIDTH:(j + 1) * FT_WIDTH] = o.astype(BF16)


def _fourier_consts(s_len, cl):
    n2 = 128
    n1 = s_len // n2
    c = np.arange(HEAD_DIM)
    ang = 2 * np.pi * ((c[:, None] * c[None, :]) % HEAD_DIM) / HEAD_DIM
    cs = np.concatenate([np.cos(ang), np.sin(ang)], axis=1)
    a = np.arange(n1)
    ang1 = 2 * np.pi * ((a[:, None] * a[None, :]) % n1) / n1
    fc, fs = np.cos(ang1), np.sin(ang1)
    m1 = np.block([[fc, -fs], [-fs, -fc]])
    ka = jnp.arange(n1, dtype=jnp.int32)[:, None, None]
    kb = jnp.arange(n2, dtype=jnp.int32)[None, :, None]
    nn = jnp.arange(n2, dtype=jnp.int32)[None, None, :]
    ph = (nn * (ka + n1 * kb)) % s_len
    th = ph.astype(F32) * (2 * np.pi / s_len)
    tw = jnp.concatenate([jnp.cos(th), jnp.sin(th)], axis=2).astype(BF16)
    p = np.arange(cl)
    angc = 2 * np.pi * ((p[:, None] * p[None, :]) % cl) / cl
    mc = np.concatenate([np.cos(angc), -np.sin(angc)], axis=1)
    return (jnp.asarray(cs, BF16), jnp.asarray(m1, BF16), tw, jnp.asarray(mc, BF16))


def _fourier_latent(z_ft, cs, m1, tw, s_len):
    n2 = 128
    n1 = s_len // n2
    width = n2 * FT_WIDTH
    cw = 2048
    y = pl.pallas_call(
        _ft1_kernel,
        grid=(width // cw,),
        in_specs=[pl.BlockSpec((n1, cw), lambda j: (0, j)),
                  pl.BlockSpec((HEAD_DIM, 2 * HEAD_DIM), lambda j: (0, 0)),
                  pl.BlockSpec((2 * n1, 2 * n1), lambda j: (0, 0))],
        out_specs=pl.BlockSpec((2, n1, cw), lambda j: (0, 0, j)),
        out_shape=jax.ShapeDtypeStruct((2, n1, width), BF16),
        compiler_params=_cparams(("arbitrary",)),
        name="fourier_stage1",
    )(z_ft[:s_len].reshape(n1, width), cs, m1)
    kab = 4
    out = pl.pallas_call(
        functools.partial(_ft2_kernel, scale=(s_len * HEAD_DIM) ** -0.5),
        grid=(n1 // kab,),
        in_specs=[pl.BlockSpec((2, kab, n2, FT_WIDTH), lambda j: (0, j, 0, 0)),
                  pl.BlockSpec((kab, n2, 2 * n2), lambda j: (j, 0, 0))],
        out_specs=pl.BlockSpec((n2, kab * FT_WIDTH), lambda j: (0, j)),
        out_shape=jax.ShapeDtypeStruct((n2, n1 * FT_WIDTH), BF16),
        compiler_params=_cparams(("arbitrary",)),
        name="fourier_stage2",
    )(y.reshape(2, n1, n2, FT_WIDTH), tw)
    return out.reshape(s_len, FT_WIDTH)


def _ft_ctx_kernel(z_ref, cs_ref, mc_ref, o_ref, *, scale):
    for g in range(FT_GROUPS):
        ab = _dot(z_ref[:, g * LANES:(g + 1) * LANES], cs_ref[...])
        st = jnp.concatenate([ab[:, :LANES], ab[:, LANES:]], axis=0).astype(BF16)
        o_ref[:, g * LANES:(g + 1) * LANES] = (_dot(mc_ref[...], st) * scale).astype(BF16)


def _fourier_ctx(z_ft, cs, mc, s_len, cl):
    blk = s_len // cl
    return pl.pallas_call(
        functools.partial(_ft_ctx_kernel, scale=(cl * HEAD_DIM) ** -0.5),
        grid=(1,),
        in_specs=[pl.BlockSpec((cl, FT_WIDTH), lambda i: (blk, 0)),
                  pl.BlockSpec((HEAD_DIM, 2 * HEAD_DIM), lambda i: (0, 0)),
                  pl.BlockSpec((cl, 2 * cl), lambda i: (0, 0))],
        out_specs=pl.BlockSpec((cl, FT_WIDTH), lambda i: (0, 0)),
        out_shape=jax.ShapeDtypeStruct((cl, FT_WIDTH), BF16),
        compiler_params=_cparams(("arbitrary",)),
        name="fourier_ctx",
    )(z_ft, cs, mc)


def _out_kernel(*refs, with_ctx, n_lat_tiles):
    if with_ctx:
        (ona_l, ona_c, osg, oft_l, oft_c, x_ref, w_ref, g_ref, lng, lnb, sh_ref, sc_ref,
         wr_ref, br_ref, x1_ref, h2_ref, lg_ref, r_scr, hs_scr) = refs
        is_ctx = pl.program_id(0) >= n_lat_tiles
        ona = jnp.where(is_ctx, ona_c[...], ona_l[...])
        oft = jnp.where(is_ctx, oft_c[...], oft_l[...])
    else:
        (ona_l, osg, oft_l, x_ref, w_ref, g_ref, lng, lnb, sh_ref, sc_ref,
         wr_ref, br_ref, x1_ref, h2_ref, lg_ref, r_scr, hs_scr) = refs
        ona = ona_l[...]
        oft = oft_l[...]
    sg = osg[...]
    nw = 512
    for n in range(D_MODEL // nw):
        cols = slice(n * nw, (n + 1) * nw)
        y = (_dot(ona, w_ref[0:NA_WIDTH, cols])
             + _dot(sg, w_ref[NA_WIDTH:NA_WIDTH + SG_WIDTH, cols])
             + _dot(oft, w_ref[NA_WIDTH + SG_WIDTH:, cols]))
        r_scr[:, cols] = ALPHA * x_ref[:, cols] + g_ref[0][:, cols] * y

    def norm_rows(ci, carry):
        rows = pl.ds(pl.multiple_of(ci * NORM_ROWS, NORM_ROWS), NORM_ROWS)
        x1 = _ln_rows(r_scr[rows, :]) * lng[...] + lnb[...]
        x1_ref[rows, :] = x1
        h2 = _ln_rows(x1) * (1.0 + sc_ref[0]) + sh_ref[0]
        h2_ref[rows, :] = h2
        hi = h2.astype(BF16)
        hs_scr[0, rows, :] = hi
        hs_scr[1, rows, :] = (h2 - hi.astype(F32)).astype(BF16)
        return carry

    lax.fori_loop(0, r_scr.shape[0] // NORM_ROWS, norm_rows, 0, unroll=True)
    lg_ref[...] = (_dot(hs_scr[0], wr_ref[0]) + _dot(hs_scr[1], wr_ref[0])
                   + _dot(hs_scr[0], wr_ref[1]) + br_ref[...])


def _out_proj(ona_l, ona_c, osg, oft_l, oft_c, xall, w_out_bf, gate, lng, lnb, shift, scale,
              wr_split, br_pad, layer, n_rows, n_lat_tiles, with_ctx):
    tm = ROW_TILE
    row = lambda i: (i, 0)
    lat = lambda i: (jnp.minimum(i, n_lat_tiles - 1), 0)
    typ = lambda i: (jnp.where(i >= n_lat_tiles, 1, 0), 0, 0)
    const = lambda i: (0, 0)
    modspec = pl.BlockSpec((1, 1, D_MODEL), typ)
    vec = pl.BlockSpec((1, D_MODEL), const)
    specs = [pl.BlockSpec((tm, NA_WIDTH), lat)]
    args = [ona_l]
    if with_ctx:
        specs.append(pl.BlockSpec((tm, NA_WIDTH), const))
        args.append(ona_c)
    specs.append(pl.BlockSpec((tm, SG_WIDTH), row))
    args.append(osg)
    specs.append(pl.BlockSpec((tm, FT_WIDTH), lat))
    args.append(oft_l)
    if with_ctx:
        specs.append(pl.BlockSpec((tm, FT_WIDTH), const))
        args.append(oft_c)
    specs += [pl.BlockSpec((tm, D_MODEL), row),
              pl.BlockSpec((None, D_MODEL, D_MODEL), lambda i: (layer, 0, 0),
                           pipeline_mode=pl.Buffered(1)),
              modspec, vec, vec, modspec, modspec,
              pl.BlockSpec((None, 2, D_MODEL, LANES), lambda i: (layer, 0, 0, 0)),
              pl.BlockSpec((1, LANES), const)]
    args += [xall, w_out_bf, gate, lng, lnb, shift, scale, wr_split, br_pad]
    return pl.pallas_call(
        functools.partial(_out_kernel, with_ctx=with_ctx, n_lat_tiles=n_lat_tiles),
        grid=(n_rows // tm,),
        in_specs=specs,
        out_specs=[pl.BlockSpec((tm, D_MODEL), row),
                   pl.BlockSpec((tm, D_MODEL), row),
                   pl.BlockSpec((tm, LANES), row)],
        out_shape=[jax.ShapeDtypeStruct((n_rows, D_MODEL), F32),
                   jax.ShapeDtypeStruct((n_rows, D_MODEL), F32),
                   jax.ShapeDtypeStruct((n_rows, LANES), F32)],
        scratch_shapes=[pltpu.VMEM((tm, D_MODEL), F32),
                        pltpu.VMEM((2, tm, D_MODEL), BF16)],
        compiler_params=_cparams(("arbitrary",)),
        name="out_proj",
    )(*args)


def _route_kernel(lg_ref, tri_ref, eid_ref, rank_ref, gate_ref, cnt_ref, carry):
    @pl.when(pl.program_id(0) == 0)
    def _():
        carry[...] = jnp.zeros_like(carry)

    work = lg_ref[...]
    lane = lax.broadcasted_iota(jnp.int32, work.shape, 1)
    lane_f = lane.astype(F32)
    vals, ids, hots = [], [], []
    for _ in range(TOP_K):
        m = jnp.max(work, axis=-1, keepdims=True)
        idx = jnp.min(jnp.where(work == m, lane_f, float(LANES)), axis=-1, keepdims=True)
        hot = lane_f == idx
        vals.append(m)
        ids.append(idx)
        hots.append(hot)
        work = jnp.where(hot, -jnp.inf, work)
    exps = [jnp.exp(v - vals[0]) for v in vals]
    den = exps[0] + exps[1] + exps[2] + exps[3]
    multi = jnp.zeros(work.shape, F32)
    for hot in hots:
        multi = multi + hot.astype(F32)
    pref = _dot(tri_ref[...], multi.astype(BF16)) + carry[...]
    eid = jnp.zeros(work.shape, F32)
    rank = jnp.zeros(work.shape, F32)
    gate = jnp.zeros(work.shape, F32)
    for k in range(TOP_K):
        rk = jnp.sum(jnp.where(hots[k], pref, 0.0), axis=-1, keepdims=True)
        sel = lane == k
        eid = jnp.where(sel, ids[k], eid)
        rank = jnp.where(sel, rk, rank)
        gate = jnp.where(sel, exps[k] / den, gate)
    eid_ref[...] = eid.astype(jnp.int32)
    rank_ref[...] = rank.astype(jnp.int32)
    gate_ref[...] = gate
    carry[...] = carry[...] + jnp.sum(multi, axis=0, keepdims=True)
    cnt_ref[...] = jnp.broadcast_to(carry[...], cnt_ref.shape).astype(jnp.int32)


def _route(logits, tri):
    t = logits.shape[0]
    tm = ROW_TILE
    row = lambda i: (i, 0)
    return pl.pallas_call(
        _route_kernel,
        grid=(t // tm,),
        in_specs=[pl.BlockSpec((tm, LANES), row),
                  pl.BlockSpec((tm, tm), lambda i: (0, 0))],
        out_specs=[pl.BlockSpec((tm, LANES), row)] * 3
        + [pl.BlockSpec((8, LANES), lambda i: (0, 0))],
        out_shape=[jax.ShapeDtypeStruct((t, LANES), jnp.int32)] * 2
        + [jax.ShapeDtypeStruct((t, LANES), F32),
           jax.ShapeDtypeStruct((8, LANES), jnp.int32)],
        scratch_shapes=[pltpu.VMEM((1, LANES), F32)],
        compiler_params=_cparams(("arbitrary",)),
        name="route",
    )(logits, tri)


def _expert_kernel(bexp_ref, nused_ref, tok_ref, tok_next_ref, h2_ref, wgu_ref, bgu_ref, wd_ref,
                   bd_ref, perm_ref, y_ref, wgu_s, wd_s, xb0, xb1, gsem):
    b = pl.program_id(0)
    n_used = nused_ref[0]
    active = b < n_used
    xbufs = (xb0, xb1)
    blk = xb0.shape[0]

    def gather(idx_ref, block, p):
        off = (block % SLOT_BLOCKS) * blk
        for r in range(blk):
            pltpu.make_async_copy(h2_ref.at[pl.ds(idx_ref[off + r], 1)],
                                  xbufs[p].at[pl.ds(r, 1)], gsem.at[p]).start()

    def gather_wait(p):
        pltpu.make_async_copy(h2_ref.at[pl.ds(0, blk)], xbufs[p], gsem.at[p]).wait()

    @pl.when(b == 0)
    def _():
        gather(tok_ref, b, 0)

    @pl.when(jnp.logical_not(active))
    def _():
        y_ref[...] = jnp.zeros_like(y_ref)

    fresh = jnp.logical_or(b == 0, bexp_ref[b] != bexp_ref[jnp.maximum(b - 1, 0)])

    @pl.when(jnp.logical_and(active, fresh))
    def _():
        wgu_s[...] = wgu_ref[...].astype(BF16)
        wd_s[...] = _dot(perm_ref[...], wd_ref[...].astype(BF16)).astype(BF16)

    def compute(p):
        gu = _dot(xbufs[p][...].astype(BF16), wgu_s[...]) + bgu_ref[...]
        lane = lax.broadcasted_iota(jnp.int32, (blk, LANES), 1)
        even = (lane % 2) == 0
        prods = []
        for c in range(2 * D_FF // LANES):
            guc = gu[:, c * LANES:(c + 1) * LANES]
            glu = jnp.minimum(guc, SWIGLU_LIMIT)
            lin = jnp.clip(guc, -SWIGLU_LIMIT, SWIGLU_LIMIT) + 1.0
            prods.append(glu * jax.nn.sigmoid(SWIGLU_ALPHA * glu)
                         * pltpu.roll(lin, LANES - 1, axis=1))
        merged = [jnp.where(even, prods[2 * m], pltpu.roll(prods[2 * m + 1], 1, axis=1))
                  for m in range(D_FF // LANES)]
        act = jnp.concatenate(merged, axis=1).astype(BF16)
        y_ref[...] = _dot(act, wd_s[...]) + bd_ref[...]

    nxt = jnp.minimum(b + 1, n_used - 1)
    for p in range(2):
        @pl.when(jnp.logical_and(active, b % 2 == p))
        def _():
            gather_wait(p)
            gather(tok_next_ref, nxt, 1 - p)
            compute(p)

            @pl.when(b == n_used - 1)
            def _():
                gather_wait(1 - p)


def _experts(bexp, nused, slot_tok, h2, w_gate_up, b_gate_up, w_down, b_down, layer, n_blocks):
    blk = MOE_BLOCK

    def emap(b, be, nu):
        return (layer, be[jnp.minimum(b, nu[0] - 1)], 0, 0)

    def slots(shift):
        def index(b, be, nu):
            return (jnp.clip(b + shift, 0, nu[0] - 1) // SLOT_BLOCKS,)
        return pl.BlockSpec((SLOT_BLOCKS * blk,), index, memory_space=pltpu.SMEM)

    gs = pltpu.PrefetchScalarGridSpec(
        num_scalar_prefetch=2,
        grid=(n_blocks,),
        in_specs=[slots(0), slots(1),
                  pl.BlockSpec(memory_space=pl.ANY),
                  pl.BlockSpec((None, None, D_MODEL, 2 * D_FF), emap),
                  pl.BlockSpec((None, None, 1, 2 * D_FF), emap),
                  pl.BlockSpec((None, None, D_FF, D_MODEL), emap),
                  pl.BlockSpec((None, None, 1, D_MODEL), emap),
                  pl.BlockSpec((D_FF, D_FF), lambda b, be, nu: (0, 0))],
        out_specs=pl.BlockSpec((blk, D_MODEL), lambda b, be, nu: (b, 0)),
        scratch_shapes=[pltpu.VMEM((D_MODEL, 2 * D_FF), BF16),
                        pltpu.VMEM((D_FF, D_MODEL), BF16),
                        pltpu.VMEM((blk, D_MODEL), F32), pltpu.VMEM((blk, D_MODEL), F32),
                        pltpu.SemaphoreType.DMA((2,))],
    )
    lane = np.arange(D_FF) % LANES
    unit = (np.arange(D_FF) // LANES) * LANES + lane // 2 + (lane % 2) * (LANES // 2)
    perm = np.zeros((D_FF, D_FF), np.float32)
    perm[np.arange(D_FF), unit] = 1.0
    return pl.pallas_call(
        _expert_kernel,
        grid_spec=gs,
        out_shape=jax.ShapeDtypeStruct((n_blocks * blk, D_MODEL), F32),
        compiler_params=_cparams(("arbitrary",), row_dma=True),
        name="experts",
    )(bexp, nused, slot_tok, slot_tok, h2, w_gate_up, b_gate_up[:, :, None, :],
      w_down, b_down[:, :, None, :], jnp.asarray(perm, BF16))


def _combine_kernel(x1_ref, slot_ref, slot_next_ref, gate_ref, yb_ref, g_ref, lng, lnb,
                    o_ref, gbuf, sems):
    tm = x1_ref.shape[0]
    i = pl.program_id(0)
    cur = i % 2

    def row_copy(src_row, buf, k, t):
        return pltpu.make_async_copy(yb_ref.at[pl.ds(src_row, 1)],
                                     gbuf.at[buf, k, pl.ds(t, 1)], sems.at[buf])

    def gather(slots, buf):
        def per_token(g, carry):
            for u in range(2):
                t = g * 2 + u
                for k in range(TOP_K):
                    row_copy(slots[t * TOP_K + k], buf, k, t).start()
            return carry
        lax.fori_loop(0, tm // 2, per_token, 0)

    @pl.when(i == 0)
    def _():
        gather(slot_ref, 0)

    @pl.when(i + 1 < pl.num_programs(0))
    def _():
        gather(slot_next_ref, 1 - cur)

    for k in range(TOP_K):
        pltpu.make_async_copy(yb_ref.at[pl.ds(0, tm)], gbuf.at[cur, k], sems.at[cur]).wait()

    def norm_rows(ci, carry):
        rows = pl.ds(pl.multiple_of(ci * NORM_ROWS, NORM_ROWS), NORM_ROWS)
        gate = gate_ref[rows, :]
        f = gate[:, 0:1] * gbuf[cur, 0, rows, :]
        for k in range(1, TOP_K):
            f = f + gate[:, k:k + 1] * gbuf[cur, k, rows, :]
        r = ALPHA * x1_ref[rows, :] + g_ref[0] * f
        o_ref[rows, :] = _ln_rows(r) * lng[...] + lnb[...]
        return carry

    lax.fori_loop(0, tm // NORM_ROWS, norm_rows, 0, unroll=True)


def _combine(x1, slot_flat, gates, yb, gate_mod, lng, lnb, n_lat_tiles):
    t = x1.shape[0]
    tm = ROW_TILE
    n_tiles = t // tm
    return pl.pallas_call(
        _combine_kernel,
        grid=(n_tiles,),
        in_specs=[pl.BlockSpec((tm, D_MODEL), lambda i: (i, 0)),
                  pl.BlockSpec((tm * TOP_K,), lambda i: (i,), memory_space=pltpu.SMEM),
                  pl.BlockSpec((tm * TOP_K,), lambda i: (jnp.minimum(i + 1, n_tiles - 1),),
                               memory_space=pltpu.SMEM),
                  pl.BlockSpec((tm, LANES), lambda i: (i, 0)),
                  pl.BlockSpec(memory_space=pl.ANY),
                  pl.BlockSpec((1, 1, D_MODEL),
                               lambda i: (jnp.where(i >= n_lat_tiles, 1, 0), 0, 0)),
                  pl.BlockSpec((1, D_MODEL), lambda i: (0, 0)),
                  pl.BlockSpec((1, D_MODEL), lambda i: (0, 0))],
        out_specs=pl.BlockSpec((tm, D_MODEL), lambda i: (i, 0)),
        out_shape=jax.ShapeDtypeStruct((t, D_MODEL), F32),
        scratch_shapes=[pltpu.VMEM((2, TOP_K, tm, D_MODEL), F32),
                        pltpu.SemaphoreType.DMA((2,))],
        compiler_params=_cparams(("arbitrary",), row_dma=True),
        name="combine",
    )(x1, slot_flat, slot_flat, gates, yb, gate_mod, lng, lnb)


def _rope_tables(s_len, cl):
    t = jnp.arange(s_len, dtype=jnp.int32)
    quarter = HEAD_DIM // 4
    inv = ROPE_BASE ** (-jnp.arange(quarter, dtype=F32) / quarter)
    ang_r = (t // GRID_W).astype(F32)[:, None] * inv
    ang_c = (t % GRID_W).astype(F32)[:, None] * inv
    cr, sr, cc, sc = jnp.cos(ang_r), jnp.sin(ang_r), jnp.cos(ang_c), jnp.sin(ang_c)
    cos_t = jnp.concatenate([cr, cr, cc, cc], axis=1)
    sin_t = jnp.concatenate([-sr, sr, -sc, sc], axis=1)
    cos_t = jnp.concatenate([cos_t, jnp.ones((cl, HEAD_DIM), F32)], axis=0)
    sin_t = jnp.concatenate([sin_t, jnp.zeros((cl, HEAD_DIM), F32)], axis=0)
    return cos_t, sin_t


def _block_plan(counts, eid, rank, n_blocks):
    n_assign = eid.size
    n_slots = n_blocks * MOE_BLOCK
    blocks_per = (counts + MOE_BLOCK - 1) // MOE_BLOCK
    block_end = jnp.cumsum(blocks_per)
    base = (block_end - blocks_per) * MOE_BLOCK
    bexp = jnp.minimum(jnp.sum(block_end[None, :] <= jnp.arange(n_blocks)[:, None], axis=1),
                       N_EXPERTS - 1).astype(jnp.int32)
    nused = block_end[-1:].astype(jnp.int32)
    experts = jnp.arange(N_EXPERTS, dtype=jnp.int32)
    slot = rank + jnp.sum(jnp.where(eid[:, :, None] == experts, base, 0), axis=-1)
    pad_j = jnp.arange(MOE_BLOCK, dtype=jnp.int32)[None, :]
    pad_slot = jnp.where(pad_j < (blocks_per * MOE_BLOCK - counts)[:, None],
                         (base + counts)[:, None] + pad_j, n_slots)
    keys = jnp.concatenate([slot.reshape(-1), pad_slot.reshape(-1)]).astype(jnp.int32)
    vals = jnp.concatenate([jnp.arange(n_assign, dtype=jnp.int32),
                            jnp.full((N_EXPERTS * MOE_BLOCK,), -1, jnp.int32)])
    assign = lax.sort((keys, vals), num_keys=1)[1]
    slot_tok = jnp.where(assign < 0, 0, assign // TOP_K)
    return bexp, nused, slot.reshape(-1).astype(jnp.int32), slot_tok


def kernel(x, c, ctx, c_ctx, w_ada, b_ada, w_in, w_out, sg_w, sg_b, na_rpb, ln1_g, ln1_b,
           ln2_g, ln2_b, w_router, b_router, w_gate_up, b_gate_up, w_down, b_down):
    bsz, s_len, dm = x.shape
    cl = ctx.shape[1]
    depth = w_ada.shape[0]
    assert bsz == 1 and dm == D_MODEL and cl == ROW_TILE and MOE_BLOCK == ROW_TILE
    assert s_len % (GRID_W * KEY_ROWS) == 0 and s_len % ROW_TILE == 0
    t_all = s_len + cl
    n_lat_tiles = s_len // ROW_TILE
    rows = s_len // GRID_W

    cond = jnp.zeros((8, dm), F32).at[0].set(c[0]).at[1].set(c_ctx)
    mod = _ada_mod(cond, w_ada, b_ada)[:, :2].reshape(depth, 2, 6, 1, dm)
    cos_t, sin_t = _rope_tables(s_len, cl)
    pb, vrow = _na_bias_tables(na_rpb, rows)
    cs, m1, tw, mc = _fourier_consts(s_len, cl)
    tri = jnp.asarray(np.tril(np.ones((ROW_TILE, ROW_TILE), np.float32), -1), BF16)
    wr_pad = jnp.pad(w_router, ((0, 0), (0, 0), (0, LANES - N_EXPERTS)))
    wr_hi = wr_pad.astype(BF16)
    wr_split = jnp.stack([wr_hi, (wr_pad - wr_hi.astype(F32)).astype(BF16)], axis=1)
    br_pad = jnp.pad(b_router, ((0, 0), (0, LANES - N_EXPERTS)), constant_values=NEG_INF)
    w_in_bf = w_in.astype(BF16)
    w_out_bf = w_out.astype(BF16)
    sgw_bf = sg_w.astype(BF16)

    xall = jnp.concatenate([x[0], ctx[0]], axis=0)
    for l in range(depth):
        last = l == depth - 1
        m = lambda j: mod[l, :, j]
        qp, qr, kr, v, z_sg, z_ft = _proj(xall, m(0), m(1), w_in_bf, cos_t, sin_t, l,
                                          n_lat_tiles)
        ona_l = _na_attention(qr, qp, kr, v, pb, vrow, l, s_len)
        osg = _spatial_gating(z_sg, sgw_bf[l], sg_b[l][:, :, None])
        oft_l = _fourier_latent(z_ft, cs, m1, tw, s_len)
        if last:
            ona_c = oft_c = None
            n_rows = s_len
        else:
            ona_c = _ctx_attention(qp, kr, v, s_len, cl)
            oft_c = _fourier_ctx(z_ft, cs, mc, s_len, cl)
            n_rows = t_all
        x1, h2, logits = _out_proj(
            ona_l, ona_c, osg, oft_l, oft_c, xall, w_out_bf, m(2),
            ln1_g[l][None], ln1_b[l][None], m(3), m(4), wr_split, br_pad[l][None],
            l, n_rows, n_lat_tiles, not last)
        eid, rank, gates, counts = _route(logits, tri)
        n_blocks = -(-n_rows * TOP_K // MOE_BLOCK) + N_EXPERTS
        bexp, nused, slot_flat, slot_tok = _block_plan(
            counts[0, :N_EXPERTS], eid[:, :TOP_K], rank[:, :TOP_K], n_blocks)
        yb = _experts(bexp, nused, slot_tok, h2, w_gate_up, b_gate_up, w_down, b_down, l,
                      n_blocks)
        xall = _combine(x1, slot_flat, gates, yb, m(5), ln2_g[l][None], ln2_b[l][None],
                        n_lat_tiles)
    return xall[None]
```

```python
import functools
import math

import numpy as np
import jax
import jax.numpy as jnp
from jax import lax
from jax.experimental import pallas as pl
from jax.experimental.pallas import tpu as pltpu

D_MODEL = 2048
DEPTH_NORM = 4
GRID_W = 64
HEAD_DIM = 128
NA_HEADS = 8
NA_WIDTH = NA_HEADS * HEAD_DIM
NA_KH = 8
NA_KW = 16
SG_GROUPS = 4
SG_WIDTH = 512
SG_CHUNK = 128
FT_GROUPS = 4
FT_WIDTH = 512
SG_OFF = 3 * NA_WIDTH
FT_OFF = SG_OFF + 2 * SG_WIDTH
IN_WIDTH = FT_OFF + FT_WIDTH
ROPE_BASE = 10000.0
N_EXPERTS = 32
TOP_K = 4
D_FF = D_MODEL // 4
SWIGLU_ALPHA = 1.702
SWIGLU_LIMIT = 7.0
LN_EPS = 1e-5
NEG_INF = -1e30
ALPHA = (2 * DEPTH_NORM) ** 0.25

LANES = 128
ROW_TILE = 256
MOE_BLOCK = 256
SLOT_BLOCKS = 4
NORM_ROWS = 32
KEY_ROWS = 16
Q_ROWS = 8
NA_PAIRS = 5
VMEM_LIMIT = 56 * 1024 * 1024

F32 = jnp.float32
BF16 = jnp.bfloat16


def _cparams(sem, row_dma=False):
    return pltpu.CompilerParams(dimension_semantics=sem, vmem_limit_bytes=VMEM_LIMIT,
                                disable_bounds_checks=row_dma)


def _dot(a, b):
    return jnp.dot(a, b, preferred_element_type=F32)


def _dot_nt(a, b):
    return lax.dot_general(a, b, (((1,), (1,)), ((), ())), preferred_element_type=F32)


def _ln_rows(x):
    mu = jnp.mean(x, axis=-1, keepdims=True)
    xc = x - mu
    var = jnp.mean(xc * xc, axis=-1, keepdims=True)
    return xc * lax.rsqrt(var + LN_EPS)


def _ada_kernel(c_ref, w_ref, b_ref, o_ref):
    c = c_ref[...]
    s = c * jax.nn.sigmoid(c)
    o_ref[0] = jnp.dot(s, w_ref[0], preferred_element_type=F32,
                       precision=lax.Precision.HIGHEST) + b_ref[0]


def _ada_mod(cond, w_ada, b_ada):
    depth, d, n = w_ada.shape
    tn = 1536
    return pl.pallas_call(
        _ada_kernel,
        grid=(depth, n // tn),
        in_specs=[pl.BlockSpec((8, d), lambda l, j: (0, 0)),
                  pl.BlockSpec((1, d, tn), lambda l, j: (l, 0, j)),
                  pl.BlockSpec((1, 1, tn), lambda l, j: (l, 0, j))],
        out_specs=pl.BlockSpec((1, 8, tn), lambda l, j: (l, 0, j)),
        out_shape=jax.ShapeDtypeStruct((depth, 8, n), F32),
        compiler_params=_cparams(("arbitrary", "arbitrary")),
        name="ada_mod",
    )(cond, w_ada, b_ada.reshape(depth, 1, n))


def _proj_kernel(x_ref, sh_ref, sc_ref, w_ref, cos_ref, sin_ref,
                 qp_ref, qr_ref, kr_ref, v_ref, sg_ref, ft_ref):
    y = _ln_rows(x_ref[...])
    h = (y * (1.0 + sc_ref[0]) + sh_ref[0]).astype(BF16)
    cos = cos_ref[...]
    sin = sin_ref[...]
    lane = lax.broadcasted_iota(jnp.int32, cos.shape, 1)
    first = (lane % 64) < 32

    def rope(z):
        swapped = jnp.where(first, pltpu.roll(z, 96, axis=1), pltpu.roll(z, 32, axis=1))
        return z * cos + swapped * sin

    nw = 512
    for j in range(IN_WIDTH // nw):
        z = _dot(h, w_ref[:, j * nw:(j + 1) * nw])
        for p in range(nw // LANES):
            col = j * nw + p * LANES
            zp = z[:, p * LANES:(p + 1) * LANES]
            if col < NA_WIDTH:
                zp = zp * (HEAD_DIM ** -0.5)
                qp_ref[:, col:col + LANES] = zp.astype(BF16)
                qr_ref[:, col:col + LANES] = rope(zp).astype(BF16)
            elif col < 2 * NA_WIDTH:
                c0 = col - NA_WIDTH
                kr_ref[:, c0:c0 + LANES] = rope(zp).astype(BF16)
            elif col < SG_OFF:
                c0 = col - 2 * NA_WIDTH
                v_ref[:, c0:c0 + LANES] = zp.astype(BF16)
            elif col < FT_OFF:
                c0 = col - SG_OFF
                sg_ref[:, c0:c0 + LANES] = zp
            else:
                c0 = col - FT_OFF
                ft_ref[:, c0:c0 + LANES] = zp.astype(BF16)


def _proj(xall, shift, scale, w_in_bf, cos_t, sin_t, layer, n_lat_tiles):
    t = xall.shape[0]
    tm = ROW_TILE
    typ = lambda i: (jnp.where(i >= n_lat_tiles, 1, 0), 0, 0)
    row = lambda i: (i, 0)
    return pl.pallas_call(
        _proj_kernel,
        grid=(t // tm,),
        in_specs=[pl.BlockSpec((tm, D_MODEL), row),
                  pl.BlockSpec((1, 1, D_MODEL), typ),
                  pl.BlockSpec((1, 1, D_MODEL), typ),
                  pl.BlockSpec((None, D_MODEL, IN_WIDTH), lambda i: (layer, 0, 0),
                               pipeline_mode=pl.Buffered(1)),
                  pl.BlockSpec((tm, LANES), row),
                  pl.BlockSpec((tm, LANES), row)],
        out_specs=[pl.BlockSpec((tm, NA_WIDTH), row)] * 4
        + [pl.BlockSpec((tm, 2 * SG_WIDTH), row), pl.BlockSpec((tm, FT_WIDTH), row)],
        out_shape=[jax.ShapeDtypeStruct((t, NA_WIDTH), BF16)] * 4
        + [jax.ShapeDtypeStruct((t, 2 * SG_WIDTH), F32),
           jax.ShapeDtypeStruct((t, FT_WIDTH), BF16)],
        compiler_params=_cparams(("arbitrary",)),
        name="ln_proj",
    )(xall, shift, scale, w_in_bf, cos_t, sin_t)


def _na_kernel(qr_ref, qp_ref, k0, k1, k2, k3, v0, v1, v2, v3, kc_ref, vc_ref, pb_ref, vrow_ref,
               o_ref, k_scr, v_scr, s_scr, p_scr, den_scr, *, n_rb):
    i = pl.program_id(1)
    is_first = i == 0
    is_last = i == n_rb - 1
    off = jnp.where(is_first, 0, jnp.where(is_last, -(KEY_ROWS - Q_ROWS), -(NA_KH // 2)))
    kb = k0.shape[0]
    for j, (kj, vj) in enumerate(((k0, v0), (k1, v1), (k2, v2), (k3, v3))):
        k_scr[j * kb:(j + 1) * kb, :] = kj[...]
        v_scr[j * kb:(j + 1) * kb, :] = vj[...]
    kc = kc_ref[...]
    vc = vc_ref[...]
    qsub = 128
    pair = 2 * GRID_W
    starts = ((0, 0, 0, 1), (0, 1, 2, 3), (2, 3, 3, 3))
    n_sub = Q_ROWS * GRID_W // qsub
    n_lat = NA_PAIRS * pair
    key0 = []
    for s in range(n_sub):
        rows = slice(s * qsub, (s + 1) * qsub)
        p0 = jnp.where(is_first, starts[0][s], jnp.where(is_last, starts[2][s], starts[1][s]))
        key0.append(pl.multiple_of(p0 * pair, pair))
        bias_rows = []
        for qr in range(s * qsub // GRID_W, (s + 1) * qsub // GRID_W):
            tiles = [pb_ref[0, 0, jnp.clip(2 * (p0 + j) - qr + off + NA_KH, 0, 2 * NA_KH - 1)]
                     + vrow_ref[0, qr, p0 + j]
                     for j in range(NA_PAIRS)]
            bias_rows.append(jnp.concatenate(tiles, axis=1))
        bias = jnp.concatenate(bias_rows, axis=0)
        s_scr[s, :, 0:n_lat] = _dot_nt(qr_ref[rows, :], k_scr[pl.ds(key0[s], n_lat), :]) + bias
        s_scr[s, :, n_lat:] = _dot_nt(qp_ref[rows, :], kc)
    for s in range(n_sub):
        sc = s_scr[s]
        p = jnp.exp(sc - jnp.max(sc, axis=-1, keepdims=True))
        p_scr[s] = p.astype(BF16)
        den_scr[s] = jnp.sum(p, axis=-1, keepdims=True)
    for s in range(n_sub):
        rows = slice(s * qsub, (s + 1) * qsub)
        o = (_dot(p_scr[s, :, 0:n_lat], v_scr[pl.ds(key0[s], n_lat), :])
             + _dot(p_scr[s, :, n_lat:], vc))
        o_ref[rows, :] = (o / den_scr[s]).astype(BF16)


def _na_attention(qr, qp, kr, v, pb, vrow, layer, s_len):
    rows = s_len // GRID_W
    n_rb = rows // Q_ROWS
    qb = Q_ROWS * GRID_W
    kb = 256
    last_kblock = (rows - KEY_ROWS) * GRID_W // kb
    ctx_block = s_len // kb

    def kmap(j):
        return lambda h, i: (jnp.clip(2 * i - 1, 0, last_kblock) + j, h)

    def btype(i):
        return jnp.where(i == 0, 0, jnp.where(i == n_rb - 1, 2, 1))

    qspec = pl.BlockSpec((qb, HEAD_DIM), lambda h, i: (i, h))
    kspecs = [pl.BlockSpec((kb, HEAD_DIM), kmap(j)) for j in range(4)]
    cspec = pl.BlockSpec((kb, HEAD_DIM), lambda h, i: (ctx_block, h))
    return pl.pallas_call(
        functools.partial(_na_kernel, n_rb=n_rb),
        grid=(NA_HEADS, n_rb),
        in_specs=[qspec, qspec] + kspecs + kspecs + [cspec, cspec]
        + [pl.BlockSpec((1, 1, 2 * NA_KH, GRID_W, 2 * GRID_W), lambda h, i: (layer, h, 0, 0, 0)),
           pl.BlockSpec((1, Q_ROWS, KEY_ROWS // 2, 1, 2 * GRID_W),
                        lambda h, i: (btype(i), 0, 0, 0, 0))],
        out_specs=pl.BlockSpec((qb, HEAD_DIM), lambda h, i: (i, h)),
        out_shape=jax.ShapeDtypeStruct((s_len, NA_WIDTH), BF16),
        scratch_shapes=[pltpu.VMEM((KEY_ROWS * GRID_W, HEAD_DIM), BF16),
                        pltpu.VMEM((KEY_ROWS * GRID_W, HEAD_DIM), BF16),
                        pltpu.VMEM((qb // 128, 128, NA_PAIRS * 2 * GRID_W + kb), F32),
                        pltpu.VMEM((qb // 128, 128, NA_PAIRS * 2 * GRID_W + kb), BF16),
                        pltpu.VMEM((qb // 128, 128, 1), F32)],
        compiler_params=_cparams(("arbitrary", "arbitrary")),
        name="na_attention",
    )(qr, qp, kr, kr, kr, kr, v, v, v, v, kr, v, pb, vrow)


def _ctx_attn_kernel(q_ref, k_ref, v_ref, o_ref):
    s = _dot_nt(q_ref[...], k_ref[...])
    m = jnp.max(s, axis=-1, keepdims=True)
    p = jnp.exp(s - m)
    den = jnp.sum(p, axis=-1, keepdims=True)
    o_ref[...] = (_dot(p.astype(BF16), v_ref[...]) / den).astype(BF16)


def _ctx_attention(qp, kr, v, s_len, cl):
    blk = s_len // cl
    spec = pl.BlockSpec((cl, HEAD_DIM), lambda h: (blk, h))
    return pl.pallas_call(
        _ctx_attn_kernel,
        grid=(NA_HEADS,),
        in_specs=[spec, spec, spec],
        out_specs=pl.BlockSpec((cl, HEAD_DIM), lambda h: (0, h)),
        out_shape=jax.ShapeDtypeStruct((cl, NA_WIDTH), BF16),
        compiler_params=_cparams(("arbitrary",)),
        name="ctx_attention",
    )(qp, kr, v)


def _na_bias_tables(na_rpb, rows):
    n_rb = rows // Q_ROWS
    rmask = np.zeros((3, Q_ROWS, KEY_ROWS), bool)
    for t, i in enumerate((0, 1, n_rb - 1)):
        ks = min(max(Q_ROWS * i - NA_KH // 2, 0), rows - KEY_ROWS)
        for qr in range(Q_ROWS):
            r = Q_ROWS * i + qr
            lo = min(max(r - NA_KH // 2, 0), rows - NA_KH)
            for kr in range(KEY_ROWS):
                rmask[t, qr, kr] = lo <= ks + kr < lo + NA_KH
    vrow = np.where(np.repeat(rmask, GRID_W, axis=2), 0.0, NEG_INF).astype(np.float32)
    csel = np.zeros((GRID_W, GRID_W, 2 * NA_KW - 1), np.float32)
    cmask = np.zeros((GRID_W, GRID_W), bool)
    for qc in range(GRID_W):
        lo = min(max(qc - NA_KW // 2, 0), GRID_W - NA_KW)
        for kc in range(GRID_W):
            cmask[qc, kc] = lo <= kc < lo + NA_KW
            csel[qc, kc, min(max(kc - qc + NA_KW - 1, 0), 2 * NA_KW - 2)] = 1.0
    b = jnp.einsum('lhab,qkb->lhaqk', na_rpb, jnp.asarray(csel), precision=lax.Precision.HIGHEST)
    b = jnp.where(jnp.asarray(cmask), b, NEG_INF)
    b = jnp.pad(b, ((0, 0), (0, 0), (1, 1), (0, 0), (0, 0)))
    pb = jnp.concatenate([b[:, :, :-1], b[:, :, 1:]], axis=-1)
    return pb, jnp.asarray(vrow).reshape(3, Q_ROWS, KEY_ROWS // 2, 1, 2 * GRID_W)


def _sg_kernel(z_ref, w_ref, b_ref, o_ref):
    z = z_ref[...]
    g = 0.5 * z * (1.0 + lax.erf(z * (2.0 ** -0.5)))
    n_chunks = z.shape[0] // SG_CHUNK
    for gi in range(SG_GROUPS):
        u = g[:, gi * LANES:(gi + 1) * LANES]
        vn = _ln_rows(g[:, SG_WIDTH + gi * LANES:SG_WIDTH + (gi + 1) * LANES]).astype(BF16)
        for n in range(n_chunks):
            rows = slice(n * SG_CHUNK, (n + 1) * SG_CHUNK)
            t = _dot(w_ref[gi], vn[rows, :]) + b_ref[gi]
            o_ref[rows, gi * LANES:(gi + 1) * LANES] = (u[rows, :] * t).astype(BF16)


def _spatial_gating(z_sg, sgw_bf, sgb):
    t = z_sg.shape[0]
    tm = ROW_TILE
    return pl.pallas_call(
        _sg_kernel,
        grid=(t // tm,),
        in_specs=[pl.BlockSpec((tm, 2 * SG_WIDTH), lambda i: (i, 0)),
                  pl.BlockSpec((SG_GROUPS, SG_CHUNK, SG_CHUNK), lambda i: (0, 0, 0)),
                  pl.BlockSpec((SG_GROUPS, SG_CHUNK, 1), lambda i: (0, 0, 0))],
        out_specs=pl.BlockSpec((tm, SG_WIDTH), lambda i: (i, 0)),
        out_shape=jax.ShapeDtypeStruct((t, SG_WIDTH), BF16),
        compiler_params=_cparams(("arbitrary",)),
        name="spatial_gating",
    )(z_sg, sgw_bf, sgb)


def _ft1_kernel(z_ref, cs_ref, m1_ref, y_ref):
    n1 = z_ref.shape[0]
    parts = []
    for j in range(z_ref.shape[1] // LANES):
        ab = _dot(z_ref[:, j * LANES:(j + 1) * LANES], cs_ref[...])
        parts.append(jnp.concatenate([ab[:, :LANES], ab[:, LANES:]], axis=0))
    st = jnp.concatenate(parts, axis=1).astype(BF16)
    y = _dot(m1_ref[...], st)
    y_ref[0] = y[:n1].astype(BF16)
    y_ref[1] = y[n1:].astype(BF16)


def _ft2_kernel(y_ref, tw_ref, o_ref, *, scale):
    for j in range(y_ref.shape[1]):
        yy = jnp.concatenate([y_ref[0, j], y_ref[1, j]], axis=0)
        o = _dot(tw_ref[j], yy) * scale
        o_ref[:, j * FT_WIDTH:(j + 1) * FT_WIDTH] = o.astype(BF16)


def _fourier_consts(s_len, cl):
    n2 = 128
    n1 = s_len // n2
    c = np.arange(HEAD_DIM)
    ang = 2 * np.pi * ((c[:, None] * c[None, :]) % HEAD_DIM) / HEAD_DIM
    cs = np.concatenate([np.cos(ang), np.sin(ang)], axis=1)
    a = np.arange(n1)
    ang1 = 2 * np.pi * ((a[:, None] * a[None, :]) % n1) / n1
    fc, fs = np.cos(ang1), np.sin(ang1)
    m1 = np.block([[fc, -fs], [-fs, -fc]])
    ka = jnp.arange(n1, dtype=jnp.int32)[:, None, None]
    kb = jnp.arange(n2, dtype=jnp.int32)[None, :, None]
    nn = jnp.arange(n2, dtype=jnp.int32)[None, None, :]
    ph = (nn * (ka + n1 * kb)) % s_len
    th = ph.astype(F32) * (2 * np.pi / s_len)
    tw = jnp.concatenate([jnp.cos(th), jnp.sin(th)], axis=2).astype(BF16)
    p = np.arange(cl)
    angc = 2 * np.pi * ((p[:, None] * p[None, :]) % cl) / cl
    mc = np.concatenate([np.cos(angc), -np.sin(angc)], axis=1)
    return (jnp.asarray(cs, BF16), jnp.asarray(m1, BF16), tw, jnp.asarray(mc, BF16))


def _fourier_latent(z_ft, cs, m1, tw, s_len):
    n2 = 128
    n1 = s_len // n2
    width = n2 * FT_WIDTH
    cw = 2048
    y = pl.pallas_call(
        _ft1_kernel,
        grid=(width // cw,),
        in_specs=[pl.BlockSpec((n1, cw), lambda j: (0, j)),
                  pl.BlockSpec((HEAD_DIM, 2 * HEAD_DIM), lambda j: (0, 0)),
                  pl.BlockSpec((2 * n1, 2 * n1), lambda j: (0, 0))],
        out_specs=pl.BlockSpec((2, n1, cw), lambda j: (0, 0, j)),
        out_shape=jax.ShapeDtypeStruct((2, n1, width), BF16),
        compiler_params=_cparams(("arbitrary",)),
        name="fourier_stage1",
    )(z_ft[:s_len].reshape(n1, width), cs, m1)
    kab = 4
    out = pl.pallas_call(
        functools.partial(_ft2_kernel, scale=(s_len * HEAD_DIM) ** -0.5),
        grid=(n1 // kab,),
        in_specs=[pl.BlockSpec((2, kab, n2, FT_WIDTH), lambda j: (0, j, 0, 0)),
                  pl.BlockSpec((kab, n2, 2 * n2), lambda j: (j, 0, 0))],
        out_specs=pl.BlockSpec((n2, kab * FT_WIDTH), lambda j: (0, j)),
        out_shape=jax.ShapeDtypeStruct((n2, n1 * FT_WIDTH), BF16),
        compiler_params=_cparams(("arbitrary",)),
        name="fourier_stage2",
    )(y.reshape(2, n1, n2, FT_WIDTH), tw)
    return out.reshape(s_len, FT_WIDTH)


def _ft_ctx_kernel(z_ref, cs_ref, mc_ref, o_ref, *, scale):
    for g in range(FT_GROUPS):
        ab = _dot(z_ref[:, g * LANES:(g + 1) * LANES], cs_ref[...])
        st = jnp.concatenate([ab[:, :LANES], ab[:, LANES:]], axis=0).astype(BF16)
        o_ref[:, g * LANES:(g + 1) * LANES] = (_dot(mc_ref[...], st) * scale).astype(BF16)


def _fourier_ctx(z_ft, cs, mc, s_len, cl):
    blk = s_len // cl
    return pl.pallas_call(
        functools.partial(_ft_ctx_kernel, scale=(cl * HEAD_DIM) ** -0.5),
        grid=(1,),
        in_specs=[pl.BlockSpec((cl, FT_WIDTH), lambda i: (blk, 0)),
                  pl.BlockSpec((HEAD_DIM, 2 * HEAD_DIM), lambda i: (0, 0)),
                  pl.BlockSpec((cl, 2 * cl), lambda i: (0, 0))],
        out_specs=pl.BlockSpec((cl, FT_WIDTH), lambda i: (0, 0)),
        out_shape=jax.ShapeDtypeStruct((cl, FT_WIDTH), BF16),
        compiler_params=_cparams(("arbitrary",)),
        name="fourier_ctx",
    )(z_ft, cs, mc)


def _out_kernel(*refs, with_ctx, n_lat_tiles):
    if with_ctx:
        (ona_l, ona_c, osg, oft_l, oft_c, x_ref, w_ref, g_ref, lng, lnb, sh_ref, sc_ref,
         wr_ref, br_ref, x1_ref, h2_ref, lg_ref, r_scr, hs_scr) = refs
        is_ctx = pl.program_id(0) >= n_lat_tiles
        ona = jnp.where(is_ctx, ona_c[...], ona_l[...])
        oft = jnp.where(is_ctx, oft_c[...], oft_l[...])
    else:
        (ona_l, osg, oft_l, x_ref, w_ref, g_ref, lng, lnb, sh_ref, sc_ref,
         wr_ref, br_ref, x1_ref, h2_ref, lg_ref, r_scr, hs_scr) = refs
        ona = ona_l[...]
        oft = oft_l[...]
    sg = osg[...]
    nw = 512
    for n in range(D_MODEL // nw):
        cols = slice(n * nw, (n + 1) * nw)
        y = (_dot(ona, w_ref[0:NA_WIDTH, cols])
             + _dot(sg, w_ref[NA_WIDTH:NA_WIDTH + SG_WIDTH, cols])
             + _dot(oft, w_ref[NA_WIDTH + SG_WIDTH:, cols]))
        r_scr[:, cols] = ALPHA * x_ref[:, cols] + g_ref[0][:, cols] * y

    def norm_rows(ci, carry):
        rows = pl.ds(pl.multiple_of(ci * NORM_ROWS, NORM_ROWS), NORM_ROWS)
        x1 = _ln_rows(r_scr[rows, :]) * lng[...] + lnb[...]
        x1_ref[rows, :] = x1
        h2 = _ln_rows(x1) * (1.0 + sc_ref[0]) + sh_ref[0]
        h2_ref[rows, :] = h2
        hi = h2.astype(BF16)
        hs_scr[0, rows, :] = hi
        hs_scr[1, rows, :] = (h2 - hi.astype(F32)).astype(BF16)
        return carry

    lax.fori_loop(0, r_scr.shape[0] // NORM_ROWS, norm_rows, 0, unroll=True)
    lg_ref[...] = (_dot(hs_scr[0], wr_ref[0]) + _dot(hs_scr[1], wr_ref[0])
                   + _dot(hs_scr[0], wr_ref[1]) + br_ref[...])


def _out_proj(ona_l, ona_c, osg, oft_l, oft_c, xall, w_out_bf, gate, lng, lnb, shift, scale,
              wr_split, br_pad, layer, n_rows, n_lat_tiles, with_ctx):
    tm = ROW_TILE
    row = lambda i: (i, 0)
    lat = lambda i: (jnp.minimum(i, n_lat_tiles - 1), 0)
    typ = lambda i: (jnp.where(i >= n_lat_tiles, 1, 0), 0, 0)
    const = lambda i: (0, 0)
    modspec = pl.BlockSpec((1, 1, D_MODEL), typ)
    vec = pl.BlockSpec((1, D_MODEL), const)
    specs = [pl.BlockSpec((tm, NA_WIDTH), lat)]
    args = [ona_l]
    if with_ctx:
        specs.append(pl.BlockSpec((tm, NA_WIDTH), const))
        args.append(ona_c)
    specs.append(pl.BlockSpec((tm, SG_WIDTH), row))
    args.append(osg)
    specs.append(pl.BlockSpec((tm, FT_WIDTH), lat))
    args.append(oft_l)
    if with_ctx:
        specs.append(pl.BlockSpec((tm, FT_WIDTH), const))
        args.append(oft_c)
    specs += [pl.BlockSpec((tm, D_MODEL), row),
              pl.BlockSpec((None, D_MODEL, D_MODEL), lambda i: (layer, 0, 0),
                           pipeline_mode=pl.Buffered(1)),
              modspec, vec, vec, modspec, modspec,
              pl.BlockSpec((None, 2, D_MODEL, LANES), lambda i: (layer, 0, 0, 0)),
              pl.BlockSpec((1, LANES), const)]
    args += [xall, w_out_bf, gate, lng, lnb, shift, scale, wr_split, br_pad]
    return pl.pallas_call(
        functools.partial(_out_kernel, with_ctx=with_ctx, n_lat_tiles=n_lat_tiles),
        grid=(n_rows // tm,),
        in_specs=specs,
        out_specs=[pl.BlockSpec((tm, D_MODEL), row),
                   pl.BlockSpec((tm, D_MODEL), row),
                   pl.BlockSpec((tm, LANES), row)],
        out_shape=[jax.ShapeDtypeStruct((n_rows, D_MODEL), F32),
                   jax.ShapeDtypeStruct((n_rows, D_MODEL), F32),
                   jax.ShapeDtypeStruct((n_rows, LANES), F32)],
        scratch_shapes=[pltpu.VMEM((tm, D_MODEL), F32),
                        pltpu.VMEM((2, tm, D_MODEL), BF16)],
        compiler_params=_cparams(("arbitrary",)),
        name="out_proj",
    )(*args)


def _route_kernel(lg_ref, tri_ref, eid_ref, rank_ref, gate_ref, cnt_ref, carry):
    @pl.when(pl.program_id(0) == 0)
    def _():
        carry[...] = jnp.zeros_like(carry)

    work = lg_ref[...]
    lane = lax.broadcasted_iota(jnp.int32, work.shape, 1)
    lane_f = lane.astype(F32)
    vals, ids, hots = [], [], []
    for _ in range(TOP_K):
        m = jnp.max(work, axis=-1, keepdims=True)
        idx = jnp.min(jnp.where(work == m, lane_f, float(LANES)), axis=-1, keepdims=True)
        hot = lane_f == idx
        vals.append(m)
        ids.append(idx)
        hots.append(hot)
        work = jnp.where(hot, -jnp.inf, work)
    exps = [jnp.exp(v - vals[0]) for v in vals]
    den = exps[0] + exps[1] + exps[2] + exps[3]
    multi = jnp.zeros(work.shape, F32)
    for hot in hots:
        multi = multi + hot.astype(F32)
    pref = _dot(tri_ref[...], multi.astype(BF16)) + carry[...]
    eid = jnp.zeros(work.shape, F32)
    rank = jnp.zeros(work.shape, F32)
    gate = jnp.zeros(work.shape, F32)
    for k in range(TOP_K):
        rk = jnp.sum(jnp.where(hots[k], pref, 0.0), axis=-1, keepdims=True)
        sel = lane == k
        eid = jnp.where(sel, ids[k], eid)
        rank = jnp.where(sel, rk, rank)
        gate = jnp.where(sel, exps[k] / den, gate)
    eid_ref[...] = eid.astype(jnp.int32)
    rank_ref[...] = rank.astype(jnp.int32)
    gate_ref[...] = gate
    carry[...] = carry[...] + jnp.sum(multi, axis=0, keepdims=True)
    cnt_ref[...] = jnp.broadcast_to(carry[...], cnt_ref.shape).astype(jnp.int32)


def _route(logits, tri):
    t = logits.shape[0]
    tm = ROW_TILE
    row = lambda i: (i, 0)
    return pl.pallas_call(
        _route_kernel,
        grid=(t // tm,),
        in_specs=[pl.BlockSpec((tm, LANES), row),
                  pl.BlockSpec((tm, tm), lambda i: (0, 0))],
        out_specs=[pl.BlockSpec((tm, LANES), row)] * 3
        + [pl.BlockSpec((8, LANES), lambda i: (0, 0))],
        out_shape=[jax.ShapeDtypeStruct((t, LANES), jnp.int32)] * 2
        + [jax.ShapeDtypeStruct((t, LANES), F32),
           jax.ShapeDtypeStruct((8, LANES), jnp.int32)],
        scratch_shapes=[pltpu.VMEM((1, LANES), F32)],
        compiler_params=_cparams(("arbitrary",)),
        name="route",
    )(logits, tri)


def _expert_kernel(bexp_ref, nused_ref, tok_ref, tok_next_ref, h2_ref, wgu_ref, bgu_ref, wd_ref,
                   bd_ref, perm_ref, y_ref, wgu_s, wd_s, xb0, xb1, gsem):
    b = pl.program_id(0)
    n_used = nused_ref[0]
    active = b < n_used
    xbufs = (xb0, xb1)
    blk = xb0.shape[0]

    def gather(idx_ref, block, p):
        off = (block % SLOT_BLOCKS) * blk
        for r in range(blk):
            pltpu.make_async_copy(h2_ref.at[pl.ds(idx_ref[off + r], 1)],
                                  xbufs[p].at[pl.ds(r, 1)], gsem.at[p]).start(priority=r % 2)

    def gather_wait(p):
        pltpu.make_async_copy(h2_ref.at[pl.ds(0, blk)], xbufs[p], gsem.at[p]).wait()

    @pl.when(b == 0)
    def _():
        gather(tok_ref, b, 0)

    @pl.when(jnp.logical_not(active))
    def _():
        y_ref[...] = jnp.zeros_like(y_ref)

    fresh = jnp.logical_or(b == 0, bexp_ref[b] != bexp_ref[jnp.maximum(b - 1, 0)])

    @pl.when(jnp.logical_and(active, fresh))
    def _():
        wgu_s[...] = wgu_ref[...].astype(BF16)
        wd_s[...] = _dot(perm_ref[...], wd_ref[...].astype(BF16)).astype(BF16)

    def compute(p):
        gu = _dot(xbufs[p][...].astype(BF16), wgu_s[...]) + bgu_ref[...]
        lane = lax.broadcasted_iota(jnp.int32, (blk, LANES), 1)
        even = (lane % 2) == 0
        prods = []
        for c in range(2 * D_FF // LANES):
            guc = gu[:, c * LANES:(c + 1) * LANES]
            glu = jnp.minimum(guc, SWIGLU_LIMIT)
            lin = jnp.clip(guc, -SWIGLU_LIMIT, SWIGLU_LIMIT) + 1.0
            prods.append(glu * jax.nn.sigmoid(SWIGLU_ALPHA * glu)
                         * pltpu.roll(lin, LANES - 1, axis=1))
        merged = [jnp.where(even, prods[2 * m], pltpu.roll(prods[2 * m + 1], 1, axis=1))
                  for m in range(D_FF // LANES)]
        act = jnp.concatenate(merged, axis=1).astype(BF16)
        y_ref[...] = _dot(act, wd_s[...]) + bd_ref[...]

    nxt = jnp.minimum(b + 1, n_used - 1)
    for p in range(2):
        @pl.when(jnp.logical_and(active, b % 2 == p))
        def _():
            gather_wait(p)
            gather(tok_next_ref, nxt, 1 - p)
            compute(p)

            @pl.when(b == n_used - 1)
            def _():
                gather_wait(1 - p)


def _experts(bexp, nused, slot_tok, h2, w_gate_up, b_gate_up, w_down, b_down, layer, n_blocks):
    blk = MOE_BLOCK

    def emap(b, be, nu):
        return (layer, be[jnp.minimum(b, nu[0] - 1)], 0, 0)

    def slots(shift):
        def index(b, be, nu):
            return (jnp.clip(b + shift, 0, nu[0] - 1) // SLOT_BLOCKS,)
        return pl.BlockSpec((SLOT_BLOCKS * blk,), index, memory_space=pltpu.SMEM)

    gs = pltpu.PrefetchScalarGridSpec(
        num_scalar_prefetch=2,
        grid=(n_blocks,),
        in_specs=[slots(0), slots(1),
                  pl.BlockSpec(memory_space=pl.ANY),
                  pl.BlockSpec((None, None, D_MODEL, 2 * D_FF), emap),
                  pl.BlockSpec((None, None, 1, 2 * D_FF), emap),
                  pl.BlockSpec((None, None, D_FF, D_MODEL), emap),
                  pl.BlockSpec((None, None, 1, D_MODEL), emap),
                  pl.BlockSpec((D_FF, D_FF), lambda b, be, nu: (0, 0))],
        out_specs=pl.BlockSpec((blk, D_MODEL), lambda b, be, nu: (b, 0)),
        scratch_shapes=[pltpu.VMEM((D_MODEL, 2 * D_FF), BF16),
                        pltpu.VMEM((D_FF, D_MODEL), BF16),
                        pltpu.VMEM((blk, D_MODEL), F32), pltpu.VMEM((blk, D_MODEL), F32),
                        pltpu.SemaphoreType.DMA((2,))],
    )
    lane = np.arange(D_FF) % LANES
    unit = (np.arange(D_FF) // LANES) * LANES + lane // 2 + (lane % 2) * (LANES // 2)
    perm = np.zeros((D_FF, D_FF), np.float32)
    perm[np.arange(D_FF), unit] = 1.0
    return pl.pallas_call(
        _expert_kernel,
        grid_spec=gs,
        out_shape=jax.ShapeDtypeStruct((n_blocks * blk, D_MODEL), F32),
        compiler_params=_cparams(("arbitrary",), row_dma=True),
        name="experts",
    )(bexp, nused, slot_tok, slot_tok, h2, w_gate_up, b_gate_up[:, :, None, :],
      w_down, b_down[:, :, None, :], jnp.asarray(perm, BF16))


def _combine_kernel(x1_ref, slot_ref, slot_next_ref, gate_ref, yb_ref, g_ref, lng, lnb,
                    o_ref, gbuf, sems):
    tm = x1_ref.shape[0]
    i = pl.program_id(0)
    cur = i % 2

    def row_copy(src_row, buf, k, t):
        return pltpu.make_async_copy(yb_ref.at[pl.ds(src_row, 1)],
                                     gbuf.at[buf, k, pl.ds(t, 1)], sems.at[buf])

    def gather(slots, buf):
        def per_token(g, carry):
            for u in range(2):
                t = g * 2 + u
                for k in range(TOP_K):
                    row_copy(slots[t * TOP_K + k], buf, k, t).start(priority=k % 2)
            return carry
        lax.fori_loop(0, tm // 2, per_token, 0)

    @pl.when(i == 0)
    def _():
        gather(slot_ref, 0)

    @pl.when(i + 1 < pl.num_programs(0))
    def _():
        gather(slot_next_ref, 1 - cur)

    for k in range(TOP_K):
        pltpu.make_async_copy(yb_ref.at[pl.ds(0, tm)], gbuf.at[cur, k], sems.at[cur]).wait()

    def norm_rows(ci, carry):
        rows = pl.ds(pl.multiple_of(ci * NORM_ROWS, NORM_ROWS), NORM_ROWS)
        gate = gate_ref[rows, :]
        f = gate[:, 0:1] * gbuf[cur, 0, rows, :]
        for k in range(1, TOP_K):
            f = f + gate[:, k:k + 1] * gbuf[cur, k, rows, :]
        r = ALPHA * x1_ref[rows, :] + g_ref[0] * f
        o_ref[rows, :] = _ln_rows(r) * lng[...] + lnb[...]
        return carry

    lax.fori_loop(0, tm // NORM_ROWS, norm_rows, 0, unroll=True)


def _combine(x1, slot_flat, gates, yb, gate_mod, lng, lnb, n_lat_tiles):
    t = x1.shape[0]
    tm = ROW_TILE
    n_tiles = t // tm
    return pl.pallas_call(
        _combine_kernel,
        grid=(n_tiles,),
        in_specs=[pl.BlockSpec((tm, D_MODEL), lambda i: (i, 0)),
                  pl.BlockSpec((tm * TOP_K,), lambda i: (i,), memory_space=pltpu.SMEM),
                  pl.BlockSpec((tm * TOP_K,), lambda i: (jnp.minimum(i + 1, n_tiles - 1),),
                               memory_space=pltpu.SMEM),
                  pl.BlockSpec((tm, LANES), lambda i: (i, 0)),
                  pl.BlockSpec(memory_space=pl.ANY),
                  pl.BlockSpec((1, 1, D_MODEL),
                               lambda i: (jnp.where(i >= n_lat_tiles, 1, 0), 0, 0)),
                  pl.BlockSpec((1, D_MODEL), lambda i: (0, 0)),
                  pl.BlockSpec((1, D_MODEL), lambda i: (0, 0))],
        out_specs=pl.BlockSpec((tm, D_MODEL), lambda i: (i, 0)),
        out_shape=jax.ShapeDtypeStruct((t, D_MODEL), F32),
        scratch_shapes=[pltpu.VMEM((2, TOP_K, tm, D_MODEL), F32),
                        pltpu.SemaphoreType.DMA((2,))],
        compiler_params=_cparams(("arbitrary",), row_dma=True),
        name="combine",
    )(x1, slot_flat, slot_flat, gates, yb, gate_mod, lng, lnb)


def _rope_tables(s_len, cl):
    t = jnp.arange(s_len, dtype=jnp.int32)
    quarter = HEAD_DIM // 4
    inv = ROPE_BASE ** (-jnp.arange(quarter, dtype=F32) / quarter)
    ang_r = (t // GRID_W).astype(F32)[:, None] * inv
    ang_c = (t % GRID_W).astype(F32)[:, None] * inv
    cr, sr, cc, sc = jnp.cos(ang_r), jnp.sin(ang_r), jnp.cos(ang_c), jnp.sin(ang_c)
    cos_t = jnp.concatenate([cr, cr, cc, cc], axis=1)
    sin_t = jnp.concatenate([-sr, sr, -sc, sc], axis=1)
    cos_t = jnp.concatenate([cos_t, jnp.ones((cl, HEAD_DIM), F32)], axis=0)
    sin_t = jnp.concatenate([sin_t, jnp.zeros((cl, HEAD_DIM), F32)], axis=0)
    return cos_t, sin_t


def _block_plan(counts, eid, rank, n_blocks):
    n_assign = eid.size
    n_slots = n_blocks * MOE_BLOCK
    blocks_per = (counts + MOE_BLOCK - 1) // MOE_BLOCK
    block_end = jnp.cumsum(blocks_per)
    base = (block_end - blocks_per) * MOE_BLOCK
    bexp = jnp.minimum(jnp.sum(block_end[None, :] <= jnp.arange(n_blocks)[:, None], axis=1),
                       N_EXPERTS - 1).astype(jnp.int32)
    nused = block_end[-1:].astype(jnp.int32)
    experts = jnp.arange(N_EXPERTS, dtype=jnp.int32)
    slot = rank + jnp.sum(jnp.where(eid[:, :, None] == experts, base, 0), axis=-1)
    pad_j = jnp.arange(MOE_BLOCK, dtype=jnp.int32)[None, :]
    pad_slot = jnp.where(pad_j < (blocks_per * MOE_BLOCK - counts)[:, None],
                         (base + counts)[:, None] + pad_j, n_slots)
    keys = jnp.concatenate([slot.reshape(-1), pad_slot.reshape(-1)]).astype(jnp.int32)
    vals = jnp.concatenate([jnp.arange(n_assign, dtype=jnp.int32),
                            jnp.full((N_EXPERTS * MOE_BLOCK,), -1, jnp.int32)])
    assign = lax.sort((keys, vals), num_keys=1)[1]
    slot_tok = jnp.where(assign < 0, 0, assign // TOP_K)
    return bexp, nused, slot.reshape(-1).astype(jnp.int32), slot_tok


def kernel(x, c, ctx, c_ctx, w_ada, b_ada, w_in, w_out, sg_w, sg_b, na_rpb, ln1_g, ln1_b,
           ln2_g, ln2_b, w_router, b_router, w_gate_up, b_gate_up, w_down, b_down):
    bsz, s_len, dm = x.shape
    cl = ctx.shape[1]
    depth = w_ada.shape[0]
    assert bsz == 1 and dm == D_MODEL and cl == ROW_TILE and MOE_BLOCK == ROW_TILE
    assert s_len % (GRID_W * KEY_ROWS) == 0 and s_len % ROW_TILE == 0
    t_all = s_len + cl
    n_lat_tiles = s_len // ROW_TILE
    rows = s_len // GRID_W

    cond = jnp.zeros((8, dm), F32).at[0].set(c[0]).at[1].set(c_ctx)
    mod = _ada_mod(cond, w_ada, b_ada)[:, :2].reshape(depth, 2, 6, 1, dm)
    cos_t, sin_t = _rope_tables(s_len, cl)
    pb, vrow = _na_bias_tables(na_rpb, rows)
    cs, m1, tw, mc = _fourier_consts(s_len, cl)
    tri = jnp.asarray(np.tril(np.ones((ROW_TILE, ROW_TILE), np.float32), -1), BF16)
    wr_pad = jnp.pad(w_router, ((0, 0), (0, 0), (0, LANES - N_EXPERTS)))
    wr_hi = wr_pad.astype(BF16)
    wr_split = jnp.stack([wr_hi, (wr_pad - wr_hi.astype(F32)).astype(BF16)], axis=1)
    br_pad = jnp.pad(b_router, ((0, 0), (0, LANES - N_EXPERTS)), constant_values=NEG_INF)
    w_in_bf = w_in.astype(BF16)
    w_out_bf = w_out.astype(BF16)
    sgw_bf = sg_w.astype(BF16)

    xall = jnp.concatenate([x[0], ctx[0]], axis=0)
    for l in range(depth):
        last = l == depth - 1
        m = lambda j: mod[l, :, j]
        qp, qr, kr, v, z_sg, z_ft = _proj(xall, m(0), m(1), w_in_bf, cos_t, sin_t, l,
                                          n_lat_tiles)
        ona_l = _na_attention(qr, qp, kr, v, pb, vrow, l, s_len)
        osg = _spatial_gating(z_sg, sgw_bf[l], sg_b[l][:, :, None])
        oft_l = _fourier_latent(z_ft, cs, m1, tw, s_len)
        if last:
            ona_c = oft_c = None
            n_rows = s_len
        else:
            ona_c = _ctx_attention(qp, kr, v, s_len, cl)
            oft_c = _fourier_ctx(z_ft, cs, mc, s_len, cl)
            n_rows = t_all
        x1, h2, logits = _out_proj(
            ona_l, ona_c, osg, oft_l, oft_c, xall, w_out_bf, m(2),
            ln1_g[l][None], ln1_b[l][None], m(3), m(4), wr_split, br_pad[l][None],
            l, n_rows, n_lat_tiles, not last)
        eid, rank, gates, counts = _route(logits, tri)
        n_blocks = -(-n_rows * TOP_K // MOE_BLOCK) + N_EXPERTS
        bexp, nused, slot_flat, slot_tok = _block_plan(
            counts[0, :N_EXPERTS], eid[:, :TOP_K], rank[:, :TOP_K], n_blocks)
        yb = _experts(bexp, nused, slot_tok, h2, w_gate_up, b_gate_up, w_down, b_down, l,
                      n_blocks)
        xall = _combine(x1, slot_flat, gates, yb, m(5), ln2_g[l][None], ln2_b[l][None],
                        n_lat_tiles)
    return xall[None]
```

```python
import functools
import math

import numpy as np
import jax
import jax.numpy as jnp
from jax import lax
from jax.experimental import pallas as pl
from jax.experimental.pallas import tpu as pltpu

D_MODEL = 2048
DEPTH_NORM = 4
GRID_W = 64
HEAD_DIM = 128
NA_HEADS = 8
NA_WIDTH = NA_HEADS * HEAD_DIM
NA_KH = 8
NA_KW = 16
SG_GROUPS = 4
SG_WIDTH = 512
SG_CHUNK = 128
FT_GROUPS = 4
FT_WIDTH = 512
SG_OFF = 3 * NA_WIDTH
FT_OFF = SG_OFF + 2 * SG_WIDTH
IN_WIDTH = FT_OFF + FT_WIDTH
ROPE_BASE = 10000.0
N_EXPERTS = 32
TOP_K = 4
D_FF = D_MODEL // 4
SWIGLU_ALPHA = 1.702
SWIGLU_LIMIT = 7.0
LN_EPS = 1e-5
NEG_INF = -1e30
ALPHA = (2 * DEPTH_NORM) ** 0.25

LANES = 128
ROW_TILE = 256
MOE_BLOCK = 256
SLOT_BLOCKS = 4
NORM_ROWS = 32
KEY_ROWS = 16
Q_ROWS = 8
NA_PAIRS = 5
VMEM_LIMIT = 56 * 1024 * 1024

F32 = jnp.float32
BF16 = jnp.bfloat16


def _cparams(sem, row_dma=False):
    return pltpu.CompilerParams(dimension_semantics=sem, vmem_limit_bytes=VMEM_LIMIT,
                                disable_bounds_checks=row_dma)


def _dot(a, b):
    return jnp.dot(a, b, preferred_element_type=F32)


def _dot_nt(a, b):
    return lax.dot_general(a, b, (((1,), (1,)), ((), ())), preferred_element_type=F32)


def _ln_rows(x):
    mu = jnp.mean(x, axis=-1, keepdims=True)
    xc = x - mu
    var = jnp.mean(xc * xc, axis=-1, keepdims=True)
    return xc * lax.rsqrt(var + LN_EPS)


def _ada_kernel(c_ref, w_ref, b_ref, o_ref):
    c = c_ref[...]
    s = c * jax.nn.sigmoid(c)
    o_ref[0] = jnp.dot(s, w_ref[0], preferred_element_type=F32,
                       precision=lax.Precision.HIGHEST) + b_ref[0]


def _ada_mod(cond, w_ada, b_ada):
    depth, d, n = w_ada.shape
    tn = 1536
    return pl.pallas_call(
        _ada_kernel,
        grid=(depth, n // tn),
        in_specs=[pl.BlockSpec((8, d), lambda l, j: (0, 0)),
                  pl.BlockSpec((1, d, tn), lambda l, j: (l, 0, j)),
                  pl.BlockSpec((1, 1, tn), lambda l, j: (l, 0, j))],
        out_specs=pl.BlockSpec((1, 8, tn), lambda l, j: (l, 0, j)),
        out_shape=jax.ShapeDtypeStruct((depth, 8, n), F32),
        compiler_params=_cparams(("arbitrary", "arbitrary")),
        name="ada_mod",
    )(cond, w_ada, b_ada.reshape(depth, 1, n))


def _proj_kernel(x_ref, sh_ref, sc_ref, w_ref, cos_ref, sin_ref,
                 qp_ref, qr_ref, kr_ref, v_ref, sg_ref, ft_ref):
    y = _ln_rows(x_ref[...])
    h = (y * (1.0 + sc_ref[0]) + sh_ref[0]).astype(BF16)
    cos = cos_ref[...]
    sin = sin_ref[...]
    lane = lax.broadcasted_iota(jnp.int32, cos.shape, 1)
    first = (lane % 64) < 32

    def rope(z):
        swapped = jnp.where(first, pltpu.roll(z, 96, axis=1), pltpu.roll(z, 32, axis=1))
        return z * cos + swapped * sin

    nw = 512
    for j in range(IN_WIDTH // nw):
        z = _dot(h, w_ref[:, j * nw:(j + 1) * nw])
        for p in range(nw // LANES):
            col = j * nw + p * LANES
            zp = z[:, p * LANES:(p + 1) * LANES]
            if col < NA_WIDTH:
                zp = zp * (HEAD_DIM ** -0.5)
                qp_ref[:, col:col + LANES] = zp.astype(BF16)
                qr_ref[:, col:col + LANES] = rope(zp).astype(BF16)
            elif col < 2 * NA_WIDTH:
                c0 = col - NA_WIDTH
                kr_ref[:, c0:c0 + LANES] = rope(zp).astype(BF16)
            elif col < SG_OFF:
                c0 = col - 2 * NA_WIDTH
                v_ref[:, c0:c0 + LANES] = zp.astype(BF16)
            elif col < FT_OFF:
                c0 = col - SG_OFF
                sg_ref[:, c0:c0 + LANES] = zp
            else:
                c0 = col - FT_OFF
                ft_ref[:, c0:c0 + LANES] = zp.astype(BF16)


def _proj(xall, shift, scale, w_in_bf, cos_t, sin_t, layer, n_lat_tiles):
    t = xall.shape[0]
    tm = ROW_TILE
    typ = lambda i: (jnp.where(i >= n_lat_tiles, 1, 0), 0, 0)
    row = lambda i: (i, 0)
    return pl.pallas_call(
        _proj_kernel,
        grid=(t // tm,),
        in_specs=[pl.BlockSpec((tm, D_MODEL), row),
                  pl.BlockSpec((1, 1, D_MODEL), typ),
                  pl.BlockSpec((1, 1, D_MODEL), typ),
                  pl.BlockSpec((None, D_MODEL, IN_WIDTH), lambda i: (layer, 0, 0),
                               pipeline_mode=pl.Buffered(1)),
                  pl.BlockSpec((tm, LANES), row),
                  pl.BlockSpec((tm, LANES), row)],
        out_specs=[pl.BlockSpec((tm, NA_WIDTH), row)] * 4
        + [pl.BlockSpec((tm, 2 * SG_WIDTH), row), pl.BlockSpec((tm, FT_WIDTH), row)],
        out_shape=[jax.ShapeDtypeStruct((t, NA_WIDTH), BF16)] * 4
        + [jax.ShapeDtypeStruct((t, 2 * SG_WIDTH), F32),
           jax.ShapeDtypeStruct((t, FT_WIDTH), BF16)],
        compiler_params=_cparams(("arbitrary",)),
        name="ln_proj",
    )(xall, shift, scale, w_in_bf, cos_t, sin_t)


def _na_kernel(qr_ref, qp_ref, k0, k1, k2, k3, v0, v1, v2, v3, kc_ref, vc_ref, pb_ref, vrow_ref,
               o_ref, k_scr, v_scr, s_scr, p_scr, den_scr, *, n_rb):
    i = pl.program_id(1)
    is_first = i == 0
    is_last = i == n_rb - 1
    off = jnp.where(is_first, 0, jnp.where(is_last, -(KEY_ROWS - Q_ROWS), -(NA_KH // 2)))
    kb = k0.shape[0]
    for j, (kj, vj) in enumerate(((k0, v0), (k1, v1), (k2, v2), (k3, v3))):
        k_scr[j * kb:(j + 1) * kb, :] = kj[...]
        v_scr[j * kb:(j + 1) * kb, :] = vj[...]
    kc = kc_ref[...]
    vc = vc_ref[...]
    qsub = 128
    pair = 2 * GRID_W
    starts = ((0, 0, 0, 1), (0, 1, 2, 3), (2, 3, 3, 3))
    n_sub = Q_ROWS * GRID_W // qsub
    n_lat = NA_PAIRS * pair
    key0 = []
    for s in range(n_sub):
        rows = slice(s * qsub, (s + 1) * qsub)
        p0 = jnp.where(is_first, starts[0][s], jnp.where(is_last, starts[2][s], starts[1][s]))
        key0.append(pl.multiple_of(p0 * pair, pair))
        bias_rows = []
        for qr in range(s * qsub // GRID_W, (s + 1) * qsub // GRID_W):
            tiles = [pb_ref[0, 0, jnp.clip(2 * (p0 + j) - qr + off + NA_KH, 0, 2 * NA_KH - 1)]
                     + vrow_ref[0, qr, p0 + j]
                     for j in range(NA_PAIRS)]
            bias_rows.append(jnp.concatenate(tiles, axis=1))
        bias = jnp.concatenate(bias_rows, axis=0)
        s_scr[s, :, 0:n_lat] = _dot_nt(qr_ref[rows, :], k_scr[pl.ds(key0[s], n_lat), :]) + bias
        s_scr[s, :, n_lat:] = _dot_nt(qp_ref[rows, :], kc)
    for s in range(n_sub):
        sc = s_scr[s]
        p = jnp.exp(sc - jnp.max(sc, axis=-1, keepdims=True))
        p_scr[s] = p.astype(BF16)
        den_scr[s] = jnp.sum(p, axis=-1, keepdims=True)
    for s in range(n_sub):
        rows = slice(s * qsub, (s + 1) * qsub)
        o = (_dot(p_scr[s, :, 0:n_lat], v_scr[pl.ds(key0[s], n_lat), :])
             + _dot(p_scr[s, :, n_lat:], vc))
        o_ref[rows, :] = (o / den_scr[s]).astype(BF16)


def _na_attention(qr, qp, kr, v, pb, vrow, layer, s_len):
    rows = s_len // GRID_W
    n_rb = rows // Q_ROWS
    qb = Q_ROWS * GRID_W
    kb = 256
    last_kblock = (rows - KEY_ROWS) * GRID_W // kb
    ctx_block = s_len // kb

    def kmap(j):
        return lambda h, i: (jnp.clip(2 * i - 1, 0, last_kblock) + j, h)

    def btype(i):
        return jnp.where(i == 0, 0, jnp.where(i == n_rb - 1, 2, 1))

    qspec = pl.BlockSpec((qb, HEAD_DIM), lambda h, i: (i, h))
    kspecs = [pl.BlockSpec((kb, HEAD_DIM), kmap(j)) for j in range(4)]
    cspec = pl.BlockSpec((kb, HEAD_DIM), lambda h, i: (ctx_block, h))
    return pl.pallas_call(
        functools.partial(_na_kernel, n_rb=n_rb),
        grid=(NA_HEADS, n_rb),
        in_specs=[qspec, qspec] + kspecs + kspecs + [cspec, cspec]
        + [pl.BlockSpec((1, 1, 2 * NA_KH, GRID_W, 2 * GRID_W), lambda h, i: (layer, h, 0, 0, 0)),
           pl.BlockSpec((1, Q_ROWS, KEY_ROWS // 2, 1, 2 * GRID_W),
                        lambda h, i: (btype(i), 0, 0, 0, 0))],
        out_specs=pl.BlockSpec((qb, HEAD_DIM), lambda h, i: (i, h)),
        out_shape=jax.ShapeDtypeStruct((s_len, NA_WIDTH), BF16),
        scratch_shapes=[pltpu.VMEM((KEY_ROWS * GRID_W, HEAD_DIM), BF16),
                        pltpu.VMEM((KEY_ROWS * GRID_W, HEAD_DIM), BF16),
                        pltpu.VMEM((qb // 128, 128, NA_PAIRS * 2 * GRID_W + kb), F32),
                        pltpu.VMEM((qb // 128, 128, NA_PAIRS * 2 * GRID_W + kb), BF16),
                        pltpu.VMEM((qb // 128, 128, 1), F32)],
        compiler_params=_cparams(("arbitrary", "arbitrary")),
        name="na_attention",
    )(qr, qp, kr, kr, kr, kr, v, v, v, v, kr, v, pb, vrow)


def _ctx_attn_kernel(q_ref, k_ref, v_ref, o_ref):
    s = _dot_nt(q_ref[...], k_ref[...])
    m = jnp.max(s, axis=-1, keepdims=True)
    p = jnp.exp(s - m)
    den = jnp.sum(p, axis=-1, keepdims=True)
    o_ref[...] = (_dot(p.astype(BF16), v_ref[...]) / den).astype(BF16)


def _ctx_attention(qp, kr, v, s_len, cl):
    blk = s_len // cl
    spec = pl.BlockSpec((cl, HEAD_DIM), lambda h: (blk, h))
    return pl.pallas_call(
        _ctx_attn_kernel,
        grid=(NA_HEADS,),
        in_specs=[spec, spec, spec],
        out_specs=pl.BlockSpec((cl, HEAD_DIM), lambda h: (0, h)),
        out_shape=jax.ShapeDtypeStruct((cl, NA_WIDTH), BF16),
        compiler_params=_cparams(("arbitrary",)),
        name="ctx_attention",
    )(qp, kr, v)


def _na_bias_tables(na_rpb, rows):
    n_rb = rows // Q_ROWS
    rmask = np.zeros((3, Q_ROWS, KEY_ROWS), bool)
    for t, i in enumerate((0, 1, n_rb - 1)):
        ks = min(max(Q_ROWS * i - NA_KH // 2, 0), rows - KEY_ROWS)
        for qr in range(Q_ROWS):
            r = Q_ROWS * i + qr
            lo = min(max(r - NA_KH // 2, 0), rows - NA_KH)
            for kr in range(KEY_ROWS):
                rmask[t, qr, kr] = lo <= ks + kr < lo + NA_KH
    vrow = np.where(np.repeat(rmask, GRID_W, axis=2), 0.0, NEG_INF).astype(np.float32)
    csel = np.zeros((GRID_W, GRID_W, 2 * NA_KW - 1), np.float32)
    cmask = np.zeros((GRID_W, GRID_W), bool)
    for qc in range(GRID_W):
        lo = min(max(qc - NA_KW // 2, 0), GRID_W - NA_KW)
        for kc in range(GRID_W):
            cmask[qc, kc] = lo <= kc < lo + NA_KW
            csel[qc, kc, min(max(kc - qc + NA_KW - 1, 0), 2 * NA_KW - 2)] = 1.0
    b = jnp.einsum('lhab,qkb->lhaqk', na_rpb, jnp.asarray(csel), precision=lax.Precision.HIGHEST)
    b = jnp.where(jnp.asarray(cmask), b, NEG_INF)
    b = jnp.pad(b, ((0, 0), (0, 0), (1, 1), (0, 0), (0, 0)))
    pb = jnp.concatenate([b[:, :, :-1], b[:, :, 1:]], axis=-1)
    return pb, jnp.asarray(vrow).reshape(3, Q_ROWS, KEY_ROWS // 2, 1, 2 * GRID_W)


def _sg_kernel(z_ref, w_ref, b_ref, o_ref):
    z = z_ref[...]
    g = 0.5 * z * (1.0 + lax.erf(z * (2.0 ** -0.5)))
    n_chunks = z.shape[0] // SG_CHUNK
    for gi in range(SG_GROUPS):
        u = g[:, gi * LANES:(gi + 1) * LANES]
        vn = _ln_rows(g[:, SG_WIDTH + gi * LANES:SG_WIDTH + (gi + 1) * LANES]).astype(BF16)
        for n in range(n_chunks):
            rows = slice(n * SG_CHUNK, (n + 1) * SG_CHUNK)
            t = _dot(w_ref[gi], vn[rows, :]) + b_ref[gi]
            o_ref[rows, gi * LANES:(gi + 1) * LANES] = (u[rows, :] * t).astype(BF16)


def _spatial_gating(z_sg, sgw_bf, sgb):
    t = z_sg.shape[0]
    tm = ROW_TILE
    return pl.pallas_call(
        _sg_kernel,
        grid=(t // tm,),
        in_specs=[pl.BlockSpec((tm, 2 * SG_WIDTH), lambda i: (i, 0)),
                  pl.BlockSpec((SG_GROUPS, SG_CHUNK, SG_CHUNK), lambda i: (0, 0, 0)),
                  pl.BlockSpec((SG_GROUPS, SG_CHUNK, 1), lambda i: (0, 0, 0))],
        out_specs=pl.BlockSpec((tm, SG_WIDTH), lambda i: (i, 0)),
        out_shape=jax.ShapeDtypeStruct((t, SG_WIDTH), BF16),
        compiler_params=_cparams(("arbitrary",)),
        name="spatial_gating",
    )(z_sg, sgw_bf, sgb)


def _ft1_kernel(z_ref, cs_ref, m1_ref, y_ref):
    n1 = z_ref.shape[0]
    parts = []
    for j in range(z_ref.shape[1] // LANES):
        ab = _dot(z_ref[:, j * LANES:(j + 1) * LANES], cs_ref[...])
        parts.append(jnp.concatenate([ab[:, :LANES], ab[:, LANES:]], axis=0))
    st = jnp.concatenate(parts, axis=1).astype(BF16)
    y = _dot(m1_ref[...], st)
    y_ref[0] = y[:n1].astype(BF16)
    y_ref[1] = y[n1:].astype(BF16)


def _ft2_kernel(y_ref, tw_ref, o_ref, *, scale):
    for j in range(y_ref.shape[1]):
        yy = jnp.concatenate([y_ref[0, j], y_ref[1, j]], axis=0)
        o = _dot(tw_ref[j], yy) * scale
        o_ref[:, j * FT_WIDTH:(j + 1) * FT_WIDTH] = o.astype(BF16)


def _fourier_consts(s_len, cl):
    n2 = 128
    n1 = s_len // n2
    c = np.arange(HEAD_DIM)
    ang = 2 * np.pi * ((c[:, None] * c[None, :]) % HEAD_DIM) / HEAD_DIM
    cs = np.concatenate([np.cos(ang), np.sin(ang)], axis=1)
    a = np.arange(n1)
    ang1 = 2 * np.pi * ((a[:, None] * a[None, :]) % n1) / n1
    fc, fs = np.cos(ang1), np.sin(ang1)
    m1 = np.block([[fc, -fs], [-fs, -fc]])
    ka = jnp.arange(n1, dtype=jnp.int32)[:, None, None]
    kb = jnp.arange(n2, dtype=jnp.int32)[None, :, None]
    nn = jnp.arange(n2, dtype=jnp.int32)[None, None, :]
    ph = (nn * (ka + n1 * kb)) % s_len
    th = ph.astype(F32) * (2 * np.pi / s_len)
    tw = jnp.concatenate([jnp.cos(th), jnp.sin(th)], axis=2).astype(BF16)
    p = np.arange(cl)
    angc = 2 * np.pi * ((p[:, None] * p[None, :]) % cl) / cl
    mc = np.concatenate([np.cos(angc), -np.sin(angc)], axis=1)
    return (jnp.asarray(cs, BF16), jnp.asarray(m1, BF16), tw, jnp.asarray(mc, BF16))


def _fourier_latent(z_ft, cs, m1, tw, s_len):
    n2 = 128
    n1 = s_len // n2
    width = n2 * FT_WIDTH
    cw = 2048
    y = pl.pallas_call(
        _ft1_kernel,
        grid=(width // cw,),
        in_specs=[pl.BlockSpec((n1, cw), lambda j: (0, j)),
                  pl.BlockSpec((HEAD_DIM, 2 * HEAD_DIM), lambda j: (0, 0)),
                  pl.BlockSpec((2 * n1, 2 * n1), lambda j: (0, 0))],
        out_specs=pl.BlockSpec((2, n1, cw), lambda j: (0, 0, j)),
        out_shape=jax.ShapeDtypeStruct((2, n1, width), BF16),
        compiler_params=_cparams(("arbitrary",)),
        name="fourier_stage1",
    )(z_ft[:s_len].reshape(n1, width), cs, m1)
    kab = 4
    out = pl.pallas_call(
        functools.partial(_ft2_kernel, scale=(s_len * HEAD_DIM) ** -0.5),
        grid=(n1 // kab,),
        in_specs=[pl.BlockSpec((2, kab, n2, FT_WIDTH), lambda j: (0, j, 0, 0)),
                  pl.BlockSpec((kab, n2, 2 * n2), lambda j: (j, 0, 0))],
        out_specs=pl.BlockSpec((n2, kab * FT_WIDTH), lambda j: (0, j)),
        out_shape=jax.ShapeDtypeStruct((n2, n1 * FT_WIDTH), BF16),
        compiler_params=_cparams(("arbitrary",)),
        name="fourier_stage2",
    )(y.reshape(2, n1, n2, FT_WIDTH), tw)
    return out.reshape(s_len, FT_WIDTH)


def _ft_ctx_kernel(z_ref, cs_ref, mc_ref, o_ref, *, scale):
    for g in range(FT_GROUPS):
        ab = _dot(z_ref[:, g * LANES:(g + 1) * LANES], cs_ref[...])
        st = jnp.concatenate([ab[:, :LANES], ab[:, LANES:]], axis=0).astype(BF16)
        o_ref[:, g * LANES:(g + 1) * LANES] = (_dot(mc_ref[...], st) * scale).astype(BF16)


def _fourier_ctx(z_ft, cs, mc, s_len, cl):
    blk = s_len // cl
    return pl.pallas_call(
        functools.partial(_ft_ctx_kernel, scale=(cl * HEAD_DIM) ** -0.5),
        grid=(1,),
        in_specs=[pl.BlockSpec((cl, FT_WIDTH), lambda i: (blk, 0)),
                  pl.BlockSpec((HEAD_DIM, 2 * HEAD_DIM), lambda i: (0, 0)),
                  pl.BlockSpec((cl, 2 * cl), lambda i: (0, 0))],
        out_specs=pl.BlockSpec((cl, FT_WIDTH), lambda i: (0, 0)),
        out_shape=jax.ShapeDtypeStruct((cl, FT_WIDTH), BF16),
        compiler_params=_cparams(("arbitrary",)),
        name="fourier_ctx",
    )(z_ft, cs, mc)


def _out_kernel(*refs, with_ctx, n_lat_tiles):
    if with_ctx:
        (ona_l, ona_c, osg, oft_l, oft_c, x_ref, w_ref, g_ref, lng, lnb, sh_ref, sc_ref,
         wr_ref, br_ref, x1_ref, h2_ref, lg_ref, r_scr, hs_scr) = refs
        is_ctx = pl.program_id(0) >= n_lat_tiles
        ona = jnp.where(is_ctx, ona_c[...], ona_l[...])
        oft = jnp.where(is_ctx, oft_c[...], oft_l[...])
    else:
        (ona_l, osg, oft_l, x_ref, w_ref, g_ref, lng, lnb, sh_ref, sc_ref,
         wr_ref, br_ref, x1_ref, h2_ref, lg_ref, r_scr, hs_scr) = refs
        ona = ona_l[...]
        oft = oft_l[...]
    sg = osg[...]
    nw = 512
    for n in range(D_MODEL // nw):
        cols = slice(n * nw, (n + 1) * nw)
        y = (_dot(ona, w_ref[0:NA_WIDTH, cols])
             + _dot(sg, w_ref[NA_WIDTH:NA_WIDTH + SG_WIDTH, cols])
             + _dot(oft, w_ref[NA_WIDTH + SG_WIDTH:, cols]))
        r_scr[:, cols] = ALPHA * x_ref[:, cols] + g_ref[0][:, cols] * y

    def norm_rows(ci, carry):
        rows = pl.ds(pl.multiple_of(ci * NORM_ROWS, NORM_ROWS), NORM_ROWS)
        x1 = _ln_rows(r_scr[rows, :]) * lng[...] + lnb[...]
        x1_ref[rows, :] = x1
        h2 = _ln_rows(x1) * (1.0 + sc_ref[0]) + sh_ref[0]
        h2_ref[rows, :] = h2
        hi = h2.astype(BF16)
        hs_scr[0, rows, :] = hi
        hs_scr[1, rows, :] = (h2 - hi.astype(F32)).astype(BF16)
        return carry

    lax.fori_loop(0, r_scr.shape[0] // NORM_ROWS, norm_rows, 0, unroll=True)
    lg_ref[...] = (_dot(hs_scr[0], wr_ref[0]) + _dot(hs_scr[1], wr_ref[0])
                   + _dot(hs_scr[0], wr_ref[1]) + br_ref[...])


def _out_proj(ona_l, ona_c, osg, oft_l, oft_c, xall, w_out_bf, gate, lng, lnb, shift, scale,
              wr_split, br_pad, layer, n_rows, n_lat_tiles, with_ctx):
    tm = ROW_TILE
    row = lambda i: (i, 0)
    lat = lambda i: (jnp.minimum(i, n_lat_tiles - 1), 0)
    typ = lambda i: (jnp.where(i >= n_lat_tiles, 1, 0), 0, 0)
    const = lambda i: (0, 0)
    modspec = pl.BlockSpec((1, 1, D_MODEL), typ)
    vec = pl.BlockSpec((1, D_MODEL), const)
    specs = [pl.BlockSpec((tm, NA_WIDTH), lat)]
    args = [ona_l]
    if with_ctx:
        specs.append(pl.BlockSpec((tm, NA_WIDTH), const))
        args.append(ona_c)
    specs.append(pl.BlockSpec((tm, SG_WIDTH), row))
    args.append(osg)
    specs.append(pl.BlockSpec((tm, FT_WIDTH), lat))
    args.append(oft_l)
    if with_ctx:
        specs.append(pl.BlockSpec((tm, FT_WIDTH), const))
        args.append(oft_c)
    specs += [pl.BlockSpec((tm, D_MODEL), row),
              pl.BlockSpec((None, D_MODEL, D_MODEL), lambda i: (layer, 0, 0),
                           pipeline_mode=pl.Buffered(1)),
              modspec, vec, vec, modspec, modspec,
              pl.BlockSpec((None, 2, D_MODEL, LANES), lambda i: (layer, 0, 0, 0)),
              pl.BlockSpec((1, LANES), const)]
    args += [xall, w_out_bf, gate, lng, lnb, shift, scale, wr_split, br_pad]
    return pl.pallas_call(
        functools.partial(_out_kernel, with_ctx=with_ctx, n_lat_tiles=n_lat_tiles),
        grid=(n_rows // tm,),
        in_specs=specs,
        out_specs=[pl.BlockSpec((tm, D_MODEL), row),
                   pl.BlockSpec((tm, D_MODEL), row),
                   pl.BlockSpec((tm, LANES), row)],
        out_shape=[jax.ShapeDtypeStruct((n_rows, D_MODEL), F32),
                   jax.ShapeDtypeStruct((n_rows, D_MODEL), F32),
                   jax.ShapeDtypeStruct((n_rows, LANES), F32)],
        scratch_shapes=[pltpu.VMEM((tm, D_MODEL), F32),
                        pltpu.VMEM((2, tm, D_MODEL), BF16)],
        compiler_params=_cparams(("arbitrary",)),
        name="out_proj",
    )(*args)


def _route_kernel(lg_ref, tri_ref, eid_ref, rank_ref, gate_ref, cnt_ref, carry):
    @pl.when(pl.program_id(0) == 0)
    def _():
        carry[...] = jnp.zeros_like(carry)

    work = lg_ref[...]
    lane = lax.broadcasted_iota(jnp.int32, work.shape, 1)
    lane_f = lane.astype(F32)
    vals, ids, hots = [], [], []
    for _ in range(TOP_K):
        m = jnp.max(work, axis=-1, keepdims=True)
        idx = jnp.min(jnp.where(work == m, lane_f, float(LANES)), axis=-1, keepdims=True)
        hot = lane_f == idx
        vals.append(m)
        ids.append(idx)
        hots.append(hot)
        work = jnp.where(hot, -jnp.inf, work)
    exps = [jnp.exp(v - vals[0]) for v in vals]
    den = exps[0] + exps[1] + exps[2] + exps[3]
    multi = jnp.zeros(work.shape, F32)
    for hot in hots:
        multi = multi + hot.astype(F32)
    pref = _dot(tri_ref[...], multi.astype(BF16)) + carry[...]
    eid = jnp.zeros(work.shape, F32)
    rank = jnp.zeros(work.shape, F32)
    gate = jnp.zeros(work.shape, F32)
    for k in range(TOP_K):
        rk = jnp.sum(jnp.where(hots[k], pref, 0.0), axis=-1, keepdims=True)
        sel = lane == k
        eid = jnp.where(sel, ids[k], eid)
        rank = jnp.where(sel, rk, rank)
        gate = jnp.where(sel, exps[k] / den, gate)
    eid_ref[...] = eid.astype(jnp.int32)
    rank_ref[...] = rank.astype(jnp.int32)
    gate_ref[...] = gate
    carry[...] = carry[...] + jnp.sum(multi, axis=0, keepdims=True)
    cnt_ref[...] = jnp.broadcast_to(carry[...], cnt_ref.shape).astype(jnp.int32)


def _route(logits, tri):
    t = logits.shape[0]
    tm = ROW_TILE
    row = lambda i: (i, 0)
    return pl.pallas_call(
        _route_kernel,
        grid=(t // tm,),
        in_specs=[pl.BlockSpec((tm, LANES), row),
                  pl.BlockSpec((tm, tm), lambda i: (0, 0))],
        out_specs=[pl.BlockSpec((tm, LANES), row)] * 3
        + [pl.BlockSpec((8, LANES), lambda i: (0, 0))],
        out_shape=[jax.ShapeDtypeStruct((t, LANES), jnp.int32)] * 2
        + [jax.ShapeDtypeStruct((t, LANES), F32),
           jax.ShapeDtypeStruct((8, LANES), jnp.int32)],
        scratch_shapes=[pltpu.VMEM((1, LANES), F32)],
        compiler_params=_cparams(("arbitrary",)),
        name="route",
    )(logits, tri)


def _dispatch_kernel(pstart_ref, pcount_ref, h_ref, slot_ref, xb_ref, sem, pad_sem):
    tm = h_ref.shape[0]
    first = pl.program_id(0) == 0

    def row_copy(src_row, dst_row, s):
        return pltpu.make_async_copy(h_ref.at[pl.ds(src_row, 1)], xb_ref.at[pl.ds(dst_row, 1)], s)

    def tail_copy(b):
        return pltpu.make_async_copy(h_ref, xb_ref.at[pl.ds(b * MOE_BLOCK, MOE_BLOCK)], pad_sem)

    n_blocks = xb_ref.shape[0] // MOE_BLOCK
    n_used = pcount_ref[N_EXPERTS + 1]

    @pl.when(first)
    def _():
        def per_expert(e, carry):
            def per_row(j, c2):
                row_copy(0, pstart_ref[e] + j, pad_sem).start()
                return c2
            return lax.fori_loop(0, pcount_ref[e], per_row, carry)
        lax.fori_loop(0, N_EXPERTS, per_expert, 0)

        def per_tail(b, carry):
            tail_copy(b).start()
            return carry
        lax.fori_loop(n_used, n_blocks, per_tail, 0)

    def per_token(g, carry):
        for u in range(2):
            t = g * 2 + u
            for k in range(TOP_K):
                row_copy(t, slot_ref[t * TOP_K + k], sem).start(priority=k % 2)
        return carry

    lax.fori_loop(0, tm // 2, per_token, 0)
    for _ in range(TOP_K):
        pltpu.make_async_copy(h_ref, xb_ref.at[pl.ds(0, tm)], sem).wait()

    @pl.when(first)
    def _():
        def wait_row(j, carry):
            row_copy(0, 0, pad_sem).wait()
            return carry
        lax.fori_loop(0, pcount_ref[N_EXPERTS], wait_row, 0)

        def wait_tail(b, carry):
            tail_copy(b).wait()
            return carry
        lax.fori_loop(n_used, n_blocks, wait_tail, 0)


def _dispatch(pstart, pcount, h2, slot_flat, n_slots):
    t = h2.shape[0]
    tm = ROW_TILE
    gs = pltpu.PrefetchScalarGridSpec(
        num_scalar_prefetch=2,
        grid=(t // tm,),
        in_specs=[pl.BlockSpec((tm, D_MODEL), lambda i, a, b: (i, 0)),
                  pl.BlockSpec((tm * TOP_K,), lambda i, a, b: (i,), memory_space=pltpu.SMEM)],
        out_specs=pl.BlockSpec(memory_space=pl.ANY),
        scratch_shapes=[pltpu.SemaphoreType.DMA(()), pltpu.SemaphoreType.DMA(())],
    )
    return pl.pallas_call(
        _dispatch_kernel,
        grid_spec=gs,
        out_shape=jax.ShapeDtypeStruct((n_slots, D_MODEL), F32),
        compiler_params=_cparams(("arbitrary",), row_dma=True),
        name="dispatch",
    )(pstart, pcount, h2, slot_flat)


def _expert_kernel(bexp_ref, nused_ref, group_ref, next_ref, xb_ref, wgu_hbm, bgu_ref, wd_hbm,
                   bd_ref, perm_ref, y_ref, wgu_f, wd_f, wgu_s, wd_s, wsem, *, layer):
    b = pl.program_id(0)
    active = b < nused_ref[0]
    blk = xb_ref.shape[0]
    expert = bexp_ref[b]
    fresh = jnp.logical_or(b == 0, expert != bexp_ref[jnp.maximum(b - 1, 0)])
    half = group_ref[b] % 2

    def weight_copies(e, buf):
        return (pltpu.make_async_copy(wgu_hbm.at[layer, e], wgu_f.at[buf], wsem.at[buf, 0]),
                pltpu.make_async_copy(wd_hbm.at[layer, e], wd_f.at[buf], wsem.at[buf, 1]))

    @pl.when(jnp.logical_not(active))
    def _():
        y_ref[...] = jnp.zeros_like(y_ref)

    @pl.when(jnp.logical_and(active, b == 0))
    def _():
        for cp in weight_copies(expert, 0):
            cp.start()

    @pl.when(jnp.logical_and(active, fresh))
    def _():
        for cp in weight_copies(expert, half):
            cp.wait()
        wgu_s[...] = wgu_f[half].astype(BF16)
        wd_s[...] = _dot(perm_ref[...], wd_f[half].astype(BF16)).astype(BF16)

        @pl.when(next_ref[b] >= 0)
        def _():
            for cp in weight_copies(next_ref[b], 1 - half):
                cp.start()

    @pl.when(active)
    def _():
        gu = _dot(xb_ref[...].astype(BF16), wgu_s[...]) + bgu_ref[...]
        lane = lax.broadcasted_iota(jnp.int32, (blk, LANES), 1)
        even = (lane % 2) == 0
        prods = []
        for c in range(2 * D_FF // LANES):
            guc = gu[:, c * LANES:(c + 1) * LANES]
            glu = jnp.minimum(guc, SWIGLU_LIMIT)
            lin = jnp.clip(guc, -SWIGLU_LIMIT, SWIGLU_LIMIT) + 1.0
            prods.append(glu * jax.nn.sigmoid(SWIGLU_ALPHA * glu)
                         * pltpu.roll(lin, LANES - 1, axis=1))
        merged = [jnp.where(even, prods[2 * m], pltpu.roll(prods[2 * m + 1], 1, axis=1))
                  for m in range(D_FF // LANES)]
        act = jnp.concatenate(merged, axis=1).astype(BF16)
        y_ref[...] = _dot(act, wd_s[...]) + bd_ref[...]


def _experts(bexp, nused, group, next_expert, xb, w_gate_up, b_gate_up, w_down, b_down, layer,
             n_blocks):
    blk = MOE_BLOCK

    def bmap(b, be, nu, gr, nx):
        return (jnp.minimum(b, nu[0] - 1), 0)

    def emap(b, be, nu, gr, nx):
        return (layer, be[jnp.minimum(b, nu[0] - 1)], 0, 0)

    gs = pltpu.PrefetchScalarGridSpec(
        num_scalar_prefetch=4,
        grid=(n_blocks,),
        in_specs=[pl.BlockSpec((blk, D_MODEL), bmap),
                  pl.BlockSpec(memory_space=pl.ANY),
                  pl.BlockSpec((None, None, 1, 2 * D_FF), emap),
                  pl.BlockSpec(memory_space=pl.ANY),
                  pl.BlockSpec((None, None, 1, D_MODEL), emap),
                  pl.BlockSpec((D_FF, D_FF), lambda b, be, nu, gr, nx: (0, 0))],
        out_specs=pl.BlockSpec((blk, D_MODEL), lambda b, be, nu, gr, nx: (b, 0)),
        scratch_shapes=[pltpu.VMEM((2, D_MODEL, 2 * D_FF), F32),
                        pltpu.VMEM((2, D_FF, D_MODEL), F32),
                        pltpu.VMEM((D_MODEL, 2 * D_FF), BF16),
                        pltpu.VMEM((D_FF, D_MODEL), BF16),
                        pltpu.SemaphoreType.DMA((2, 2))],
    )
    lane = np.arange(D_FF) % LANES
    unit = (np.arange(D_FF) // LANES) * LANES + lane // 2 + (lane % 2) * (LANES // 2)
    perm = np.zeros((D_FF, D_FF), np.float32)
    perm[np.arange(D_FF), unit] = 1.0
    return pl.pallas_call(
        functools.partial(_expert_kernel, layer=layer),
        grid_spec=gs,
        out_shape=jax.ShapeDtypeStruct(xb.shape, F32),
        compiler_params=_cparams(("arbitrary",)),
        name="experts",
    )(bexp, nused, group, next_expert, xb, w_gate_up, b_gate_up[:, :, None, :],
      w_down, b_down[:, :, None, :], jnp.asarray(perm, BF16))


def _combine_kernel(x1_ref, slot_ref, slot_next_ref, gate_ref, yb_ref, g_ref, lng, lnb,
                    o_ref, gbuf, sems):
    tm = x1_ref.shape[0]
    i = pl.program_id(0)
    cur = i % 2

    def row_copy(src_row, buf, k, t):
        return pltpu.make_async_copy(yb_ref.at[pl.ds(src_row, 1)],
                                     gbuf.at[buf, k, pl.ds(t, 1)], sems.at[buf])

    def gather(slots, buf):
        def per_token(g, carry):
            for u in range(2):
                t = g * 2 + u
                for k in range(TOP_K):
                    row_copy(slots[t * TOP_K + k], buf, k, t).start(priority=k % 2)
            return carry
        lax.fori_loop(0, tm // 2, per_token, 0)

    @pl.when(i == 0)
    def _():
        gather(slot_ref, 0)

    @pl.when(i + 1 < pl.num_programs(0))
    def _():
        gather(slot_next_ref, 1 - cur)

    for k in range(TOP_K):
        pltpu.make_async_copy(yb_ref.at[pl.ds(0, tm)], gbuf.at[cur, k], sems.at[cur]).wait()

    def norm_rows(ci, carry):
        rows = pl.ds(pl.multiple_of(ci * NORM_ROWS, NORM_ROWS), NORM_ROWS)
        gate = gate_ref[rows, :]
        f = gate[:, 0:1] * gbuf[cur, 0, rows, :]
        for k in range(1, TOP_K):
            f = f + gate[:, k:k + 1] * gbuf[cur, k, rows, :]
        r = ALPHA * x1_ref[rows, :] + g_ref[0] * f
        o_ref[rows, :] = _ln_rows(r) * lng[...] + lnb[...]
        return carry

    lax.fori_loop(0, tm // NORM_ROWS, norm_rows, 0, unroll=True)


def _combine(x1, slot_flat, gates, yb, gate_mod, lng, lnb, n_lat_tiles):
    t = x1.shape[0]
    tm = ROW_TILE
    n_tiles = t // tm
    return pl.pallas_call(
        _combine_kernel,
        grid=(n_tiles,),
        in_specs=[pl.BlockSpec((tm, D_MODEL), lambda i: (i, 0)),
                  pl.BlockSpec((tm * TOP_K,), lambda i: (i,), memory_space=pltpu.SMEM),
                  pl.BlockSpec((tm * TOP_K,), lambda i: (jnp.minimum(i + 1, n_tiles - 1),),
                               memory_space=pltpu.SMEM),
                  pl.BlockSpec((tm, LANES), lambda i: (i, 0)),
                  pl.BlockSpec(memory_space=pl.ANY),
                  pl.BlockSpec((1, 1, D_MODEL),
                               lambda i: (jnp.where(i >= n_lat_tiles, 1, 0), 0, 0)),
                  pl.BlockSpec((1, D_MODEL), lambda i: (0, 0)),
                  pl.BlockSpec((1, D_MODEL), lambda i: (0, 0))],
        out_specs=pl.BlockSpec((tm, D_MODEL), lambda i: (i, 0)),
        out_shape=jax.ShapeDtypeStruct((t, D_MODEL), F32),
        scratch_shapes=[pltpu.VMEM((2, TOP_K, tm, D_MODEL), F32),
                        pltpu.SemaphoreType.DMA((2,))],
        compiler_params=_cparams(("arbitrary",), row_dma=True),
        name="combine",
    )(x1, slot_flat, slot_flat, gates, yb, gate_mod, lng, lnb)


def _rope_tables(s_len, cl):
    t = jnp.arange(s_len, dtype=jnp.int32)
    quarter = HEAD_DIM // 4
    inv = ROPE_BASE ** (-jnp.arange(quarter, dtype=F32) / quarter)
    ang_r = (t // GRID_W).astype(F32)[:, None] * inv
    ang_c = (t % GRID_W).astype(F32)[:, None] * inv
    cr, sr, cc, sc = jnp.cos(ang_r), jnp.sin(ang_r), jnp.cos(ang_c), jnp.sin(ang_c)
    cos_t = jnp.concatenate([cr, cr, cc, cc], axis=1)
    sin_t = jnp.concatenate([-sr, sr, -sc, sc], axis=1)
    cos_t = jnp.concatenate([cos_t, jnp.ones((cl, HEAD_DIM), F32)], axis=0)
    sin_t = jnp.concatenate([sin_t, jnp.zeros((cl, HEAD_DIM), F32)], axis=0)
    return cos_t, sin_t


def _block_plan(counts, eid, rank, n_blocks):
    blocks_per = (counts + MOE_BLOCK - 1) // MOE_BLOCK
    block_end = jnp.cumsum(blocks_per)
    base = (block_end - blocks_per) * MOE_BLOCK
    blocks = jnp.arange(n_blocks)
    bexp = jnp.minimum(jnp.sum(block_end[None, :] <= blocks[:, None], axis=1),
                       N_EXPERTS - 1).astype(jnp.int32)
    nused = block_end[-1:].astype(jnp.int32)
    experts = jnp.arange(N_EXPERTS, dtype=jnp.int32)
    first = jnp.concatenate([jnp.ones((1,), bool), bexp[1:] != bexp[:-1]])
    group = (jnp.cumsum(first) - 1).astype(jnp.int32)
    group_end = jnp.sum(jnp.where(bexp[:, None] == experts, block_end, 0), axis=-1)
    follower = jnp.sum(jnp.where(blocks[None, :] == group_end[:, None], bexp[None, :], 0), axis=-1)
    next_expert = jnp.where(group_end < nused[0], follower, -1).astype(jnp.int32)
    slot = rank + jnp.sum(jnp.where(eid[:, :, None] == experts, base, 0), axis=-1)
    pad = blocks_per * MOE_BLOCK - counts
    pcount = jnp.concatenate([pad, jnp.sum(pad)[None], nused]).astype(jnp.int32)
    pstart = (base + counts).astype(jnp.int32)
    return (bexp, nused, group, next_expert, slot.reshape(-1).astype(jnp.int32), pstart,
            pcount)


def kernel(x, c, ctx, c_ctx, w_ada, b_ada, w_in, w_out, sg_w, sg_b, na_rpb, ln1_g, ln1_b,
           ln2_g, ln2_b, w_router, b_router, w_gate_up, b_gate_up, w_down, b_down):
    bsz, s_len, dm = x.shape
    cl = ctx.shape[1]
    depth = w_ada.shape[0]
    assert bsz == 1 and dm == D_MODEL and cl == ROW_TILE and MOE_BLOCK == ROW_TILE
    assert s_len % (GRID_W * KEY_ROWS) == 0 and s_len % ROW_TILE == 0
    t_all = s_len + cl
    n_lat_tiles = s_len // ROW_TILE
    rows = s_len // GRID_W

    cond = jnp.zeros((8, dm), F32).at[0].set(c[0]).at[1].set(c_ctx)
    mod = _ada_mod(cond, w_ada, b_ada)[:, :2].reshape(depth, 2, 6, 1, dm)
    cos_t, sin_t = _rope_tables(s_len, cl)
    pb, vrow = _na_bias_tables(na_rpb, rows)
    cs, m1, tw, mc = _fourier_consts(s_len, cl)
    tri = jnp.asarray(np.tril(np.ones((ROW_TILE, ROW_TILE), np.float32), -1), BF16)
    wr_pad = jnp.pad(w_router, ((0, 0), (0, 0), (0, LANES - N_EXPERTS)))
    wr_hi = wr_pad.astype(BF16)
    wr_split = jnp.stack([wr_hi, (wr_pad - wr_hi.astype(F32)).astype(BF16)], axis=1)
    br_pad = jnp.pad(b_router, ((0, 0), (0, LANES - N_EXPERTS)), constant_values=NEG_INF)
    w_in_bf = w_in.astype(BF16)
    w_out_bf = w_out.astype(BF16)
    sgw_bf = sg_w.astype(BF16)

    xall = jnp.concatenate([x[0], ctx[0]], axis=0)
    for l in range(depth):
        last = l == depth - 1
        m = lambda j: mod[l, :, j]
        qp, qr, kr, v, z_sg, z_ft = _proj(xall, m(0), m(1), w_in_bf, cos_t, sin_t, l,
                                          n_lat_tiles)
        ona_l = _na_attention(qr, qp, kr, v, pb, vrow, l, s_len)
        osg = _spatial_gating(z_sg, sgw_bf[l], sg_b[l][:, :, None])
        oft_l = _fourier_latent(z_ft, cs, m1, tw, s_len)
        if last:
            ona_c = oft_c = None
            n_rows = s_len
        else:
            ona_c = _ctx_attention(qp, kr, v, s_len, cl)
            oft_c = _fourier_ctx(z_ft, cs, mc, s_len, cl)
            n_rows = t_all
        x1, h2, logits = _out_proj(
            ona_l, ona_c, osg, oft_l, oft_c, xall, w_out_bf, m(2),
            ln1_g[l][None], ln1_b[l][None], m(3), m(4), wr_split, br_pad[l][None],
            l, n_rows, n_lat_tiles, not last)
        eid, rank, gates, counts = _route(logits, tri)
        n_blocks = -(-n_rows * TOP_K // MOE_BLOCK) + N_EXPERTS
        bexp, nused, group, next_expert, slot_flat, pstart, pcount = _block_plan(
            counts[0, :N_EXPERTS], eid[:, :TOP_K], rank[:, :TOP_K], n_blocks)
        xb = _dispatch(pstart, pcount, h2, slot_flat, n_blocks * MOE_BLOCK)
        yb = _experts(bexp, nused, group, next_expert, xb, w_gate_up, b_gate_up, w_down, b_down,
                      l, n_blocks)
        xall = _combine(x1, slot_flat, gates, yb, m(5), ln2_g[l][None], ln2_b[l][None],
                        n_lat_tiles)
    return xall[None]
```

```python
import functools
import math

import numpy as np
import jax
import jax.numpy as jnp
from jax import lax
from jax.experimental import pallas as pl
from jax.experimental.pallas import tpu as pltpu

D_MODEL = 2048
DEPTH_NORM = 4
GRID_W = 64
HEAD_DIM = 128
NA_HEADS = 8
NA_WIDTH = NA_HEADS * HEAD_DIM
NA_KH = 8
NA_KW = 16
SG_GROUPS = 4
SG_WIDTH = 512
SG_CHUNK = 128
FT_GROUPS = 4
FT_WIDTH = 512
SG_OFF = 3 * NA_WIDTH
FT_OFF = SG_OFF + 2 * SG_WIDTH
IN_WIDTH = FT_OFF + FT_WIDTH
ROPE_BASE = 10000.0
N_EXPERTS = 32
TOP_K = 4
D_FF = D_MODEL // 4
SWIGLU_ALPHA = 1.702
SWIGLU_LIMIT = 7.0
LN_EPS = 1e-5
NEG_INF = -1e30
ALPHA = (2 * DEPTH_NORM) ** 0.25

LANES = 128
ROW_TILE = 256
MOE_BLOCK = 256
SLOT_BLOCKS = 4
NORM_ROWS = 32
KEY_ROWS = 16
Q_ROWS = 8
NA_PAIRS = 5
VMEM_LIMIT = 56 * 1024 * 1024

F32 = jnp.float32
BF16 = jnp.bfloat16


def _cparams(sem, row_dma=False):
    return pltpu.CompilerParams(dimension_semantics=sem, vmem_limit_bytes=VMEM_LIMIT,
                                disable_bounds_checks=row_dma)


def _dot(a, b):
    return jnp.dot(a, b, preferred_element_type=F32)


def _dot_nt(a, b):
    return lax.dot_general(a, b, (((1,), (1,)), ((), ())), preferred_element_type=F32)


def _pack_halves(x):
    n = x.shape[1] // 2
    hi = lax.bitcast_convert_type(x[:, :n], jnp.uint32)
    lo = lax.bitcast_convert_type(x[:, n:], jnp.uint32)
    return hi | (lo >> 16)


def _unpack_halves(w):
    first = lax.bitcast_convert_type(w & jnp.uint32(0xFFFF0000), F32)
    second = lax.bitcast_convert_type(w << 16, F32)
    return first, second


def _ln_rows(x):
    mu = jnp.mean(x, axis=-1, keepdims=True)
    xc = x - mu
    var = jnp.mean(xc * xc, axis=-1, keepdims=True)
    return xc * lax.rsqrt(var + LN_EPS)


def _ada_kernel(c_ref, w_ref, b_ref, o_ref):
    c = c_ref[...]
    s = c * jax.nn.sigmoid(c)
    o_ref[0] = jnp.dot(s, w_ref[0], preferred_element_type=F32,
                       precision=lax.Precision.HIGHEST) + b_ref[0]


def _ada_mod(cond, w_ada, b_ada):
    depth, d, n = w_ada.shape
    tn = 1536
    return pl.pallas_call(
        _ada_kernel,
        grid=(depth, n // tn),
        in_specs=[pl.BlockSpec((8, d), lambda l, j: (0, 0)),
                  pl.BlockSpec((1, d, tn), lambda l, j: (l, 0, j)),
                  pl.BlockSpec((1, 1, tn), lambda l, j: (l, 0, j))],
        out_specs=pl.BlockSpec((1, 8, tn), lambda l, j: (l, 0, j)),
        out_shape=jax.ShapeDtypeStruct((depth, 8, n), F32),
        compiler_params=_cparams(("arbitrary", "arbitrary")),
        name="ada_mod",
    )(cond, w_ada, b_ada.reshape(depth, 1, n))


def _proj_kernel(x_ref, sh_ref, sc_ref, w_ref, cos_ref, sin_ref,
                 qp_ref, qr_ref, kr_ref, v_ref, sg_ref, ft_ref):
    y = _ln_rows(x_ref[...])
    h = (y * (1.0 + sc_ref[0]) + sh_ref[0]).astype(BF16)
    cos = cos_ref[...]
    sin = sin_ref[...]
    lane = lax.broadcasted_iota(jnp.int32, cos.shape, 1)
    first = (lane % 64) < 32

    def rope(z):
        swapped = jnp.where(first, pltpu.roll(z, 96, axis=1), pltpu.roll(z, 32, axis=1))
        return z * cos + swapped * sin

    nw = 512
    for j in range(IN_WIDTH // nw):
        z = _dot(h, w_ref[:, j * nw:(j + 1) * nw])
        for p in range(nw // LANES):
            col = j * nw + p * LANES
            zp = z[:, p * LANES:(p + 1) * LANES]
            if col < NA_WIDTH:
                zp = zp * (HEAD_DIM ** -0.5)
                qp_ref[:, col:col + LANES] = zp.astype(BF16)
                qr_ref[:, col:col + LANES] = rope(zp).astype(BF16)
            elif col < 2 * NA_WIDTH:
                c0 = col - NA_WIDTH
                kr_ref[:, c0:c0 + LANES] = rope(zp).astype(BF16)
            elif col < SG_OFF:
                c0 = col - 2 * NA_WIDTH
                v_ref[:, c0:c0 + LANES] = zp.astype(BF16)
            elif col < FT_OFF:
                c0 = col - SG_OFF
                sg_ref[:, c0:c0 + LANES] = zp
            else:
                c0 = col - FT_OFF
                ft_ref[:, c0:c0 + LANES] = zp.astype(BF16)


def _proj(xall, shift, scale, w_in_bf, cos_t, sin_t, layer, n_lat_tiles):
    t = xall.shape[0]
    tm = ROW_TILE
    typ = lambda i: (jnp.where(i >= n_lat_tiles, 1, 0), 0, 0)
    row = lambda i: (i, 0)
    return pl.pallas_call(
        _proj_kernel,
        grid=(t // tm,),
        in_specs=[pl.BlockSpec((tm, D_MODEL), row),
                  pl.BlockSpec((1, 1, D_MODEL), typ),
                  pl.BlockSpec((1, 1, D_MODEL), typ),
                  pl.BlockSpec((None, D_MODEL, IN_WIDTH), lambda i: (layer, 0, 0),
                               pipeline_mode=pl.Buffered(1)),
                  pl.BlockSpec((tm, LANES), row),
                  pl.BlockSpec((tm, LANES), row)],
        out_specs=[pl.BlockSpec((tm, NA_WIDTH), row)] * 4
        + [pl.BlockSpec((tm, 2 * SG_WIDTH), row), pl.BlockSpec((tm, FT_WIDTH), row)],
        out_shape=[jax.ShapeDtypeStruct((t, NA_WIDTH), BF16)] * 4
        + [jax.ShapeDtypeStruct((t, 2 * SG_WIDTH), F32),
           jax.ShapeDtypeStruct((t, FT_WIDTH), BF16)],
        compiler_params=_cparams(("arbitrary",)),
        name="ln_proj",
    )(xall, shift, scale, w_in_bf, cos_t, sin_t)


def _na_kernel(qr_ref, qp_ref, k0, k1, k2, k3, v0, v1, v2, v3, kc_ref, vc_ref, pb_ref, vrow_ref,
               o_ref, k_scr, v_scr, s_scr, p_scr, den_scr, *, n_rb):
    i = pl.program_id(1)
    is_first = i == 0
    is_last = i == n_rb - 1
    off = jnp.where(is_first, 0, jnp.where(is_last, -(KEY_ROWS - Q_ROWS), -(NA_KH // 2)))
    kb = k0.shape[0]
    for j, (kj, vj) in enumerate(((k0, v0), (k1, v1), (k2, v2), (k3, v3))):
        k_scr[j * kb:(j + 1) * kb, :] = kj[...]
        v_scr[j * kb:(j + 1) * kb, :] = vj[...]
    kc = kc_ref[...]
    vc = vc_ref[...]
    qsub = 128
    pair = 2 * GRID_W
    starts = ((0, 0, 0, 1), (0, 1, 2, 3), (2, 3, 3, 3))
    n_sub = Q_ROWS * GRID_W // qsub
    n_lat = NA_PAIRS * pair
    key0 = []
    for s in range(n_sub):
        rows = slice(s * qsub, (s + 1) * qsub)
        p0 = jnp.where(is_first, starts[0][s], jnp.where(is_last, starts[2][s], starts[1][s]))
        key0.append(pl.multiple_of(p0 * pair, pair))
        bias_rows = []
        for qr in range(s * qsub // GRID_W, (s + 1) * qsub // GRID_W):
            tiles = [pb_ref[0, 0, jnp.clip(2 * (p0 + j) - qr + off + NA_KH, 0, 2 * NA_KH - 1)]
                     + vrow_ref[0, qr, p0 + j]
                     for j in range(NA_PAIRS)]
            bias_rows.append(jnp.concatenate(tiles, axis=1))
        bias = jnp.concatenate(bias_rows, axis=0)
        s_scr[s, :, 0:n_lat] = _dot_nt(qr_ref[rows, :], k_scr[pl.ds(key0[s], n_lat), :]) + bias
        s_scr[s, :, n_lat:] = _dot_nt(qp_ref[rows, :], kc)
    for s in range(n_sub):
        sc = s_scr[s]
        p = jnp.exp(sc - jnp.max(sc, axis=-1, keepdims=True))
        p_scr[s] = p.astype(BF16)
        den_scr[s] = jnp.sum(p, axis=-1, keepdims=True)
    for s in range(n_sub):
        rows = slice(s * qsub, (s + 1) * qsub)
        o = (_dot(p_scr[s, :, 0:n_lat], v_scr[pl.ds(key0[s], n_lat), :])
             + _dot(p_scr[s, :, n_lat:], vc))
        o_ref[rows, :] = (o / den_scr[s]).astype(BF16)


def _na_attention(qr, qp, kr, v, pb, vrow, layer, s_len):
    rows = s_len // GRID_W
    n_rb = rows // Q_ROWS
    qb = Q_ROWS * GRID_W
    kb = 256
    last_kblock = (rows - KEY_ROWS) * GRID_W // kb
    ctx_block = s_len // kb

    def kmap(j):
        return lambda h, i: (jnp.clip(2 * i - 1, 0, last_kblock) + j, h)

    def btype(i):
        return jnp.where(i == 0, 0, jnp.where(i == n_rb - 1, 2, 1))

    qspec = pl.BlockSpec((qb, HEAD_DIM), lambda h, i: (i, h))
    kspecs = [pl.BlockSpec((kb, HEAD_DIM), kmap(j)) for j in range(4)]
    cspec = pl.BlockSpec((kb, HEAD_DIM), lambda h, i: (ctx_block, h))
    return pl.pallas_call(
        functools.partial(_na_kernel, n_rb=n_rb),
        grid=(NA_HEADS, n_rb),
        in_specs=[qspec, qspec] + kspecs + kspecs + [cspec, cspec]
        + [pl.BlockSpec((1, 1, 2 * NA_KH, GRID_W, 2 * GRID_W), lambda h, i: (layer, h, 0, 0, 0)),
           pl.BlockSpec((1, Q_ROWS, KEY_ROWS // 2, 1, 2 * GRID_W),
                        lambda h, i: (btype(i), 0, 0, 0, 0))],
        out_specs=pl.BlockSpec((qb, HEAD_DIM), lambda h, i: (i, h)),
        out_shape=jax.ShapeDtypeStruct((s_len, NA_WIDTH), BF16),
        scratch_shapes=[pltpu.VMEM((KEY_ROWS * GRID_W, HEAD_DIM), BF16),
                        pltpu.VMEM((KEY_ROWS * GRID_W, HEAD_DIM), BF16),
                        pltpu.VMEM((qb // 128, 128, NA_PAIRS * 2 * GRID_W + kb), F32),
                        pltpu.VMEM((qb // 128, 128, NA_PAIRS * 2 * GRID_W + kb), BF16),
                        pltpu.VMEM((qb // 128, 128, 1), F32)],
        compiler_params=_cparams(("arbitrary", "arbitrary")),
        name="na_attention",
    )(qr, qp, kr, kr, kr, kr, v, v, v, v, kr, v, pb, vrow)


def _ctx_attn_kernel(q_ref, k_ref, v_ref, o_ref):
    s = _dot_nt(q_ref[...], k_ref[...])
    m = jnp.max(s, axis=-1, keepdims=True)
    p = jnp.exp(s - m)
    den = jnp.sum(p, axis=-1, keepdims=True)
    o_ref[...] = (_dot(p.astype(BF16), v_ref[...]) / den).astype(BF16)


def _ctx_attention(qp, kr, v, s_len, cl):
    blk = s_len // cl
    spec = pl.BlockSpec((cl, HEAD_DIM), lambda h: (blk, h))
    return pl.pallas_call(
        _ctx_attn_kernel,
        grid=(NA_HEADS,),
        in_specs=[spec, spec, spec],
        out_specs=pl.BlockSpec((cl, HEAD_DIM), lambda h: (0, h)),
        out_shape=jax.ShapeDtypeStruct((cl, NA_WIDTH), BF16),
        compiler_params=_cparams(("arbitrary",)),
        name="ctx_attention",
    )(qp, kr, v)


def _na_bias_tables(na_rpb, rows):
    n_rb = rows // Q_ROWS
    rmask = np.zeros((3, Q_ROWS, KEY_ROWS), bool)
    for t, i in enumerate((0, 1, n_rb - 1)):
        ks = min(max(Q_ROWS * i - NA_KH // 2, 0), rows - KEY_ROWS)
        for qr in range(Q_ROWS):
            r = Q_ROWS * i + qr
            lo = min(max(r - NA_KH // 2, 0), rows - NA_KH)
            for kr in range(KEY_ROWS):
                rmask[t, qr, kr] = lo <= ks + kr < lo + NA_KH
    vrow = np.where(np.repeat(rmask, GRID_W, axis=2), 0.0, NEG_INF).astype(np.float32)
    csel = np.zeros((GRID_W, GRID_W, 2 * NA_KW - 1), np.float32)
    cmask = np.zeros((GRID_W, GRID_W), bool)
    for qc in range(GRID_W):
        lo = min(max(qc - NA_KW // 2, 0), GRID_W - NA_KW)
        for kc in range(GRID_W):
            cmask[qc, kc] = lo <= kc < lo + NA_KW
            csel[qc, kc, min(max(kc - qc + NA_KW - 1, 0), 2 * NA_KW - 2)] = 1.0
    b = jnp.einsum('lhab,qkb->lhaqk', na_rpb, jnp.asarray(csel), precision=lax.Precision.HIGHEST)
    b = jnp.where(jnp.asarray(cmask), b, NEG_INF)
    b = jnp.pad(b, ((0, 0), (0, 0), (1, 1), (0, 0), (0, 0)))
    pb = jnp.concatenate([b[:, :, :-1], b[:, :, 1:]], axis=-1)
    return pb, jnp.asarray(vrow).reshape(3, Q_ROWS, KEY_ROWS // 2, 1, 2 * GRID_W)


def _sg_kernel(z_ref, w_ref, b_ref, o_ref):
    z = z_ref[...]
    g = 0.5 * z * (1.0 + lax.erf(z * (2.0 ** -0.5)))
    n_chunks = z.shape[0] // SG_CHUNK
    for gi in range(SG_GROUPS):
        u = g[:, gi * LANES:(gi + 1) * LANES]
        vn = _ln_rows(g[:, SG_WIDTH + gi * LANES:SG_WIDTH + (gi + 1) * LANES]).astype(BF16)
        for n in range(n_chunks):
            rows = slice(n * SG_CHUNK, (n + 1) * SG_CHUNK)
            t = _dot(w_ref[gi], vn[rows, :]) + b_ref[gi]
            o_ref[rows, gi * LANES:(gi + 1) * LANES] = (u[rows, :] * t).astype(BF16)


def _spatial_gating(z_sg, sgw_bf, sgb):
    t = z_sg.shape[0]
    tm = ROW_TILE
    return pl.pallas_call(
        _sg_kernel,
        grid=(t // tm,),
        in_specs=[pl.BlockSpec((tm, 2 * SG_WIDTH), lambda i: (i, 0)),
                  pl.BlockSpec((SG_GROUPS, SG_CHUNK, SG_CHUNK), lambda i: (0, 0, 0)),
                  pl.BlockSpec((SG_GROUPS, SG_CHUNK, 1), lambda i: (0, 0, 0))],
        out_specs=pl.BlockSpec((tm, SG_WIDTH), lambda i: (i, 0)),
        out_shape=jax.ShapeDtypeStruct((t, SG_WIDTH), BF16),
        compiler_params=_cparams(("arbitrary",)),
        name="spatial_gating",
    )(z_sg, sgw_bf, sgb)


def _ft1_kernel(z_ref, cs_ref, m1_ref, y_ref):
    n1 = z_ref.shape[0]
    parts = []
    for j in range(z_ref.shape[1] // LANES):
        ab = _dot(z_ref[:, j * LANES:(j + 1) * LANES], cs_ref[...])
        parts.append(jnp.concatenate([ab[:, :LANES], ab[:, LANES:]], axis=0))
    st = jnp.concatenate(parts, axis=1).astype(BF16)
    y = _dot(m1_ref[...], st)
    y_ref[0] = y[:n1].astype(BF16)
    y_ref[1] = y[n1:].astype(BF16)


def _ft2_kernel(y_ref, tw_ref, o_ref, *, scale):
    for j in range(y_ref.shape[1]):
        yy = jnp.concatenate([y_ref[0, j], y_ref[1, j]], axis=0)
        o = _dot(tw_ref[j], yy) * scale
        o_ref[:, j * FT_WIDTH:(j + 1) * FT_WIDTH] = o.astype(BF16)


def _fourier_consts(s_len, cl):
    n2 = 128
    n1 = s_len // n2
    c = np.arange(HEAD_DIM)
    ang = 2 * np.pi * ((c[:, None] * c[None, :]) % HEAD_DIM) / HEAD_DIM
    cs = np.concatenate([np.cos(ang), np.sin(ang)], axis=1)
    a = np.arange(n1)
    ang1 = 2 * np.pi * ((a[:, None] * a[None, :]) % n1) / n1
    fc, fs = np.cos(ang1), np.sin(ang1)
    m1 = np.block([[fc, -fs], [-fs, -fc]])
    ka = jnp.arange(n1, dtype=jnp.int32)[:, None, None]
    kb = jnp.arange(n2, dtype=jnp.int32)[None, :, None]
    nn = jnp.arange(n2, dtype=jnp.int32)[None, None, :]
    ph = (nn * (ka + n1 * kb)) % s_len
    th = ph.astype(F32) * (2 * np.pi / s_len)
    tw = jnp.concatenate([jnp.cos(th), jnp.sin(th)], axis=2).astype(BF16)
    p = np.arange(cl)
    angc = 2 * np.pi * ((p[:, None] * p[None, :]) % cl) / cl
    mc = np.concatenate([np.cos(angc), -np.sin(angc)], axis=1)
    return (jnp.asarray(cs, BF16), jnp.asarray(m1, BF16), tw, jnp.asarray(mc, BF16))


def _fourier_latent(z_ft, cs, m1, tw, s_len):
    n2 = 128
    n1 = s_len // n2
    width = n2 * FT_WIDTH
    cw = 2048
    y = pl.pallas_call(
        _ft1_kernel,
        grid=(width // cw,),
        in_specs=[pl.BlockSpec((n1, cw), lambda j: (0, j)),
                  pl.BlockSpec((HEAD_DIM, 2 * HEAD_DIM), lambda j: (0, 0)),
                  pl.BlockSpec((2 * n1, 2 * n1), lambda j: (0, 0))],
        out_specs=pl.BlockSpec((2, n1, cw), lambda j: (0, 0, j)),
        out_shape=jax.ShapeDtypeStruct((2, n1, width), BF16),
        compiler_params=_cparams(("arbitrary",)),
        name="fourier_stage1",
    )(z_ft[:s_len].reshape(n1, width), cs, m1)
    kab = 4
    out = pl.pallas_call(
        functools.partial(_ft2_kernel, scale=(s_len * HEAD_DIM) ** -0.5),
        grid=(n1 // kab,),
        in_specs=[pl.BlockSpec((2, kab, n2, FT_WIDTH), lambda j: (0, j, 0, 0)),
                  pl.BlockSpec((kab, n2, 2 * n2), lambda j: (j, 0, 0))],
        out_specs=pl.BlockSpec((n2, kab * FT_WIDTH), lambda j: (0, j)),
        out_shape=jax.ShapeDtypeStruct((n2, n1 * FT_WIDTH), BF16),
        compiler_params=_cparams(("arbitrary",)),
        name="fourier_stage2",
    )(y.reshape(2, n1, n2, FT_WIDTH), tw)
    return out.reshape(s_len, FT_WIDTH)


def _ft_ctx_kernel(z_ref, cs_ref, mc_ref, o_ref, *, scale):
    for g in range(FT_GROUPS):
        ab = _dot(z_ref[:, g * LANES:(g + 1) * LANES], cs_ref[...])
        st = jnp.concatenate([ab[:, :LANES], ab[:, LANES:]], axis=0).astype(BF16)
        o_ref[:, g * LANES:(g + 1) * LANES] = (_dot(mc_ref[...], st) * scale).astype(BF16)


def _fourier_ctx(z_ft, cs, mc, s_len, cl):
    blk = s_len // cl
    return pl.pallas_call(
        functools.partial(_ft_ctx_kernel, scale=(cl * HEAD_DIM) ** -0.5),
        grid=(1,),
        in_specs=[pl.BlockSpec((cl, FT_WIDTH), lambda i: (blk, 0)),
                  pl.BlockSpec((HEAD_DIM, 2 * HEAD_DIM), lambda i: (0, 0)),
                  pl.BlockSpec((cl, 2 * cl), lambda i: (0, 0))],
        out_specs=pl.BlockSpec((cl, FT_WIDTH), lambda i: (0, 0)),
        out_shape=jax.ShapeDtypeStruct((cl, FT_WIDTH), BF16),
        compiler_params=_cparams(("arbitrary",)),
        name="fourier_ctx",
    )(z_ft, cs, mc)


def _out_kernel(*refs, with_ctx, n_lat_tiles):
    if with_ctx:
        (ona_l, ona_c, osg, oft_l, oft_c, x_ref, w_ref, g_ref, lng, lnb, sh_ref, sc_ref,
         wr_ref, br_ref, x1_ref, h2_ref, lg_ref, r_scr, hs_scr) = refs
        is_ctx = pl.program_id(0) >= n_lat_tiles
        ona = jnp.where(is_ctx, ona_c[...], ona_l[...])
        oft = jnp.where(is_ctx, oft_c[...], oft_l[...])
    else:
        (ona_l, osg, oft_l, x_ref, w_ref, g_ref, lng, lnb, sh_ref, sc_ref,
         wr_ref, br_ref, x1_ref, h2_ref, lg_ref, r_scr, hs_scr) = refs
        ona = ona_l[...]
        oft = oft_l[...]
    sg = osg[...]
    nw = 512
    for n in range(D_MODEL // nw):
        cols = slice(n * nw, (n + 1) * nw)
        y = (_dot(ona, w_ref[0:NA_WIDTH, cols])
             + _dot(sg, w_ref[NA_WIDTH:NA_WIDTH + SG_WIDTH, cols])
             + _dot(oft, w_ref[NA_WIDTH + SG_WIDTH:, cols]))
        r_scr[:, cols] = ALPHA * x_ref[:, cols] + g_ref[0][:, cols] * y

    def norm_rows(ci, carry):
        rows = pl.ds(pl.multiple_of(ci * NORM_ROWS, NORM_ROWS), NORM_ROWS)
        x1 = _ln_rows(r_scr[rows, :]) * lng[...] + lnb[...]
        x1_ref[rows, :] = x1
        h2 = _ln_rows(x1) * (1.0 + sc_ref[0]) + sh_ref[0]
        hi = h2.astype(BF16)
        h2_ref[rows, :] = _pack_halves(hi.astype(F32))
        hs_scr[0, rows, :] = hi
        hs_scr[1, rows, :] = (h2 - hi.astype(F32)).astype(BF16)
        return carry

    lax.fori_loop(0, r_scr.shape[0] // NORM_ROWS, norm_rows, 0, unroll=True)
    lg_ref[...] = (_dot(hs_scr[0], wr_ref[0]) + _dot(hs_scr[1], wr_ref[0])
                   + _dot(hs_scr[0], wr_ref[1]) + br_ref[...])


def _out_proj(ona_l, ona_c, osg, oft_l, oft_c, xall, w_out_bf, gate, lng, lnb, shift, scale,
              wr_split, br_pad, layer, n_rows, n_lat_tiles, with_ctx):
    tm = ROW_TILE
    row = lambda i: (i, 0)
    lat = lambda i: (jnp.minimum(i, n_lat_tiles - 1), 0)
    typ = lambda i: (jnp.where(i >= n_lat_tiles, 1, 0), 0, 0)
    const = lambda i: (0, 0)
    modspec = pl.BlockSpec((1, 1, D_MODEL), typ)
    vec = pl.BlockSpec((1, D_MODEL), const)
    specs = [pl.BlockSpec((tm, NA_WIDTH), lat)]
    args = [ona_l]
    if with_ctx:
        specs.append(pl.BlockSpec((tm, NA_WIDTH), const))
        args.append(ona_c)
    specs.append(pl.BlockSpec((tm, SG_WIDTH), row))
    args.append(osg)
    specs.append(pl.BlockSpec((tm, FT_WIDTH), lat))
    args.append(oft_l)
    if with_ctx:
        specs.append(pl.BlockSpec((tm, FT_WIDTH), const))
        args.append(oft_c)
    specs += [pl.BlockSpec((tm, D_MODEL), row),
              pl.BlockSpec((None, D_MODEL, D_MODEL), lambda i: (layer, 0, 0),
                           pipeline_mode=pl.Buffered(1)),
              modspec, vec, vec, modspec, modspec,
              pl.BlockSpec((None, 2, D_MODEL, LANES), lambda i: (layer, 0, 0, 0)),
              pl.BlockSpec((1, LANES), const)]
    args += [xall, w_out_bf, gate, lng, lnb, shift, scale, wr_split, br_pad]
    return pl.pallas_call(
        functools.partial(_out_kernel, with_ctx=with_ctx, n_lat_tiles=n_lat_tiles),
        grid=(n_rows // tm,),
        in_specs=specs,
        out_specs=[pl.BlockSpec((tm, D_MODEL), row),
                   pl.BlockSpec((tm, D_MODEL // 2), row),
                   pl.BlockSpec((tm, LANES), row)],
        out_shape=[jax.ShapeDtypeStruct((n_rows, D_MODEL), F32),
                   jax.ShapeDtypeStruct((n_rows, D_MODEL // 2), jnp.uint32),
                   jax.ShapeDtypeStruct((n_rows, LANES), F32)],
        scratch_shapes=[pltpu.VMEM((tm, D_MODEL), F32),
                        pltpu.VMEM((2, tm, D_MODEL), BF16)],
        compiler_params=_cparams(("arbitrary",)),
        name="out_proj",
    )(*args)


def _route_kernel(lg_ref, tri_ref, eid_ref, rank_ref, gate_ref, cnt_ref, carry):
    @pl.when(pl.program_id(0) == 0)
    def _():
        carry[...] = jnp.zeros_like(carry)

    work = lg_ref[...]
    lane = lax.broadcasted_iota(jnp.int32, work.shape, 1)
    lane_f = lane.astype(F32)
    vals, ids, hots = [], [], []
    for _ in range(TOP_K):
        m = jnp.max(work, axis=-1, keepdims=True)
        idx = jnp.min(jnp.where(work == m, lane_f, float(LANES)), axis=-1, keepdims=True)
        hot = lane_f == idx
        vals.append(m)
        ids.append(idx)
        hots.append(hot)
        work = jnp.where(hot, -jnp.inf, work)
    exps = [jnp.exp(v - vals[0]) for v in vals]
    den = exps[0] + exps[1] + exps[2] + exps[3]
    multi = jnp.zeros(work.shape, F32)
    for hot in hots:
        multi = multi + hot.astype(F32)
    pref = _dot(tri_ref[...], multi.astype(BF16)) + carry[...]
    eid = jnp.zeros(work.shape, F32)
    rank = jnp.zeros(work.shape, F32)
    gate = jnp.zeros(work.shape, F32)
    for k in range(TOP_K):
        rk = jnp.sum(jnp.where(hots[k], pref, 0.0), axis=-1, keepdims=True)
        sel = lane == k
        eid = jnp.where(sel, ids[k], eid)
        rank = jnp.where(sel, rk, rank)
        gate = jnp.where(sel, exps[k] / den, gate)
    eid_ref[...] = eid.astype(jnp.int32)
    rank_ref[...] = rank.astype(jnp.int32)
    gate_ref[...] = gate
    carry[...] = carry[...] + jnp.sum(multi, axis=0, keepdims=True)
    cnt_ref[...] = jnp.broadcast_to(carry[...], cnt_ref.shape).astype(jnp.int32)


def _route(logits, tri):
    t = logits.shape[0]
    tm = ROW_TILE
    row = lambda i: (i, 0)
    return pl.pallas_call(
        _route_kernel,
        grid=(t // tm,),
        in_specs=[pl.BlockSpec((tm, LANES), row),
                  pl.BlockSpec((tm, tm), lambda i: (0, 0))],
        out_specs=[pl.BlockSpec((tm, LANES), row)] * 3
        + [pl.BlockSpec((8, LANES), lambda i: (0, 0))],
        out_shape=[jax.ShapeDtypeStruct((t, LANES), jnp.int32)] * 2
        + [jax.ShapeDtypeStruct((t, LANES), F32),
           jax.ShapeDtypeStruct((8, LANES), jnp.int32)],
        scratch_shapes=[pltpu.VMEM((1, LANES), F32)],
        compiler_params=_cparams(("arbitrary",)),
        name="route",
    )(logits, tri)


def _dispatch_kernel(pstart_ref, pcount_ref, h_ref, slot_ref, xb_ref, sem, pad_sem):
    tm = h_ref.shape[0]
    first = pl.program_id(0) == 0

    def row_copy(src_row, dst_row, s):
        return pltpu.make_async_copy(h_ref.at[pl.ds(src_row, 1)], xb_ref.at[pl.ds(dst_row, 1)], s)

    def tail_copy(b):
        return pltpu.make_async_copy(h_ref, xb_ref.at[pl.ds(b * MOE_BLOCK, MOE_BLOCK)], pad_sem)

    n_blocks = xb_ref.shape[0] // MOE_BLOCK
    n_used = pcount_ref[N_EXPERTS + 1]

    @pl.when(first)
    def _():
        def per_expert(e, carry):
            def per_row(j, c2):
                row_copy(0, pstart_ref[e] + j, pad_sem).start()
                return c2
            return lax.fori_loop(0, pcount_ref[e], per_row, carry)
        lax.fori_loop(0, N_EXPERTS, per_expert, 0)

        def per_tail(b, carry):
            tail_copy(b).start()
            return carry
        lax.fori_loop(n_used, n_blocks, per_tail, 0)

    def per_token(g, carry):
        for u in range(2):
            t = g * 2 + u
            for k in range(TOP_K):
                row_copy(t, slot_ref[t * TOP_K + k], sem).start(priority=k % 2)
        return carry

    lax.fori_loop(0, tm // 2, per_token, 0)
    for _ in range(TOP_K):
        pltpu.make_async_copy(h_ref, xb_ref.at[pl.ds(0, tm)], sem).wait()

    @pl.when(first)
    def _():
        def wait_row(j, carry):
            row_copy(0, 0, pad_sem).wait()
            return carry
        lax.fori_loop(0, pcount_ref[N_EXPERTS], wait_row, 0)

        def wait_tail(b, carry):
            tail_copy(b).wait()
            return carry
        lax.fori_loop(n_used, n_blocks, wait_tail, 0)


def _dispatch(pstart, pcount, h2, slot_flat, n_slots):
    t = h2.shape[0]
    tm = ROW_TILE
    gs = pltpu.PrefetchScalarGridSpec(
        num_scalar_prefetch=2,
        grid=(t // tm,),
        in_specs=[pl.BlockSpec((tm, h2.shape[1]), lambda i, a, b: (i, 0)),
                  pl.BlockSpec((tm * TOP_K,), lambda i, a, b: (i,), memory_space=pltpu.SMEM)],
        out_specs=pl.BlockSpec(memory_space=pl.ANY),
        scratch_shapes=[pltpu.SemaphoreType.DMA(()), pltpu.SemaphoreType.DMA(())],
    )
    return pl.pallas_call(
        _dispatch_kernel,
        grid_spec=gs,
        out_shape=jax.ShapeDtypeStruct((n_slots, h2.shape[1]), h2.dtype),
        compiler_params=_cparams(("arbitrary",), row_dma=True),
        name="dispatch",
    )(pstart, pcount, h2, slot_flat)


def _expert_kernel(bexp_ref, nused_ref, group_ref, next_ref, xb_ref, wgu_hbm, bgu_ref, wd_hbm,
                   bd_ref, perm_ref, y_ref, wgu_f, wd_f, wgu_s, wd_s, wsem, *, layer):
    b = pl.program_id(0)
    active = b < nused_ref[0]
    blk = xb_ref.shape[0]
    expert = bexp_ref[b]
    fresh = jnp.logical_or(b == 0, expert != bexp_ref[jnp.maximum(b - 1, 0)])
    half = group_ref[b] % 2

    def weight_copies(e, buf):
        return (pltpu.make_async_copy(wgu_hbm.at[layer, e], wgu_f.at[buf], wsem.at[buf, 0]),
                pltpu.make_async_copy(wd_hbm.at[layer, e], wd_f.at[buf], wsem.at[buf, 1]))

    @pl.when(jnp.logical_not(active))
    def _():
        y_ref[...] = jnp.zeros_like(y_ref)

    @pl.when(jnp.logical_and(active, b == 0))
    def _():
        for cp in weight_copies(expert, 0):
            cp.start()

    @pl.when(jnp.logical_and(active, fresh))
    def _():
        for cp in weight_copies(expert, half):
            cp.wait()
        wgu_s[...] = wgu_f[half].astype(BF16)
        wd_s[...] = _dot(perm_ref[...], wd_f[half].astype(BF16)).astype(BF16)

        @pl.when(next_ref[b] >= 0)
        def _():
            for cp in weight_copies(next_ref[b], 1 - half):
                cp.start()

    @pl.when(active)
    def _():
        x_first, x_second = _unpack_halves(xb_ref[...])
        half_k = D_MODEL // 2
        gu = (_dot(x_first.astype(BF16), wgu_s[0:half_k, :])
              + _dot(x_second.astype(BF16), wgu_s[half_k:, :]) + bgu_ref[...])
        lane = lax.broadcasted_iota(jnp.int32, (blk, LANES), 1)
        even = (lane % 2) == 0
        prods = []
        for c in range(2 * D_FF // LANES):
            guc = gu[:, c * LANES:(c + 1) * LANES]
            glu = jnp.minimum(guc, SWIGLU_LIMIT)
            lin = jnp.clip(guc, -SWIGLU_LIMIT, SWIGLU_LIMIT) + 1.0
            prods.append(glu * jax.nn.sigmoid(SWIGLU_ALPHA * glu)
                         * pltpu.roll(lin, LANES - 1, axis=1))
        merged = [jnp.where(even, prods[2 * m], pltpu.roll(prods[2 * m + 1], 1, axis=1))
                  for m in range(D_FF // LANES)]
        act = jnp.concatenate(merged, axis=1).astype(BF16)
        y = _dot(act, wd_s[...]) + bd_ref[...]
        y_ref[...] = _pack_halves(y.astype(BF16).astype(F32))


def _experts(bexp, nused, group, next_expert, xb, w_gate_up, b_gate_up, w_down, b_down, layer,
             n_blocks):
    blk = MOE_BLOCK

    def bmap(b, be, nu, gr, nx):
        return (jnp.minimum(b, nu[0] - 1), 0)

    def emap(b, be, nu, gr, nx):
        return (layer, be[jnp.minimum(b, nu[0] - 1)], 0, 0)

    gs = pltpu.PrefetchScalarGridSpec(
        num_scalar_prefetch=4,
        grid=(n_blocks,),
        in_specs=[pl.BlockSpec((blk, D_MODEL // 2), bmap),
                  pl.BlockSpec(memory_space=pl.ANY),
                  pl.BlockSpec((None, None, 1, 2 * D_FF), emap),
                  pl.BlockSpec(memory_space=pl.ANY),
                  pl.BlockSpec((None, None, 1, D_MODEL), emap),
                  pl.BlockSpec((D_FF, D_FF), lambda b, be, nu, gr, nx: (0, 0))],
        out_specs=pl.BlockSpec((blk, D_MODEL // 2), lambda b, be, nu, gr, nx: (b, 0)),
        scratch_shapes=[pltpu.VMEM((2, D_MODEL, 2 * D_FF), F32),
                        pltpu.VMEM((2, D_FF, D_MODEL), F32),
                        pltpu.VMEM((D_MODEL, 2 * D_FF), BF16),
                        pltpu.VMEM((D_FF, D_MODEL), BF16),
                        pltpu.SemaphoreType.DMA((2, 2))],
    )
    lane = np.arange(D_FF) % LANES
    unit = (np.arange(D_FF) // LANES) * LANES + lane // 2 + (lane % 2) * (LANES // 2)
    perm = np.zeros((D_FF, D_FF), np.float32)
    perm[np.arange(D_FF), unit] = 1.0
    return pl.pallas_call(
        functools.partial(_expert_kernel, layer=layer),
        grid_spec=gs,
        out_shape=jax.ShapeDtypeStruct(xb.shape, jnp.uint32),
        compiler_params=_cparams(("arbitrary",)),
        name="experts",
    )(bexp, nused, group, next_expert, xb, w_gate_up, b_gate_up[:, :, None, :],
      w_down, b_down[:, :, None, :], jnp.asarray(perm, BF16))


def _combine_kernel(x1_ref, slot_ref, slot_next_ref, gate_ref, yb_ref, g_ref, lng, lnb,
                    o_ref, gbuf, sems):
    tm = x1_ref.shape[0]
    i = pl.program_id(0)
    cur = i % 2

    def row_copy(src_row, buf, k, t):
        return pltpu.make_async_copy(yb_ref.at[pl.ds(src_row, 1)],
                                     gbuf.at[buf, k, pl.ds(t, 1)], sems.at[buf])

    def gather(slots, buf):
        def per_token(g, carry):
            for u in range(2):
                t = g * 2 + u
                for k in range(TOP_K):
                    row_copy(slots[t * TOP_K + k], buf, k, t).start(priority=k % 2)
            return carry
        lax.fori_loop(0, tm // 2, per_token, 0)

    @pl.when(i == 0)
    def _():
        gather(slot_ref, 0)

    @pl.when(i + 1 < pl.num_programs(0))
    def _():
        gather(slot_next_ref, 1 - cur)

    for k in range(TOP_K):
        pltpu.make_async_copy(yb_ref.at[pl.ds(0, tm)], gbuf.at[cur, k], sems.at[cur]).wait()

    def norm_rows(ci, carry):
        rows = pl.ds(pl.multiple_of(ci * NORM_ROWS, NORM_ROWS), NORM_ROWS)
        gate = gate_ref[rows, :]
        f_first = f_second = None
        for k in range(TOP_K):
            y_first, y_second = _unpack_halves(gbuf[cur, k, rows, :])
            gk = gate[:, k:k + 1]
            f_first = gk * y_first if k == 0 else f_first + gk * y_first
            f_second = gk * y_second if k == 0 else f_second + gk * y_second
        f = jnp.concatenate([f_first, f_second], axis=1)
        r = ALPHA * x1_ref[rows, :] + g_ref[0] * f
        o_ref[rows, :] = _ln_rows(r) * lng[...] + lnb[...]
        return carry

    lax.fori_loop(0, tm // NORM_ROWS, norm_rows, 0, unroll=True)


def _combine(x1, slot_flat, gates, yb, gate_mod, lng, lnb, n_lat_tiles):
    t = x1.shape[0]
    tm = ROW_TILE
    n_tiles = t // tm
    return pl.pallas_call(
        _combine_kernel,
        grid=(n_tiles,),
        in_specs=[pl.BlockSpec((tm, D_MODEL), lambda i: (i, 0)),
                  pl.BlockSpec((tm * TOP_K,), lambda i: (i,), memory_space=pltpu.SMEM),
                  pl.BlockSpec((tm * TOP_K,), lambda i: (jnp.minimum(i + 1, n_tiles - 1),),
                               memory_space=pltpu.SMEM),
                  pl.BlockSpec((tm, LANES), lambda i: (i, 0)),
                  pl.BlockSpec(memory_space=pl.ANY),
                  pl.BlockSpec((1, 1, D_MODEL),
                               lambda i: (jnp.where(i >= n_lat_tiles, 1, 0), 0, 0)),
                  pl.BlockSpec((1, D_MODEL), lambda i: (0, 0)),
                  pl.BlockSpec((1, D_MODEL), lambda i: (0, 0))],
        out_specs=pl.BlockSpec((tm, D_MODEL), lambda i: (i, 0)),
        out_shape=jax.ShapeDtypeStruct((t, D_MODEL), F32),
        scratch_shapes=[pltpu.VMEM((2, TOP_K, tm, yb.shape[1]), yb.dtype),
                        pltpu.SemaphoreType.DMA((2,))],
        compiler_params=_cparams(("arbitrary",), row_dma=True),
        name="combine",
    )(x1, slot_flat, slot_flat, gates, yb, gate_mod, lng, lnb)


def _rope_tables(s_len, cl):
    t = jnp.arange(s_len, dtype=jnp.int32)
    quarter = HEAD_DIM // 4
    inv = ROPE_BASE ** (-jnp.arange(quarter, dtype=F32) / quarter)
    ang_r = (t // GRID_W).astype(F32)[:, None] * inv
    ang_c = (t % GRID_W).astype(F32)[:, None] * inv
    cr, sr, cc, sc = jnp.cos(ang_r), jnp.sin(ang_r), jnp.cos(ang_c), jnp.sin(ang_c)
    cos_t = jnp.concatenate([cr, cr, cc, cc], axis=1)
    sin_t = jnp.concatenate([-sr, sr, -sc, sc], axis=1)
    cos_t = jnp.concatenate([cos_t, jnp.ones((cl, HEAD_DIM), F32)], axis=0)
    sin_t = jnp.concatenate([sin_t, jnp.zeros((cl, HEAD_DIM), F32)], axis=0)
    return cos_t, sin_t


def _block_plan(counts, eid, rank, n_blocks):
    blocks_per = (counts + MOE_BLOCK - 1) // MOE_BLOCK
    block_end = jnp.cumsum(blocks_per)
    base = (block_end - blocks_per) * MOE_BLOCK
    blocks = jnp.arange(n_blocks)
    bexp = jnp.minimum(jnp.sum(block_end[None, :] <= blocks[:, None], axis=1),
                       N_EXPERTS - 1).astype(jnp.int32)
    nused = block_end[-1:].astype(jnp.int32)
    experts = jnp.arange(N_EXPERTS, dtype=jnp.int32)
    first = jnp.concatenate([jnp.ones((1,), bool), bexp[1:] != bexp[:-1]])
    group = (jnp.cumsum(first) - 1).astype(jnp.int32)
    group_end = jnp.sum(jnp.where(bexp[:, None] == experts, block_end, 0), axis=-1)
    follower = jnp.sum(jnp.where(blocks[None, :] == group_end[:, None], bexp[None, :], 0), axis=-1)
    next_expert = jnp.where(group_end < nused[0], follower, -1).astype(jnp.int32)
    slot = rank + jnp.sum(jnp.where(eid[:, :, None] == experts, base, 0), axis=-1)
    pad = blocks_per * MOE_BLOCK - counts
    pcount = jnp.concatenate([pad, jnp.sum(pad)[None], nused]).astype(jnp.int32)
    pstart = (base + counts).astype(jnp.int32)
    return (bexp, nused, group, next_expert, slot.reshape(-1).astype(jnp.int32), pstart,
            pcount)


def kernel(x, c, ctx, c_ctx, w_ada, b_ada, w_in, w_out, sg_w, sg_b, na_rpb, ln1_g, ln1_b,
           ln2_g, ln2_b, w_router, b_router, w_gate_up, b_gate_up, w_down, b_down):
    bsz, s_len, dm = x.shape
    cl = ctx.shape[1]
    depth = w_ada.shape[0]
    assert bsz == 1 and dm == D_MODEL and cl == ROW_TILE and MOE_BLOCK == ROW_TILE
    assert s_len % (GRID_W * KEY_ROWS) == 0 and s_len % ROW_TILE == 0
    t_all = s_len + cl
    n_lat_tiles = s_len // ROW_TILE
    rows = s_len // GRID_W

    cond = jnp.zeros((8, dm), F32).at[0].set(c[0]).at[1].set(c_ctx)
    mod = _ada_mod(cond, w_ada, b_ada)[:, :2].reshape(depth, 2, 6, 1, dm)
    cos_t, sin_t = _rope_tables(s_len, cl)
    pb, vrow = _na_bias_tables(na_rpb, rows)
    cs, m1, tw, mc = _fourier_consts(s_len, cl)
    tri = jnp.asarray(np.tril(np.ones((ROW_TILE, ROW_TILE), np.float32), -1), BF16)
    wr_pad = jnp.pad(w_router, ((0, 0), (0, 0), (0, LANES - N_EXPERTS)))
    wr_hi = wr_pad.astype(BF16)
    wr_split = jnp.stack([wr_hi, (wr_pad - wr_hi.astype(F32)).astype(BF16)], axis=1)
    br_pad = jnp.pad(b_router, ((0, 0), (0, LANES - N_EXPERTS)), constant_values=NEG_INF)
    w_in_bf = w_in.astype(BF16)
    w_out_bf = w_out.astype(BF16)
    sgw_bf = sg_w.astype(BF16)

    xall = jnp.concatenate([x[0], ctx[0]], axis=0)
    for l in range(depth):
        last = l == depth - 1
        m = lambda j: mod[l, :, j]
        qp, qr, kr, v, z_sg, z_ft = _proj(xall, m(0), m(1), w_in_bf, cos_t, sin_t, l,
                                          n_lat_tiles)
        ona_l = _na_attention(qr, qp, kr, v, pb, vrow, l, s_len)
        osg = _spatial_gating(z_sg, sgw_bf[l], sg_b[l][:, :, None])
        oft_l = _fourier_latent(z_ft, cs, m1, tw, s_len)
        if last:
            ona_c = oft_c = None
            n_rows = s_len
        else:
            ona_c = _ctx_attention(qp, kr, v, s_len, cl)
            oft_c = _fourier_ctx(z_ft, cs, mc, s_len, cl)
            n_rows = t_all
        x1, h2, logits = _out_proj(
            ona_l, ona_c, osg, oft_l, oft_c, xall, w_out_bf, m(2),
            ln1_g[l][None], ln1_b[l][None], m(3), m(4), wr_split, br_pad[l][None],
            l, n_rows, n_lat_tiles, not last)
        eid, rank, gates, counts = _route(logits, tri)
        n_blocks = -(-n_rows * TOP_K // MOE_BLOCK) + N_EXPERTS
        bexp, nused, group, next_expert, slot_flat, pstart, pcount = _block_plan(
            counts[0, :N_EXPERTS], eid[:, :TOP_K], rank[:, :TOP_K], n_blocks)
        xb = _dispatch(pstart, pcount, h2, slot_flat, n_blocks * MOE_BLOCK)
        yb = _experts(bexp, nused, group, next_expert, xb, w_gate_up, b_gate_up, w_down, b_down,
                      l, n_blocks)
        xall = _combine(x1, slot_flat, gates, yb, m(5), ln2_g[l][None], ln2_b[l][None],
                        n_lat_tiles)
    return xall[None]
```

```python
import functools

import numpy as np
import jax
import jax.numpy as jnp
from jax import lax
from jax.experimental import pallas as pl
from jax.experimental.pallas import tpu as pltpu

D_MODEL = 2048
DEPTH_NORM = 4
GRID_W = 64
HEAD_DIM = 128
NA_HEADS = 8
NA_WIDTH = NA_HEADS * HEAD_DIM
NA_KH = 8
NA_KW = 16
SG_GROUPS = 4
SG_WIDTH = 512
SG_CHUNK = 128
FT_GROUPS = 4
FT_WIDTH = 512
SG_OFF = 3 * NA_WIDTH
FT_OFF = SG_OFF + 2 * SG_WIDTH
IN_WIDTH = FT_OFF + FT_WIDTH
ROPE_BASE = 10000.0
N_EXPERTS = 32
TOP_K = 4
D_FF = D_MODEL // 4
SWIGLU_ALPHA = 1.702
SWIGLU_LIMIT = 7.0
LN_EPS = 1e-5
NEG_INF = -1e30
ALPHA = (2 * DEPTH_NORM) ** 0.25

LANES = 128
ROW_TILE = 256
MOE_BLOCK = 256
NORM_ROWS = 32
KEY_ROWS = 16
Q_ROWS = 8
NA_PAIRS = 5
NA_HEADS_PER_STEP = 4
VMEM_LIMIT = 56 * 1024 * 1024

F32 = jnp.float32
BF16 = jnp.bfloat16


def _cparams(sem, row_dma=False):
    return pltpu.CompilerParams(dimension_semantics=sem, vmem_limit_bytes=VMEM_LIMIT,
                                disable_bounds_checks=row_dma)


def _dot(a, b):
    return jnp.dot(a, b, preferred_element_type=F32)


def _dot_nt(a, b):
    return lax.dot_general(a, b, (((1,), (1,)), ((), ())), preferred_element_type=F32)


def _pack_halves(x):
    n = x.shape[1] // 2
    hi = lax.bitcast_convert_type(x[:, :n], jnp.uint32)
    lo = lax.bitcast_convert_type(x[:, n:], jnp.uint32)
    return hi | (lo >> 16)


def _unpack_halves(w):
    first = lax.bitcast_convert_type(w & jnp.uint32(0xFFFF0000), F32)
    second = lax.bitcast_convert_type(w << 16, F32)
    return first, second


def _ln_rows(x):
    mu = jnp.mean(x, axis=-1, keepdims=True)
    xc = x - mu
    var = jnp.mean(xc * xc, axis=-1, keepdims=True)
    return xc * lax.rsqrt(var + LN_EPS)


def _ada_kernel(c_ref, w_ref, b_ref, o_ref):
    c = c_ref[...]
    s = c * jax.nn.sigmoid(c)
    o_ref[0] = jnp.dot(s, w_ref[0], preferred_element_type=F32,
                       precision=lax.Precision.HIGHEST) + b_ref[0]


def _ada_mod(cond, w_ada, b_ada):
    depth, d, n = w_ada.shape
    tn = 1536
    return pl.pallas_call(
        _ada_kernel,
        grid=(depth, n // tn),
        in_specs=[pl.BlockSpec((8, d), lambda l, j: (0, 0)),
                  pl.BlockSpec((1, d, tn), lambda l, j: (l, 0, j)),
                  pl.BlockSpec((1, 1, tn), lambda l, j: (l, 0, j))],
        out_specs=pl.BlockSpec((1, 8, tn), lambda l, j: (l, 0, j)),
        out_shape=jax.ShapeDtypeStruct((depth, 8, n), F32),
        compiler_params=_cparams(("arbitrary", "arbitrary")),
        name="ada_mod",
    )(cond, w_ada, b_ada.reshape(depth, 1, n))


def _proj_kernel(x_ref, sh_ref, sc_ref, w_ref, cos_ref, sin_ref,
                 qp_ref, qr_ref, kr_ref, v_ref, sg_ref, ft_ref):
    y = _ln_rows(x_ref[...])
    h = (y * (1.0 + sc_ref[0]) + sh_ref[0]).astype(BF16)
    cos = cos_ref[...]
    sin = sin_ref[...]
    lane = lax.broadcasted_iota(jnp.int32, cos.shape, 1)
    first = (lane % 64) < 32

    def rope(z):
        swapped = jnp.where(first, pltpu.roll(z, 96, axis=1), pltpu.roll(z, 32, axis=1))
        return z * cos + swapped * sin

    nw = 512
    for j in range(IN_WIDTH // nw):
        z = _dot(h, w_ref[:, j * nw:(j + 1) * nw])
        for p in range(nw // LANES):
            col = j * nw + p * LANES
            zp = z[:, p * LANES:(p + 1) * LANES]
            if col < NA_WIDTH:
                zp = zp * (HEAD_DIM ** -0.5)
                qp_ref[:, col:col + LANES] = zp.astype(BF16)
                qr_ref[:, col:col + LANES] = rope(zp).astype(BF16)
            elif col < 2 * NA_WIDTH:
                c0 = col - NA_WIDTH
                kr_ref[:, c0:c0 + LANES] = rope(zp).astype(BF16)
            elif col < SG_OFF:
                c0 = col - 2 * NA_WIDTH
                v_ref[:, c0:c0 + LANES] = zp.astype(BF16)
            elif col < FT_OFF:
                c0 = col - SG_OFF
                sg_ref[:, c0:c0 + LANES] = zp
            else:
                c0 = col - FT_OFF
                ft_ref[:, c0:c0 + LANES] = zp.astype(BF16)


def _proj(xall, shift, scale, w_in_bf, cos_t, sin_t, layer, n_lat_tiles):
    t = xall.shape[0]
    tm = ROW_TILE
    typ = lambda i: (jnp.where(i >= n_lat_tiles, 1, 0), 0, 0)
    row = lambda i: (i, 0)
    return pl.pallas_call(
        _proj_kernel,
        grid=(t // tm,),
        in_specs=[pl.BlockSpec((tm, D_MODEL), row),
                  pl.BlockSpec((1, 1, D_MODEL), typ),
                  pl.BlockSpec((1, 1, D_MODEL), typ),
                  pl.BlockSpec((None, D_MODEL, IN_WIDTH), lambda i: (layer, 0, 0),
                               pipeline_mode=pl.Buffered(1)),
                  pl.BlockSpec((tm, LANES), row),
                  pl.BlockSpec((tm, LANES), row)],
        out_specs=[pl.BlockSpec((tm, NA_WIDTH), row)] * 4
        + [pl.BlockSpec((tm, 2 * SG_WIDTH), row), pl.BlockSpec((tm, FT_WIDTH), row)],
        out_shape=[jax.ShapeDtypeStruct((t, NA_WIDTH), BF16)] * 4
        + [jax.ShapeDtypeStruct((t, 2 * SG_WIDTH), F32),
           jax.ShapeDtypeStruct((t, FT_WIDTH), BF16)],
        compiler_params=_cparams(("arbitrary",)),
        name="ln_proj",
    )(xall, shift, scale, w_in_bf, cos_t, sin_t)


def _na_kernel(qr_ref, qp_ref, k0, k1, k2, k3, v0, v1, v2, v3, kc_ref, vc_ref, pb_ref, vrow_ref,
               o_ref, k_scr, v_scr, s_scr, p_scr, den_scr, *, n_rb):
    i = pl.program_id(1)
    is_first = i == 0
    is_last = i == n_rb - 1
    off = jnp.where(is_first, 0, jnp.where(is_last, -(KEY_ROWS - Q_ROWS), -(NA_KH // 2)))
    kb = k0.shape[0]
    for j, (kj, vj) in enumerate(((k0, v0), (k1, v1), (k2, v2), (k3, v3))):
        k_scr[j * kb:(j + 1) * kb, :] = kj[...]
        v_scr[j * kb:(j + 1) * kb, :] = vj[...]
    qsub = 128
    pair = 2 * GRID_W
    starts = ((0, 0, 0, 1), (0, 1, 2, 3), (2, 3, 3, 3))
    n_sub = Q_ROWS * GRID_W // qsub
    n_lat = NA_PAIRS * pair
    units = [(hh, s) for hh in range(NA_HEADS_PER_STEP) for s in range(n_sub)]
    p0s, key0 = [], []
    for s in range(n_sub):
        p0 = jnp.where(is_first, starts[0][s], jnp.where(is_last, starts[2][s], starts[1][s]))
        p0s.append(p0)
        key0.append(pl.multiple_of(p0 * pair, pair))
    for u, (hh, s) in enumerate(units):
        rows = slice(s * qsub, (s + 1) * qsub)
        cols = slice(hh * HEAD_DIM, (hh + 1) * HEAD_DIM)
        bias_rows = []
        for qr in range(s * qsub // GRID_W, (s + 1) * qsub // GRID_W):
            tiles = [pb_ref[0, hh, jnp.clip(2 * (p0s[s] + j) - qr + off + NA_KH, 0, 2 * NA_KH - 1)]
                     + vrow_ref[0, qr, p0s[s] + j]
                     for j in range(NA_PAIRS)]
            bias_rows.append(jnp.concatenate(tiles, axis=1))
        bias = jnp.concatenate(bias_rows, axis=0)
        s_scr[u, :, 0:n_lat] = _dot_nt(qr_ref[rows, cols],
                                       k_scr[pl.ds(key0[s], n_lat), cols]) + bias
        s_scr[u, :, n_lat:] = _dot_nt(qp_ref[rows, cols], kc_ref[:, cols])
    for u in range(len(units)):
        sc = s_scr[u]
        p = jnp.exp(sc - jnp.max(sc, axis=-1, keepdims=True))
        p_scr[u] = p.astype(BF16)
        den_scr[u] = jnp.sum(p, axis=-1, keepdims=True)
    for u, (hh, s) in enumerate(units):
        rows = slice(s * qsub, (s + 1) * qsub)
        cols = slice(hh * HEAD_DIM, (hh + 1) * HEAD_DIM)
        o = (_dot(p_scr[u, :, 0:n_lat], v_scr[pl.ds(key0[s], n_lat), cols])
             + _dot(p_scr[u, :, n_lat:], vc_ref[:, cols]))
        o_ref[rows, cols] = (o / den_scr[u]).astype(BF16)


def _na_attention(qr, qp, kr, v, pb, vrow, layer, s_len):
    rows = s_len // GRID_W
    n_rb = rows // Q_ROWS
    qb = Q_ROWS * GRID_W
    kb = 256
    last_kblock = (rows - KEY_ROWS) * GRID_W // kb
    ctx_block = s_len // kb

    def kmap(j):
        return lambda h, i: (jnp.clip(2 * i - 1, 0, last_kblock) + j, h)

    def btype(i):
        return jnp.where(i == 0, 0, jnp.where(i == n_rb - 1, 2, 1))

    hw = NA_HEADS_PER_STEP * HEAD_DIM
    n_units = NA_HEADS_PER_STEP * qb // 128
    n_keys = NA_PAIRS * 2 * GRID_W + kb
    qspec = pl.BlockSpec((qb, hw), lambda h, i: (i, h))
    kspecs = [pl.BlockSpec((kb, hw), kmap(j)) for j in range(4)]
    cspec = pl.BlockSpec((kb, hw), lambda h, i: (ctx_block, h))
    return pl.pallas_call(
        functools.partial(_na_kernel, n_rb=n_rb),
        grid=(NA_HEADS // NA_HEADS_PER_STEP, n_rb),
        in_specs=[qspec, qspec] + kspecs + kspecs + [cspec, cspec]
        + [pl.BlockSpec((1, NA_HEADS_PER_STEP, 2 * NA_KH, GRID_W, 2 * GRID_W),
                        lambda h, i: (layer, h, 0, 0, 0)),
           pl.BlockSpec((1, Q_ROWS, KEY_ROWS // 2, 1, 2 * GRID_W),
                        lambda h, i: (btype(i), 0, 0, 0, 0))],
        out_specs=pl.BlockSpec((qb, hw), lambda h, i: (i, h)),
        out_shape=jax.ShapeDtypeStruct((s_len, NA_WIDTH), BF16),
        scratch_shapes=[pltpu.VMEM((KEY_ROWS * GRID_W, hw), BF16),
                        pltpu.VMEM((KEY_ROWS * GRID_W, hw), BF16),
                        pltpu.VMEM((n_units, 128, n_keys), F32),
                        pltpu.VMEM((n_units, 128, n_keys), BF16),
                        pltpu.VMEM((n_units, 128, 1), F32)],
        compiler_params=_cparams(("arbitrary", "arbitrary")),
        name="na_attention",
    )(qr, qp, kr, kr, kr, kr, v, v, v, v, kr, v, pb, vrow)


def _ctx_attn_kernel(q_ref, k_ref, v_ref, o_ref):
    s = _dot_nt(q_ref[...], k_ref[...])
    m = jnp.max(s, axis=-1, keepdims=True)
    p = jnp.exp(s - m)
    den = jnp.sum(p, axis=-1, keepdims=True)
    o_ref[...] = (_dot(p.astype(BF16), v_ref[...]) / den).astype(BF16)


def _ctx_attention(qp, kr, v, s_len, cl):
    blk = s_len // cl
    spec = pl.BlockSpec((cl, HEAD_DIM), lambda h: (blk, h))
    return pl.pallas_call(
        _ctx_attn_kernel,
        grid=(NA_HEADS,),
        in_specs=[spec, spec, spec],
        out_specs=pl.BlockSpec((cl, HEAD_DIM), lambda h: (0, h)),
        out_shape=jax.ShapeDtypeStruct((cl, NA_WIDTH), BF16),
        compiler_params=_cparams(("arbitrary",)),
        name="ctx_attention",
    )(qp, kr, v)


def _na_bias_tables(na_rpb, rows):
    n_rb = rows // Q_ROWS
    rmask = np.zeros((3, Q_ROWS, KEY_ROWS), bool)
    for t, i in enumerate((0, 1, n_rb - 1)):
        ks = min(max(Q_ROWS * i - NA_KH // 2, 0), rows - KEY_ROWS)
        for qr in range(Q_ROWS):
            r = Q_ROWS * i + qr
            lo = min(max(r - NA_KH // 2, 0), rows - NA_KH)
            for kr in range(KEY_ROWS):
                rmask[t, qr, kr] = lo <= ks + kr < lo + NA_KH
    vrow = np.where(np.repeat(rmask, GRID_W, axis=2), 0.0, NEG_INF).astype(np.float32)
    csel = np.zeros((GRID_W, GRID_W, 2 * NA_KW - 1), np.float32)
    cmask = np.zeros((GRID_W, GRID_W), bool)
    for qc in range(GRID_W):
        lo = min(max(qc - NA_KW // 2, 0), GRID_W - NA_KW)
        for kc in range(GRID_W):
            cmask[qc, kc] = lo <= kc < lo + NA_KW
            csel[qc, kc, min(max(kc - qc + NA_KW - 1, 0), 2 * NA_KW - 2)] = 1.0
    b = jnp.einsum('lhab,qkb->lhaqk', na_rpb, jnp.asarray(csel), precision=lax.Precision.HIGHEST)
    b = jnp.where(jnp.asarray(cmask), b, NEG_INF)
    b = jnp.pad(b, ((0, 0), (0, 0), (1, 1), (0, 0), (0, 0)))
    pb = jnp.concatenate([b[:, :, :-1], b[:, :, 1:]], axis=-1)
    return pb, jnp.asarray(vrow).reshape(3, Q_ROWS, KEY_ROWS // 2, 1, 2 * GRID_W)


def _sg_kernel(z_ref, w_ref, b_ref, o_ref):
    z = z_ref[...]
    g = 0.5 * z * (1.0 + lax.erf(z * (2.0 ** -0.5)))
    n_chunks = z.shape[0] // SG_CHUNK
    for gi in range(SG_GROUPS):
        u = g[:, gi * LANES:(gi + 1) * LANES]
        vn = _ln_rows(g[:, SG_WIDTH + gi * LANES:SG_WIDTH + (gi + 1) * LANES]).astype(BF16)
        for n in range(n_chunks):
            rows = slice(n * SG_CHUNK, (n + 1) * SG_CHUNK)
            t = _dot(w_ref[gi], vn[rows, :]) + b_ref[gi]
            o_ref[rows, gi * LANES:(gi + 1) * LANES] = (u[rows, :] * t).astype(BF16)


def _spatial_gating(z_sg, sgw_bf, sgb):
    t = z_sg.shape[0]
    tm = ROW_TILE
    return pl.pallas_call(
        _sg_kernel,
        grid=(t // tm,),
        in_specs=[pl.BlockSpec((tm, 2 * SG_WIDTH), lambda i: (i, 0)),
                  pl.BlockSpec((SG_GROUPS, SG_CHUNK, SG_CHUNK), lambda i: (0, 0, 0)),
                  pl.BlockSpec((SG_GROUPS, SG_CHUNK, 1), lambda i: (0, 0, 0))],
        out_specs=pl.BlockSpec((tm, SG_WIDTH), lambda i: (i, 0)),
        out_shape=jax.ShapeDtypeStruct((t, SG_WIDTH), BF16),
        compiler_params=_cparams(("arbitrary",)),
        name="spatial_gating",
    )(z_sg, sgw_bf, sgb)


def _ft1_kernel(z_ref, cs_ref, m1_ref, y_ref):
    n1 = z_ref.shape[0]
    parts = []
    for j in range(z_ref.shape[1] // LANES):
        ab = _dot(z_ref[:, j * LANES:(j + 1) * LANES], cs_ref[...])
        parts.append(jnp.concatenate([ab[:, :LANES], ab[:, LANES:]], axis=0))
    st = jnp.concatenate(parts, axis=1).astype(BF16)
    y = _dot(m1_ref[...], st)
    y_ref[0] = y[:n1].astype(BF16)
    y_ref[1] = y[n1:].astype(BF16)


def _ft2_kernel(y_ref, tw_ref, o_ref, *, scale):
    for j in range(y_ref.shape[1]):
        yy = jnp.concatenate([y_ref[0, j], y_ref[1, j]], axis=0)
        o = _dot(tw_ref[j], yy) * scale
        o_ref[:, j * FT_WIDTH:(j + 1) * FT_WIDTH] = o.astype(BF16)


def _fourier_consts(s_len, cl):
    n2 = 128
    n1 = s_len // n2
    c = np.arange(HEAD_DIM)
    ang = 2 * np.pi * ((c[:, None] * c[None, :]) % HEAD_DIM) / HEAD_DIM
    cs = np.concatenate([np.cos(ang), np.sin(ang)], axis=1)
    a = np.arange(n1)
    ang1 = 2 * np.pi * ((a[:, None] * a[None, :]) % n1) / n1
    fc, fs = np.cos(ang1), np.sin(ang1)
    m1 = np.block([[fc, -fs], [-fs, -fc]])
    ka = jnp.arange(n1, dtype=jnp.int32)[:, None, None]
    kb = jnp.arange(n2, dtype=jnp.int32)[None, :, None]
    nn = jnp.arange(n2, dtype=jnp.int32)[None, None, :]
    ph = (nn * (ka + n1 * kb)) % s_len
    th = ph.astype(F32) * (2 * np.pi / s_len)
    tw = jnp.concatenate([jnp.cos(th), jnp.sin(th)], axis=2).astype(BF16)
    p = np.arange(cl)
    angc = 2 * np.pi * ((p[:, None] * p[None, :]) % cl) / cl
    mc = np.concatenate([np.cos(angc), -np.sin(angc)], axis=1)
    return (jnp.asarray(cs, BF16), jnp.asarray(m1, BF16), tw, jnp.asarray(mc, BF16))


def _fourier_latent(z_ft, cs, m1, tw, s_len):
    n2 = 128
    n1 = s_len // n2
    width = n2 * FT_WIDTH
    cw = 2048
    y = pl.pallas_call(
        _ft1_kernel,
        grid=(width // cw,),
        in_specs=[pl.BlockSpec((n1, cw), lambda j: (0, j)),
                  pl.BlockSpec((HEAD_DIM, 2 * HEAD_DIM), lambda j: (0, 0)),
                  pl.BlockSpec((2 * n1, 2 * n1), lambda j: (0, 0))],
        out_specs=pl.BlockSpec((2, n1, cw), lambda j: (0, 0, j)),
        out_shape=jax.ShapeDtypeStruct((2, n1, width), BF16),
        compiler_params=_cparams(("arbitrary",)),
        name="fourier_stage1",
    )(z_ft[:s_len].reshape(n1, width), cs, m1)
    kab = 4
    out = pl.pallas_call(
        functools.partial(_ft2_kernel, scale=(s_len * HEAD_DIM) ** -0.5),
        grid=(n1 // kab,),
        in_specs=[pl.BlockSpec((2, kab, n2, FT_WIDTH), lambda j: (0, j, 0, 0)),
                  pl.BlockSpec((kab, n2, 2 * n2), lambda j: (j, 0, 0))],
        out_specs=pl.BlockSpec((n2, kab * FT_WIDTH), lambda j: (0, j)),
        out_shape=jax.ShapeDtypeStruct((n2, n1 * FT_WIDTH), BF16),
        compiler_params=_cparams(("arbitrary",)),
        name="fourier_stage2",
    )(y.reshape(2, n1, n2, FT_WIDTH), tw)
    return out.reshape(s_len, FT_WIDTH)


def _ft_ctx_kernel(z_ref, cs_ref, mc_ref, o_ref, *, scale):
    for g in range(FT_GROUPS):
        ab = _dot(z_ref[:, g * LANES:(g + 1) * LANES], cs_ref[...])
        st = jnp.concatenate([ab[:, :LANES], ab[:, LANES:]], axis=0).astype(BF16)
        o_ref[:, g * LANES:(g + 1) * LANES] = (_dot(mc_ref[...], st) * scale).astype(BF16)


def _fourier_ctx(z_ft, cs, mc, s_len, cl):
    blk = s_len // cl
    return pl.pallas_call(
        functools.partial(_ft_ctx_kernel, scale=(cl * HEAD_DIM) ** -0.5),
        grid=(1,),
        in_specs=[pl.BlockSpec((cl, FT_WIDTH), lambda i: (blk, 0)),
                  pl.BlockSpec((HEAD_DIM, 2 * HEAD_DIM), lambda i: (0, 0)),
                  pl.BlockSpec((cl, 2 * cl), lambda i: (0, 0))],
        out_specs=pl.BlockSpec((cl, FT_WIDTH), lambda i: (0, 0)),
        out_shape=jax.ShapeDtypeStruct((cl, FT_WIDTH), BF16),
        compiler_params=_cparams(("arbitrary",)),
        name="fourier_ctx",
    )(z_ft, cs, mc)


def _out_kernel(*refs, with_ctx, n_lat_tiles):
    if with_ctx:
        (ona_l, ona_c, osg, oft_l, oft_c, x_ref, w_ref, g_ref, lng, lnb, sh_ref, sc_ref,
         wr_ref, br_ref, x1_ref, h2_ref, lg_ref, r_scr, hs_scr) = refs
        is_ctx = pl.program_id(0) >= n_lat_tiles
        ona = jnp.where(is_ctx, ona_c[...], ona_l[...])
        oft = jnp.where(is_ctx, oft_c[...], oft_l[...])
    else:
        (ona_l, osg, oft_l, x_ref, w_ref, g_ref, lng, lnb, sh_ref, sc_ref,
         wr_ref, br_ref, x1_ref, h2_ref, lg_ref, r_scr, hs_scr) = refs
        ona = ona_l[...]
        oft = oft_l[...]
    sg = osg[...]
    nw = 512
    for n in range(D_MODEL // nw):
        cols = slice(n * nw, (n + 1) * nw)
        y = (_dot(ona, w_ref[0:NA_WIDTH, cols])
             + _dot(sg, w_ref[NA_WIDTH:NA_WIDTH + SG_WIDTH, cols])
             + _dot(oft, w_ref[NA_WIDTH + SG_WIDTH:, cols]))
        r_scr[:, cols] = ALPHA * x_ref[:, cols] + g_ref[0][:, cols] * y

    def norm_rows(ci, carry):
        rows = pl.ds(pl.multiple_of(ci * NORM_ROWS, NORM_ROWS), NORM_ROWS)
        x1 = _ln_rows(r_scr[rows, :]) * lng[...] + lnb[...]
        x1_ref[rows, :] = x1
        h2 = _ln_rows(x1) * (1.0 + sc_ref[0]) + sh_ref[0]
        hi = h2.astype(BF16)
        h2_ref[rows, :] = _pack_halves(hi.astype(F32))
        hs_scr[0, rows, :] = hi
        hs_scr[1, rows, :] = (h2 - hi.astype(F32)).astype(BF16)
        return carry

    lax.fori_loop(0, r_scr.shape[0] // NORM_ROWS, norm_rows, 0, unroll=True)
    lg_ref[...] = (_dot(hs_scr[0], wr_ref[0]) + _dot(hs_scr[1], wr_ref[0])
                   + _dot(hs_scr[0], wr_ref[1]) + br_ref[...])


def _out_proj(ona_l, ona_c, osg, oft_l, oft_c, xall, w_out_bf, gate, lng, lnb, shift, scale,
              wr_split, br_pad, layer, n_rows, n_lat_tiles, with_ctx):
    tm = ROW_TILE
    row = lambda i: (i, 0)
    lat = lambda i: (jnp.minimum(i, n_lat_tiles - 1), 0)
    typ = lambda i: (jnp.where(i >= n_lat_tiles, 1, 0), 0, 0)
    const = lambda i: (0, 0)
    modspec = pl.BlockSpec((1, 1, D_MODEL), typ)
    vec = pl.BlockSpec((1, D_MODEL), const)
    specs = [pl.BlockSpec((tm, NA_WIDTH), lat)]
    args = [ona_l]
    if with_ctx:
        specs.append(pl.BlockSpec((tm, NA_WIDTH), const))
        args.append(ona_c)
    specs.append(pl.BlockSpec((tm, SG_WIDTH), row))
    args.append(osg)
    specs.append(pl.BlockSpec((tm, FT_WIDTH), lat))
    args.append(oft_l)
    if with_ctx:
        specs.append(pl.BlockSpec((tm, FT_WIDTH), const))
        args.append(oft_c)
    specs += [pl.BlockSpec((tm, D_MODEL), row),
              pl.BlockSpec((None, D_MODEL, D_MODEL), lambda i: (layer, 0, 0),
                           pipeline_mode=pl.Buffered(1)),
              modspec, vec, vec, modspec, modspec,
              pl.BlockSpec((None, 2, D_MODEL, LANES), lambda i: (layer, 0, 0, 0)),
              pl.BlockSpec((1, LANES), const)]
    args += [xall, w_out_bf, gate, lng, lnb, shift, scale, wr_split, br_pad]
    return pl.pallas_call(
        functools.partial(_out_kernel, with_ctx=with_ctx, n_lat_tiles=n_lat_tiles),
        grid=(n_rows // tm,),
        in_specs=specs,
        out_specs=[pl.BlockSpec((tm, D_MODEL), row),
                   pl.BlockSpec((tm, D_MODEL // 2), row),
                   pl.BlockSpec((tm, LANES), row)],
        out_shape=[jax.ShapeDtypeStruct((n_rows, D_MODEL), F32),
                   jax.ShapeDtypeStruct((n_rows, D_MODEL // 2), jnp.uint32),
                   jax.ShapeDtypeStruct((n_rows, LANES), F32)],
        scratch_shapes=[pltpu.VMEM((tm, D_MODEL), F32),
                        pltpu.VMEM((2, tm, D_MODEL), BF16)],
        compiler_params=_cparams(("arbitrary",)),
        name="out_proj",
    )(*args)


def _route_kernel(lg_ref, tri_ref, eid_ref, rank_ref, gate_ref, cnt_ref, carry):
    @pl.when(pl.program_id(0) == 0)
    def _():
        carry[...] = jnp.zeros_like(carry)

    work = lg_ref[...]
    lane = lax.broadcasted_iota(jnp.int32, work.shape, 1)
    lane_f = lane.astype(F32)
    vals, ids, hots = [], [], []
    for _ in range(TOP_K):
        m = jnp.max(work, axis=-1, keepdims=True)
        idx = jnp.min(jnp.where(work == m, lane_f, float(LANES)), axis=-1, keepdims=True)
        hot = lane_f == idx
        vals.append(m)
        ids.append(idx)
        hots.append(hot)
        work = jnp.where(hot, -jnp.inf, work)
    exps = [jnp.exp(v - vals[0]) for v in vals]
    den = exps[0] + exps[1] + exps[2] + exps[3]
    multi = jnp.zeros(work.shape, F32)
    for hot in hots:
        multi = multi + hot.astype(F32)
    pref = _dot(tri_ref[...], multi.astype(BF16)) + carry[...]
    eid = jnp.zeros(work.shape, F32)
    rank = jnp.zeros(work.shape, F32)
    gate = jnp.zeros(work.shape, F32)
    for k in range(TOP_K):
        rk = jnp.sum(jnp.where(hots[k], pref, 0.0), axis=-1, keepdims=True)
        sel = lane == k
        eid = jnp.where(sel, ids[k], eid)
        rank = jnp.where(sel, rk, rank)
        gate = jnp.where(sel, exps[k] / den, gate)
    eid_ref[...] = eid.astype(jnp.int32)
    rank_ref[...] = rank.astype(jnp.int32)
    gate_ref[...] = gate
    carry[...] = carry[...] + jnp.sum(multi, axis=0, keepdims=True)
    cnt_ref[...] = jnp.broadcast_to(carry[...], cnt_ref.shape).astype(jnp.int32)


def _route(logits, tri):
    t = logits.shape[0]
    tm = ROW_TILE
    row = lambda i: (i, 0)
    return pl.pallas_call(
        _route_kernel,
        grid=(t // tm,),
        in_specs=[pl.BlockSpec((tm, LANES), row),
                  pl.BlockSpec((tm, tm), lambda i: (0, 0))],
        out_specs=[pl.BlockSpec((tm, LANES), row)] * 3
        + [pl.BlockSpec((8, LANES), lambda i: (0, 0))],
        out_shape=[jax.ShapeDtypeStruct((t, LANES), jnp.int32)] * 2
        + [jax.ShapeDtypeStruct((t, LANES), F32),
           jax.ShapeDtypeStruct((8, LANES), jnp.int32)],
        scratch_shapes=[pltpu.VMEM((1, LANES), F32)],
        compiler_params=_cparams(("arbitrary",)),
        name="route",
    )(logits, tri)


def _dispatch_kernel(pstart_ref, pcount_ref, h_ref, slot_ref, xb_ref, sem, pad_sem):
    tm = h_ref.shape[0]
    first = pl.program_id(0) == 0

    def row_copy(src_row, dst_row, s):
        return pltpu.make_async_copy(h_ref.at[pl.ds(src_row, 1)], xb_ref.at[pl.ds(dst_row, 1)], s)

    def tail_copy(b):
        return pltpu.make_async_copy(h_ref, xb_ref.at[pl.ds(b * MOE_BLOCK, MOE_BLOCK)], pad_sem)

    n_blocks = xb_ref.shape[0] // MOE_BLOCK
    n_used = pcount_ref[N_EXPERTS + 1]

    @pl.when(first)
    def _():
        def per_expert(e, carry):
            def per_row(j, c2):
                row_copy(0, pstart_ref[e] + j, pad_sem).start()
                return c2
            return lax.fori_loop(0, pcount_ref[e], per_row, carry)
        lax.fori_loop(0, N_EXPERTS, per_expert, 0)

        def per_tail(b, carry):
            tail_copy(b).start()
            return carry
        lax.fori_loop(n_used, n_blocks, per_tail, 0)

    def per_token(g, carry):
        for u in range(2):
            t = g * 2 + u
            for k in range(TOP_K):
                row_copy(t, slot_ref[t * TOP_K + k], sem).start(priority=k % 2)
        return carry

    lax.fori_loop(0, tm // 2, per_token, 0)
    for _ in range(TOP_K):
        pltpu.make_async_copy(h_ref, xb_ref.at[pl.ds(0, tm)], sem).wait()

    @pl.when(first)
    def _():
        def wait_row(j, carry):
            row_copy(0, 0, pad_sem).wait()
            return carry
        lax.fori_loop(0, pcount_ref[N_EXPERTS], wait_row, 0)

        def wait_tail(b, carry):
            tail_copy(b).wait()
            return carry
        lax.fori_loop(n_used, n_blocks, wait_tail, 0)


def _dispatch(pstart, pcount, h2, slot_flat, n_slots):
    t = h2.shape[0]
    tm = ROW_TILE
    gs = pltpu.PrefetchScalarGridSpec(
        num_scalar_prefetch=2,
        grid=(t // tm,),
        in_specs=[pl.BlockSpec((tm, h2.shape[1]), lambda i, a, b: (i, 0)),
                  pl.BlockSpec((tm * TOP_K,), lambda i, a, b: (i,), memory_space=pltpu.SMEM)],
        out_specs=pl.BlockSpec(memory_space=pl.ANY),
        scratch_shapes=[pltpu.SemaphoreType.DMA(()), pltpu.SemaphoreType.DMA(())],
    )
    return pl.pallas_call(
        _dispatch_kernel,
        grid_spec=gs,
        out_shape=jax.ShapeDtypeStruct((n_slots, h2.shape[1]), h2.dtype),
        compiler_params=_cparams(("arbitrary",), row_dma=True),
        name="dispatch",
    )(pstart, pcount, h2, slot_flat)


def _expert_kernel(bexp_ref, nused_ref, group_ref, next_ref, xb_ref, wgu_hbm, bgu_ref, wd_hbm,
                   bd_ref, perm_ref, y_ref, wgu_f, wd_f, wgu_s, wd_s, wsem, *, layer):
    b = pl.program_id(0)
    active = b < nused_ref[0]
    blk = xb_ref.shape[0]
    expert = bexp_ref[b]
    fresh = jnp.logical_or(b == 0, expert != bexp_ref[jnp.maximum(b - 1, 0)])
    half = group_ref[b] % 2

    def weight_copies(e, buf):
        return (pltpu.make_async_copy(wgu_hbm.at[layer, e], wgu_f.at[buf], wsem.at[buf, 0]),
                pltpu.make_async_copy(wd_hbm.at[layer, e], wd_f.at[buf], wsem.at[buf, 1]))

    @pl.when(jnp.logical_not(active))
    def _():
        y_ref[...] = jnp.zeros_like(y_ref)

    @pl.when(jnp.logical_and(active, b == 0))
    def _():
        for cp in weight_copies(expert, 0):
            cp.start()

    @pl.when(jnp.logical_and(active, fresh))
    def _():
        for cp in weight_copies(expert, half):
            cp.wait()
        wgu_s[...] = wgu_f[half].astype(BF16)
        wd_s[...] = _dot(perm_ref[...], wd_f[half].astype(BF16)).astype(BF16)

        @pl.when(next_ref[b] >= 0)
        def _():
            for cp in weight_copies(next_ref[b], 1 - half):
                cp.start()

    @pl.when(active)
    def _():
        x_first, x_second = _unpack_halves(xb_ref[...])
        half_k = D_MODEL // 2
        gu = (_dot(x_first.astype(BF16), wgu_s[0:half_k, :])
              + _dot(x_second.astype(BF16), wgu_s[half_k:, :]) + bgu_ref[...])
        lane = lax.broadcasted_iota(jnp.int32, (blk, LANES), 1)
        even = (lane % 2) == 0
        prods = []
        for c in range(2 * D_FF // LANES):
            guc = gu[:, c * LANES:(c + 1) * LANES]
            glu = jnp.minimum(guc, SWIGLU_LIMIT)
            lin = jnp.clip(guc, -SWIGLU_LIMIT, SWIGLU_LIMIT) + 1.0
            prods.append(glu * jax.nn.sigmoid(SWIGLU_ALPHA * glu)
                         * pltpu.roll(lin, LANES - 1, axis=1))
        merged = [jnp.where(even, prods[2 * m], pltpu.roll(prods[2 * m + 1], 1, axis=1))
                  for m in range(D_FF // LANES)]
        act = jnp.concatenate(merged, axis=1).astype(BF16)
        y = _dot(act, wd_s[...]) + bd_ref[...]
        y_ref[...] = _pack_halves(y.astype(BF16).astype(F32))


def _experts(bexp, nused, group, next_expert, xb, w_gate_up, b_gate_up, w_down, b_down, layer,
             n_blocks):
    blk = MOE_BLOCK

    def bmap(b, be, nu, gr, nx):
        return (jnp.minimum(b, nu[0] - 1), 0)

    def emap(b, be, nu, gr, nx):
        return (layer, be[jnp.minimum(b, nu[0] - 1)], 0, 0)

    gs = pltpu.PrefetchScalarGridSpec(
        num_scalar_prefetch=4,
        grid=(n_blocks,),
        in_specs=[pl.BlockSpec((blk, D_MODEL // 2), bmap),
                  pl.BlockSpec(memory_space=pl.ANY),
                  pl.BlockSpec((None, None, 1, 2 * D_FF), emap),
                  pl.BlockSpec(memory_space=pl.ANY),
                  pl.BlockSpec((None, None, 1, D_MODEL), emap),
                  pl.BlockSpec((D_FF, D_FF), lambda b, be, nu, gr, nx: (0, 0))],
        out_specs=pl.BlockSpec((blk, D_MODEL // 2), lambda b, be, nu, gr, nx: (b, 0)),
        scratch_shapes=[pltpu.VMEM((2, D_MODEL, 2 * D_FF), F32),
                        pltpu.VMEM((2, D_FF, D_MODEL), F32),
                        pltpu.VMEM((D_MODEL, 2 * D_FF), BF16),
                        pltpu.VMEM((D_FF, D_MODEL), BF16),
                        pltpu.SemaphoreType.DMA((2, 2))],
    )
    lane = np.arange(D_FF) % LANES
    unit = (np.arange(D_FF) // LANES) * LANES + lane // 2 + (lane % 2) * (LANES // 2)
    perm = np.zeros((D_FF, D_FF), np.float32)
    perm[np.arange(D_FF), unit] = 1.0
    return pl.pallas_call(
        functools.partial(_expert_kernel, layer=layer),
        grid_spec=gs,
        out_shape=jax.ShapeDtypeStruct(xb.shape, jnp.uint32),
        compiler_params=_cparams(("arbitrary",)),
        name="experts",
    )(bexp, nused, group, next_expert, xb, w_gate_up, b_gate_up[:, :, None, :],
      w_down, b_down[:, :, None, :], jnp.asarray(perm, BF16))


def _combine_kernel(x1_ref, slot_ref, slot_next_ref, gate_ref, yb_ref, g_ref, lng, lnb,
                    o_ref, gbuf, sems):
    tm = x1_ref.shape[0]
    i = pl.program_id(0)
    cur = i % 2

    def row_copy(src_row, buf, k, t):
        return pltpu.make_async_copy(yb_ref.at[pl.ds(src_row, 1)],
                                     gbuf.at[buf, k, pl.ds(t, 1)], sems.at[buf])

    def gather(slots, buf):
        def per_token(g, carry):
            for u in range(2):
                t = g * 2 + u
                for k in range(TOP_K):
                    row_copy(slots[t * TOP_K + k], buf, k, t).start(priority=k % 2)
            return carry
        lax.fori_loop(0, tm // 2, per_token, 0)

    @pl.when(i == 0)
    def _():
        gather(slot_ref, 0)

    @pl.when(i + 1 < pl.num_programs(0))
    def _():
        gather(slot_next_ref, 1 - cur)

    for k in range(TOP_K):
        pltpu.make_async_copy(yb_ref.at[pl.ds(0, tm)], gbuf.at[cur, k], sems.at[cur]).wait()

    def norm_rows(ci, carry):
        rows = pl.ds(pl.multiple_of(ci * NORM_ROWS, NORM_ROWS), NORM_ROWS)
        gate = gate_ref[rows, :]
        f_first = f_second = None
        for k in range(TOP_K):
            y_first, y_second = _unpack_halves(gbuf[cur, k, rows, :])
            gk = gate[:, k:k + 1]
            f_first = gk * y_first if k == 0 else f_first + gk * y_first
            f_second = gk * y_second if k == 0 else f_second + gk * y_second
        f = jnp.concatenate([f_first, f_second], axis=1)
        r = ALPHA * x1_ref[rows, :] + g_ref[0] * f
        o_ref[rows, :] = _ln_rows(r) * lng[...] + lnb[...]
        return carry

    lax.fori_loop(0, tm // NORM_ROWS, norm_rows, 0, unroll=True)


def _combine(x1, slot_flat, gates, yb, gate_mod, lng, lnb, n_lat_tiles):
    t = x1.shape[0]
    tm = ROW_TILE
    n_tiles = t // tm
    return pl.pallas_call(
        _combine_kernel,
        grid=(n_tiles,),
        in_specs=[pl.BlockSpec((tm, D_MODEL), lambda i: (i, 0)),
                  pl.BlockSpec((tm * TOP_K,), lambda i: (i,), memory_space=pltpu.SMEM),
                  pl.BlockSpec((tm * TOP_K,), lambda i: (jnp.minimum(i + 1, n_tiles - 1),),
                               memory_space=pltpu.SMEM),
                  pl.BlockSpec((tm, LANES), lambda i: (i, 0)),
                  pl.BlockSpec(memory_space=pl.ANY),
                  pl.BlockSpec((1, 1, D_MODEL),
                               lambda i: (jnp.where(i >= n_lat_tiles, 1, 0), 0, 0)),
                  pl.BlockSpec((1, D_MODEL), lambda i: (0, 0)),
                  pl.BlockSpec((1, D_MODEL), lambda i: (0, 0))],
        out_specs=pl.BlockSpec((tm, D_MODEL), lambda i: (i, 0)),
        out_shape=jax.ShapeDtypeStruct((t, D_MODEL), F32),
        scratch_shapes=[pltpu.VMEM((2, TOP_K, tm, yb.shape[1]), yb.dtype),
                        pltpu.SemaphoreType.DMA((2,))],
        compiler_params=_cparams(("arbitrary",), row_dma=True),
        name="combine",
    )(x1, slot_flat, slot_flat, gates, yb, gate_mod, lng, lnb)


def _rope_tables(s_len, cl):
    t = jnp.arange(s_len, dtype=jnp.int32)
    quarter = HEAD_DIM // 4
    inv = ROPE_BASE ** (-jnp.arange(quarter, dtype=F32) / quarter)
    ang_r = (t // GRID_W).astype(F32)[:, None] * inv
    ang_c = (t % GRID_W).astype(F32)[:, None] * inv
    cr, sr, cc, sc = jnp.cos(ang_r), jnp.sin(ang_r), jnp.cos(ang_c), jnp.sin(ang_c)
    cos_t = jnp.concatenate([cr, cr, cc, cc], axis=1)
    sin_t = jnp.concatenate([-sr, sr, -sc, sc], axis=1)
    cos_t = jnp.concatenate([cos_t, jnp.ones((cl, HEAD_DIM), F32)], axis=0)
    sin_t = jnp.concatenate([sin_t, jnp.zeros((cl, HEAD_DIM), F32)], axis=0)
    return cos_t, sin_t


def _block_plan(counts, eid, rank, n_blocks):
    blocks_per = (counts + MOE_BLOCK - 1) // MOE_BLOCK
    block_end = jnp.cumsum(blocks_per)
    base = (block_end - blocks_per) * MOE_BLOCK
    blocks = jnp.arange(n_blocks)
    bexp = jnp.minimum(jnp.sum(block_end[None, :] <= blocks[:, None], axis=1),
                       N_EXPERTS - 1).astype(jnp.int32)
    nused = block_end[-1:].astype(jnp.int32)
    experts = jnp.arange(N_EXPERTS, dtype=jnp.int32)
    first = jnp.concatenate([jnp.ones((1,), bool), bexp[1:] != bexp[:-1]])
    group = (jnp.cumsum(first) - 1).astype(jnp.int32)
    group_end = jnp.sum(jnp.where(bexp[:, None] == experts, block_end, 0), axis=-1)
    follower = jnp.sum(jnp.where(blocks[None, :] == group_end[:, None], bexp[None, :], 0), axis=-1)
    next_expert = jnp.where(group_end < nused[0], follower, -1).astype(jnp.int32)
    slot = rank + jnp.sum(jnp.where(eid[:, :, None] == experts, base, 0), axis=-1)
    pad = blocks_per * MOE_BLOCK - counts
    pcount = jnp.concatenate([pad, jnp.sum(pad)[None], nused]).astype(jnp.int32)
    pstart = (base + counts).astype(jnp.int32)
    return (bexp, nused, group, next_expert, slot.reshape(-1).astype(jnp.int32), pstart,
            pcount)


def kernel(x, c, ctx, c_ctx, w_ada, b_ada, w_in, w_out, sg_w, sg_b, na_rpb, ln1_g, ln1_b,
           ln2_g, ln2_b, w_router, b_router, w_gate_up, b_gate_up, w_down, b_down):
    bsz, s_len, dm = x.shape
    cl = ctx.shape[1]
    depth = w_ada.shape[0]
    assert bsz == 1 and dm == D_MODEL and cl == ROW_TILE and MOE_BLOCK == ROW_TILE
    assert s_len % (GRID_W * KEY_ROWS) == 0 and s_len % ROW_TILE == 0
    t_all = s_len + cl
    n_lat_tiles = s_len // ROW_TILE
    rows = s_len // GRID_W

    cond = jnp.zeros((8, dm), F32).at[0].set(c[0]).at[1].set(c_ctx)
    mod = _ada_mod(cond, w_ada, b_ada)[:, :2].reshape(depth, 2, 6, 1, dm)
    cos_t, sin_t = _rope_tables(s_len, cl)
    pb, vrow = _na_bias_tables(na_rpb, rows)
    cs, m1, tw, mc = _fourier_consts(s_len, cl)
    tri = jnp.asarray(np.tril(np.ones((ROW_TILE, ROW_TILE), np.float32), -1), BF16)
    wr_pad = jnp.pad(w_router, ((0, 0), (0, 0), (0, LANES - N_EXPERTS)))
    wr_hi = wr_pad.astype(BF16)
    wr_split = jnp.stack([wr_hi, (wr_pad - wr_hi.astype(F32)).astype(BF16)], axis=1)
    br_pad = jnp.pad(b_router, ((0, 0), (0, LANES - N_EXPERTS)), constant_values=NEG_INF)
    w_in_bf = w_in.astype(BF16)
    w_out_bf = w_out.astype(BF16)
    sgw_bf = sg_w.astype(BF16)

    xall = jnp.concatenate([x[0], ctx[0]], axis=0)
    for l in range(depth):
        last = l == depth - 1
        m = lambda j: mod[l, :, j]
        qp, qr, kr, v, z_sg, z_ft = _proj(xall, m(0), m(1), w_in_bf, cos_t, sin_t, l,
                                          n_lat_tiles)
        ona_l = _na_attention(qr, qp, kr, v, pb, vrow, l, s_len)
        osg = _spatial_gating(z_sg, sgw_bf[l], sg_b[l][:, :, None])
        oft_l = _fourier_latent(z_ft, cs, m1, tw, s_len)
        if last:
            ona_c = oft_c = None
            n_rows = s_len
        else:
            ona_c = _ctx_attention(qp, kr, v, s_len, cl)
            oft_c = _fourier_ctx(z_ft, cs, mc, s_len, cl)
            n_rows = t_all
        x1, h2, logits = _out_proj(
            ona_l, ona_c, osg, oft_l, oft_c, xall, w_out_bf, m(2),
            ln1_g[l][None], ln1_b[l][None], m(3), m(4), wr_split, br_pad[l][None],
            l, n_rows, n_lat_tiles, not last)
        eid, rank, gates, counts = _route(logits, tri)
        n_blocks = -(-n_rows * TOP_K // MOE_BLOCK) + N_EXPERTS
        bexp, nused, group, next_expert, slot_flat, pstart, pcount = _block_plan(
            counts[0, :N_EXPERTS], eid[:, :TOP_K], rank[:, :TOP_K], n_blocks)
        xb = _dispatch(pstart, pcount, h2, slot_flat, n_blocks * MOE_BLOCK)
        yb = _experts(bexp, nused, group, next_expert, xb, w_gate_up, b_gate_up, w_down, b_down,
                      l, n_blocks)
        xall = _combine(x1, slot_flat, gates, yb, m(5), ln2_g[l][None], ln2_b[l][None],
                        n_lat_tiles)
    return xall[None]
```

```python
import functools

import numpy as np
import jax
import jax.numpy as jnp
from jax import lax
from jax.experimental import pallas as pl
from jax.experimental.pallas import tpu as pltpu

D_MODEL = 2048
DEPTH_NORM = 4
GRID_W = 64
HEAD_DIM = 128
NA_HEADS = 8
NA_WIDTH = NA_HEADS * HEAD_DIM
NA_KH = 8
NA_KW = 16
SG_GROUPS = 4
SG_WIDTH = 512
SG_CHUNK = 128
FT_GROUPS = 4
FT_WIDTH = 512
SG_OFF = 3 * NA_WIDTH
FT_OFF = SG_OFF + 2 * SG_WIDTH
IN_WIDTH = FT_OFF + FT_WIDTH
ROPE_BASE = 10000.0
N_EXPERTS = 32
TOP_K = 4
D_FF = D_MODEL // 4
SWIGLU_ALPHA = 1.702
SWIGLU_LIMIT = 7.0
LN_EPS = 1e-5
NEG_INF = -1e30
ALPHA = (2 * DEPTH_NORM) ** 0.25

LANES = 128
ROW_TILE = 256
MOE_BLOCK = 256
NORM_ROWS = 32
KEY_ROWS = 16
Q_ROWS = 8
NA_PAIRS = 5
NA_HEADS_PER_STEP = 4
VMEM_LIMIT = 56 * 1024 * 1024

F32 = jnp.float32
BF16 = jnp.bfloat16


def _cparams(sem, row_dma=False):
    return pltpu.CompilerParams(dimension_semantics=sem, vmem_limit_bytes=VMEM_LIMIT,
                                disable_bounds_checks=row_dma)


def _dot(a, b):
    return jnp.dot(a, b, preferred_element_type=F32)


def _dot_nt(a, b):
    return lax.dot_general(a, b, (((1,), (1,)), ((), ())), preferred_element_type=F32)


def _ln_rows(x):
    mu = jnp.mean(x, axis=-1, keepdims=True)
    xc = x - mu
    var = jnp.mean(xc * xc, axis=-1, keepdims=True)
    return xc * lax.rsqrt(var + LN_EPS)


def _ada_kernel(c_ref, w_ref, b_ref, o_ref):
    c = c_ref[...]
    s = c * jax.nn.sigmoid(c)
    o_ref[0] = jnp.dot(s, w_ref[0], preferred_element_type=F32,
                       precision=lax.Precision.HIGHEST) + b_ref[0]


def _ada_mod(cond, w_ada, b_ada):
    depth, d, n = w_ada.shape
    tn = 1536
    return pl.pallas_call(
        _ada_kernel,
        grid=(depth, n // tn),
        in_specs=[pl.BlockSpec((8, d), lambda l, j: (0, 0)),
                  pl.BlockSpec((1, d, tn), lambda l, j: (l, 0, j)),
                  pl.BlockSpec((1, 1, tn), lambda l, j: (l, 0, j))],
        out_specs=pl.BlockSpec((1, 8, tn), lambda l, j: (l, 0, j)),
        out_shape=jax.ShapeDtypeStruct((depth, 8, n), F32),
        compiler_params=_cparams(("arbitrary", "arbitrary")),
        name="ada_mod",
    )(cond, w_ada, b_ada.reshape(depth, 1, n))


def _proj_kernel(x_ref, sh_ref, sc_ref, w_ref, cos_ref, sin_ref,
                 qp_ref, qr_ref, kr_ref, v_ref, sg_ref, ft_ref):
    y = _ln_rows(x_ref[...])
    h = (y * (1.0 + sc_ref[0]) + sh_ref[0]).astype(BF16)
    cos = cos_ref[...]
    sin = sin_ref[...]
    lane = lax.broadcasted_iota(jnp.int32, cos.shape, 1)
    first = (lane % 64) < 32

    def rope(z):
        swapped = jnp.where(first, pltpu.roll(z, 96, axis=1), pltpu.roll(z, 32, axis=1))
        return z * cos + swapped * sin

    nw = 512
    for j in range(IN_WIDTH // nw):
        z = _dot(h, w_ref[:, j * nw:(j + 1) * nw])
        for p in range(nw // LANES):
            col = j * nw + p * LANES
            zp = z[:, p * LANES:(p + 1) * LANES]
            if col < NA_WIDTH:
                zp = zp * (HEAD_DIM ** -0.5)
                qp_ref[:, col:col + LANES] = zp.astype(BF16)
                qr_ref[:, col:col + LANES] = rope(zp).astype(BF16)
            elif col < 2 * NA_WIDTH:
                c0 = col - NA_WIDTH
                kr_ref[:, c0:c0 + LANES] = rope(zp).astype(BF16)
            elif col < SG_OFF:
                c0 = col - 2 * NA_WIDTH
                v_ref[:, c0:c0 + LANES] = zp.astype(BF16)
            elif col < FT_OFF:
                c0 = col - SG_OFF
                sg_ref[:, c0:c0 + LANES] = zp
            else:
                c0 = col - FT_OFF
                ft_ref[:, c0:c0 + LANES] = zp.astype(BF16)


def _proj(xall, shift, scale, w_in_bf, cos_t, sin_t, layer, n_lat_tiles):
    t = xall.shape[0]
    tm = ROW_TILE
    typ = lambda i: (jnp.where(i >= n_lat_tiles, 1, 0), 0, 0)
    row = lambda i: (i, 0)
    return pl.pallas_call(
        _proj_kernel,
        grid=(t // tm,),
        in_specs=[pl.BlockSpec((tm, D_MODEL), row),
                  pl.BlockSpec((1, 1, D_MODEL), typ),
                  pl.BlockSpec((1, 1, D_MODEL), typ),
                  pl.BlockSpec((None, D_MODEL, IN_WIDTH), lambda i: (layer, 0, 0),
                               pipeline_mode=pl.Buffered(1)),
                  pl.BlockSpec((tm, LANES), row),
                  pl.BlockSpec((tm, LANES), row)],
        out_specs=[pl.BlockSpec((tm, NA_WIDTH), row)] * 4
        + [pl.BlockSpec((tm, 2 * SG_WIDTH), row), pl.BlockSpec((tm, FT_WIDTH), row)],
        out_shape=[jax.ShapeDtypeStruct((t, NA_WIDTH), BF16)] * 4
        + [jax.ShapeDtypeStruct((t, 2 * SG_WIDTH), F32),
           jax.ShapeDtypeStruct((t, FT_WIDTH), BF16)],
        compiler_params=_cparams(("arbitrary",)),
        name="ln_proj",
    )(xall, shift, scale, w_in_bf, cos_t, sin_t)


def _na_kernel(qr_ref, qp_ref, k0, k1, k2, k3, v0, v1, v2, v3, kc_ref, vc_ref, pb_ref, vrow_ref,
               o_ref, k_scr, v_scr, s_scr, p_scr, den_scr, *, n_rb):
    i = pl.program_id(1)
    is_first = i == 0
    is_last = i == n_rb - 1
    off = jnp.where(is_first, 0, jnp.where(is_last, -(KEY_ROWS - Q_ROWS), -(NA_KH // 2)))
    kb = k0.shape[0]
    for j, (kj, vj) in enumerate(((k0, v0), (k1, v1), (k2, v2), (k3, v3))):
        k_scr[j * kb:(j + 1) * kb, :] = kj[...]
        v_scr[j * kb:(j + 1) * kb, :] = vj[...]
    qsub = 128
    pair = 2 * GRID_W
    starts = ((0, 0, 0, 1), (0, 1, 2, 3), (2, 3, 3, 3))
    n_sub = Q_ROWS * GRID_W // qsub
    n_lat = NA_PAIRS * pair
    units = [(hh, s) for hh in range(NA_HEADS_PER_STEP) for s in range(n_sub)]
    p0s, key0 = [], []
    for s in range(n_sub):
        p0 = jnp.where(is_first, starts[0][s], jnp.where(is_last, starts[2][s], starts[1][s]))
        p0s.append(p0)
        key0.append(pl.multiple_of(p0 * pair, pair))
    for u, (hh, s) in enumerate(units):
        rows = slice(s * qsub, (s + 1) * qsub)
        cols = slice(hh * HEAD_DIM, (hh + 1) * HEAD_DIM)
        bias_rows = []
        for qr in range(s * qsub // GRID_W, (s + 1) * qsub // GRID_W):
            tiles = [pb_ref[0, hh, jnp.clip(2 * (p0s[s] + j) - qr + off + NA_KH, 0, 2 * NA_KH - 1)]
                     + vrow_ref[0, qr, p0s[s] + j]
                     for j in range(NA_PAIRS)]
            bias_rows.append(jnp.concatenate(tiles, axis=1))
        bias = jnp.concatenate(bias_rows, axis=0)
        s_scr[u, :, 0:n_lat] = _dot_nt(qr_ref[rows, cols],
                                       k_scr[pl.ds(key0[s], n_lat), cols]) + bias
        s_scr[u, :, n_lat:] = _dot_nt(qp_ref[rows, cols], kc_ref[:, cols])
    for u in range(len(units)):
        sc = s_scr[u]
        p = jnp.exp(sc - jnp.max(sc, axis=-1, keepdims=True))
        p_scr[u] = p.astype(BF16)
        den_scr[u] = jnp.sum(p, axis=-1, keepdims=True)
    for u, (hh, s) in enumerate(units):
        rows = slice(s * qsub, (s + 1) * qsub)
        cols = slice(hh * HEAD_DIM, (hh + 1) * HEAD_DIM)
        o = (_dot(p_scr[u, :, 0:n_lat], v_scr[pl.ds(key0[s], n_lat), cols])
             + _dot(p_scr[u, :, n_lat:], vc_ref[:, cols]))
        o_ref[rows, cols] = (o / den_scr[u]).astype(BF16)


def _na_attention(qr, qp, kr, v, pb, vrow, layer, s_len):
    rows = s_len // GRID_W
    n_rb = rows // Q_ROWS
    qb = Q_ROWS * GRID_W
    kb = 256
    last_kblock = (rows - KEY_ROWS) * GRID_W // kb
    ctx_block = s_len // kb

    def kmap(j):
        return lambda h, i: (jnp.clip(2 * i - 1, 0, last_kblock) + j, h)

    def btype(i):
        return jnp.where(i == 0, 0, jnp.where(i == n_rb - 1, 2, 1))

    hw = NA_HEADS_PER_STEP * HEAD_DIM
    n_units = NA_HEADS_PER_STEP * qb // 128
    n_keys = NA_PAIRS * 2 * GRID_W + kb
    qspec = pl.BlockSpec((qb, hw), lambda h, i: (i, h))
    kspecs = [pl.BlockSpec((kb, hw), kmap(j)) for j in range(4)]
    cspec = pl.BlockSpec((kb, hw), lambda h, i: (ctx_block, h))
    return pl.pallas_call(
        functools.partial(_na_kernel, n_rb=n_rb),
        grid=(NA_HEADS // NA_HEADS_PER_STEP, n_rb),
        in_specs=[qspec, qspec] + kspecs + kspecs + [cspec, cspec]
        + [pl.BlockSpec((1, NA_HEADS_PER_STEP, 2 * NA_KH, GRID_W, 2 * GRID_W),
                        lambda h, i: (layer, h, 0, 0, 0)),
           pl.BlockSpec((1, Q_ROWS, KEY_ROWS // 2, 1, 2 * GRID_W),
                        lambda h, i: (btype(i), 0, 0, 0, 0))],
        out_specs=pl.BlockSpec((qb, hw), lambda h, i: (i, h)),
        out_shape=jax.ShapeDtypeStruct((s_len, NA_WIDTH), BF16),
        scratch_shapes=[pltpu.VMEM((KEY_ROWS * GRID_W, hw), BF16),
                        pltpu.VMEM((KEY_ROWS * GRID_W, hw), BF16),
                        pltpu.VMEM((n_units, 128, n_keys), F32),
                        pltpu.VMEM((n_units, 128, n_keys), BF16),
                        pltpu.VMEM((n_units, 128, 1), F32)],
        compiler_params=_cparams(("arbitrary", "arbitrary")),
        name="na_attention",
    )(qr, qp, kr, kr, kr, kr, v, v, v, v, kr, v, pb, vrow)


def _ctx_attn_kernel(q_ref, k_ref, v_ref, o_ref):
    s = _dot_nt(q_ref[...], k_ref[...])
    m = jnp.max(s, axis=-1, keepdims=True)
    p = jnp.exp(s - m)
    den = jnp.sum(p, axis=-1, keepdims=True)
    o_ref[...] = (_dot(p.astype(BF16), v_ref[...]) / den).astype(BF16)


def _ctx_attention(qp, kr, v, s_len, cl):
    blk = s_len // cl
    spec = pl.BlockSpec((cl, HEAD_DIM), lambda h: (blk, h))
    return pl.pallas_call(
        _ctx_attn_kernel,
        grid=(NA_HEADS,),
        in_specs=[spec, spec, spec],
        out_specs=pl.BlockSpec((cl, HEAD_DIM), lambda h: (0, h)),
        out_shape=jax.ShapeDtypeStruct((cl, NA_WIDTH), BF16),
        compiler_params=_cparams(("arbitrary",)),
        name="ctx_attention",
    )(qp, kr, v)


def _na_bias_tables(na_rpb, rows):
    n_rb = rows // Q_ROWS
    rmask = np.zeros((3, Q_ROWS, KEY_ROWS), bool)
    for t, i in enumerate((0, 1, n_rb - 1)):
        ks = min(max(Q_ROWS * i - NA_KH // 2, 0), rows - KEY_ROWS)
        for qr in range(Q_ROWS):
            r = Q_ROWS * i + qr
            lo = min(max(r - NA_KH // 2, 0), rows - NA_KH)
            for kr in range(KEY_ROWS):
                rmask[t, qr, kr] = lo <= ks + kr < lo + NA_KH
    vrow = np.where(np.repeat(rmask, GRID_W, axis=2), 0.0, NEG_INF).astype(np.float32)
    csel = np.zeros((GRID_W, GRID_W, 2 * NA_KW - 1), np.float32)
    cmask = np.zeros((GRID_W, GRID_W), bool)
    for qc in range(GRID_W):
        lo = min(max(qc - NA_KW // 2, 0), GRID_W - NA_KW)
        for kc in range(GRID_W):
            cmask[qc, kc] = lo <= kc < lo + NA_KW
            csel[qc, kc, min(max(kc - qc + NA_KW - 1, 0), 2 * NA_KW - 2)] = 1.0
    b = jnp.einsum('lhab,qkb->lhaqk', na_rpb, jnp.asarray(csel), precision=lax.Precision.HIGHEST)
    b = jnp.where(jnp.asarray(cmask), b, NEG_INF)
    b = jnp.pad(b, ((0, 0), (0, 0), (1, 1), (0, 0), (0, 0)))
    pb = jnp.concatenate([b[:, :, :-1], b[:, :, 1:]], axis=-1)
    return pb, jnp.asarray(vrow).reshape(3, Q_ROWS, KEY_ROWS // 2, 1, 2 * GRID_W)


def _sg_kernel(z_ref, w_ref, b_ref, o_ref):
    z = z_ref[...]
    g = 0.5 * z * (1.0 + lax.erf(z * (2.0 ** -0.5)))
    n_chunks = z.shape[0] // SG_CHUNK
    for gi in range(SG_GROUPS):
        u = g[:, gi * LANES:(gi + 1) * LANES]
        vn = _ln_rows(g[:, SG_WIDTH + gi * LANES:SG_WIDTH + (gi + 1) * LANES]).astype(BF16)
        for n in range(n_chunks):
            rows = slice(n * SG_CHUNK, (n + 1) * SG_CHUNK)
            t = _dot(w_ref[gi], vn[rows, :]) + b_ref[gi]
            o_ref[rows, gi * LANES:(gi + 1) * LANES] = (u[rows, :] * t).astype(BF16)


def _spatial_gating(z_sg, sgw_bf, sgb):
    t = z_sg.shape[0]
    tm = ROW_TILE
    return pl.pallas_call(
        _sg_kernel,
        grid=(t // tm,),
        in_specs=[pl.BlockSpec((tm, 2 * SG_WIDTH), lambda i: (i, 0)),
                  pl.BlockSpec((SG_GROUPS, SG_CHUNK, SG_CHUNK), lambda i: (0, 0, 0)),
                  pl.BlockSpec((SG_GROUPS, SG_CHUNK, 1), lambda i: (0, 0, 0))],
        out_specs=pl.BlockSpec((tm, SG_WIDTH), lambda i: (i, 0)),
        out_shape=jax.ShapeDtypeStruct((t, SG_WIDTH), BF16),
        compiler_params=_cparams(("arbitrary",)),
        name="spatial_gating",
    )(z_sg, sgw_bf, sgb)


def _ft1_kernel(z_ref, cs_ref, m1_ref, y_ref):
    n1 = z_ref.shape[0]
    parts = []
    for j in range(z_ref.shape[1] // LANES):
        ab = _dot(z_ref[:, j * LANES:(j + 1) * LANES], cs_ref[...])
        parts.append(jnp.concatenate([ab[:, :LANES], ab[:, LANES:]], axis=0))
    st = jnp.concatenate(parts, axis=1).astype(BF16)
    y = _dot(m1_ref[...], st)
    y_ref[0] = y[:n1].astype(BF16)
    y_ref[1] = y[n1:].astype(BF16)


def _ft2_kernel(y_ref, tw_ref, o_ref, *, scale):
    for j in range(y_ref.shape[1]):
        yy = jnp.concatenate([y_ref[0, j], y_ref[1, j]], axis=0)
        o = _dot(tw_ref[j], yy) * scale
        o_ref[:, j * FT_WIDTH:(j + 1) * FT_WIDTH] = o.astype(BF16)


def _fourier_consts(s_len, cl):
    n2 = 128
    n1 = s_len // n2
    c = np.arange(HEAD_DIM)
    ang = 2 * np.pi * ((c[:, None] * c[None, :]) % HEAD_DIM) / HEAD_DIM
    cs = np.concatenate([np.cos(ang), np.sin(ang)], axis=1)
    a = np.arange(n1)
    ang1 = 2 * np.pi * ((a[:, None] * a[None, :]) % n1) / n1
    fc, fs = np.cos(ang1), np.sin(ang1)
    m1 = np.block([[fc, -fs], [-fs, -fc]])
    ka = jnp.arange(n1, dtype=jnp.int32)[:, None, None]
    kb = jnp.arange(n2, dtype=jnp.int32)[None, :, None]
    nn = jnp.arange(n2, dtype=jnp.int32)[None, None, :]
    ph = (nn * (ka + n1 * kb)) % s_len
    th = ph.astype(F32) * (2 * np.pi / s_len)
    tw = jnp.concatenate([jnp.cos(th), jnp.sin(th)], axis=2).astype(BF16)
    p = np.arange(cl)
    angc = 2 * np.pi * ((p[:, None] * p[None, :]) % cl) / cl
    mc = np.concatenate([np.cos(angc), -np.sin(angc)], axis=1)
    to_bf16 = lambda t: jnp.asarray(t, F32).astype(BF16)
    return (to_bf16(cs), to_bf16(m1), tw, to_bf16(mc))


def _fourier_latent(z_ft, cs, m1, tw, s_len):
    n2 = 128
    n1 = s_len // n2
    width = n2 * FT_WIDTH
    cw = 2048
    y = pl.pallas_call(
        _ft1_kernel,
        grid=(width // cw,),
        in_specs=[pl.BlockSpec((n1, cw), lambda j: (0, j)),
                  pl.BlockSpec((HEAD_DIM, 2 * HEAD_DIM), lambda j: (0, 0)),
                  pl.BlockSpec((2 * n1, 2 * n1), lambda j: (0, 0))],
        out_specs=pl.BlockSpec((2, n1, cw), lambda j: (0, 0, j)),
        out_shape=jax.ShapeDtypeStruct((2, n1, width), BF16),
        compiler_params=_cparams(("arbitrary",)),
        name="fourier_stage1",
    )(z_ft[:s_len].reshape(n1, width), cs, m1)
    kab = 4
    out = pl.pallas_call(
        functools.partial(_ft2_kernel, scale=(s_len * HEAD_DIM) ** -0.5),
        grid=(n1 // kab,),
        in_specs=[pl.BlockSpec((2, kab, n2, FT_WIDTH), lambda j: (0, j, 0, 0)),
                  pl.BlockSpec((kab, n2, 2 * n2), lambda j: (j, 0, 0))],
        out_specs=pl.BlockSpec((n2, kab * FT_WIDTH), lambda j: (0, j)),
        out_shape=jax.ShapeDtypeStruct((n2, n1 * FT_WIDTH), BF16),
        compiler_params=_cparams(("arbitrary",)),
        name="fourier_stage2",
    )(y.reshape(2, n1, n2, FT_WIDTH), tw)
    return out.reshape(s_len, FT_WIDTH)


def _ft_ctx_kernel(z_ref, cs_ref, mc_ref, o_ref, *, scale):
    for g in range(FT_GROUPS):
        ab = _dot(z_ref[:, g * LANES:(g + 1) * LANES], cs_ref[...])
        st = jnp.concatenate([ab[:, :LANES], ab[:, LANES:]], axis=0).astype(BF16)
        o_ref[:, g * LANES:(g + 1) * LANES] = (_dot(mc_ref[...], st) * scale).astype(BF16)


def _fourier_ctx(z_ft, cs, mc, s_len, cl):
    blk = s_len // cl
    return pl.pallas_call(
        functools.partial(_ft_ctx_kernel, scale=(cl * HEAD_DIM) ** -0.5),
        grid=(1,),
        in_specs=[pl.BlockSpec((cl, FT_WIDTH), lambda i: (blk, 0)),
                  pl.BlockSpec((HEAD_DIM, 2 * HEAD_DIM), lambda i: (0, 0)),
                  pl.BlockSpec((cl, 2 * cl), lambda i: (0, 0))],
        out_specs=pl.BlockSpec((cl, FT_WIDTH), lambda i: (0, 0)),
        out_shape=jax.ShapeDtypeStruct((cl, FT_WIDTH), BF16),
        compiler_params=_cparams(("arbitrary",)),
        name="fourier_ctx",
    )(z_ft, cs, mc)


def _out_kernel(*refs, with_ctx, n_lat_tiles):
    if with_ctx:
        (ona_l, ona_c, osg, oft_l, oft_c, x_ref, w_ref, g_ref, lng, lnb, sh_ref, sc_ref,
         wr_ref, br_ref, x1_ref, h2_ref, lg_ref, r_scr, hs_scr) = refs
        is_ctx = pl.program_id(0) >= n_lat_tiles
        ona = jnp.where(is_ctx, ona_c[...], ona_l[...])
        oft = jnp.where(is_ctx, oft_c[...], oft_l[...])
    else:
        (ona_l, osg, oft_l, x_ref, w_ref, g_ref, lng, lnb, sh_ref, sc_ref,
         wr_ref, br_ref, x1_ref, h2_ref, lg_ref, r_scr, hs_scr) = refs
        ona = ona_l[...]
        oft = oft_l[...]
    sg = osg[...]
    nw = 512
    for n in range(D_MODEL // nw):
        cols = slice(n * nw, (n + 1) * nw)
        y = (_dot(ona, w_ref[0:NA_WIDTH, cols])
             + _dot(sg, w_ref[NA_WIDTH:NA_WIDTH + SG_WIDTH, cols])
             + _dot(oft, w_ref[NA_WIDTH + SG_WIDTH:, cols]))
        r_scr[:, cols] = ALPHA * x_ref[:, cols] + g_ref[0][:, cols] * y

    def norm_rows(ci, carry):
        rows = pl.ds(pl.multiple_of(ci * NORM_ROWS, NORM_ROWS), NORM_ROWS)
        x1 = _ln_rows(r_scr[rows, :]) * lng[...] + lnb[...]
        x1_ref[rows, :] = x1
        h2 = _ln_rows(x1) * (1.0 + sc_ref[0]) + sh_ref[0]
        h2_ref[rows, :] = h2
        hi = h2.astype(BF16)
        hs_scr[0, rows, :] = hi
        hs_scr[1, rows, :] = (h2 - hi.astype(F32)).astype(BF16)
        return carry

    lax.fori_loop(0, r_scr.shape[0] // NORM_ROWS, norm_rows, 0, unroll=True)
    lg_ref[...] = (_dot(hs_scr[0], wr_ref[0]) + _dot(hs_scr[1], wr_ref[0])
                   + _dot(hs_scr[0], wr_ref[1]) + br_ref[...])


def _out_proj(ona_l, ona_c, osg, oft_l, oft_c, xall, w_out_bf, gate, lng, lnb, shift, scale,
              wr_split, br_pad, layer, n_rows, n_lat_tiles, with_ctx):
    tm = ROW_TILE
    row = lambda i: (i, 0)
    lat = lambda i: (jnp.minimum(i, n_lat_tiles - 1), 0)
    typ = lambda i: (jnp.where(i >= n_lat_tiles, 1, 0), 0, 0)
    const = lambda i: (0, 0)
    modspec = pl.BlockSpec((1, 1, D_MODEL), typ)
    vec = pl.BlockSpec((1, D_MODEL), const)
    specs = [pl.BlockSpec((tm, NA_WIDTH), lat)]
    args = [ona_l]
    if with_ctx:
        specs.append(pl.BlockSpec((tm, NA_WIDTH), const))
        args.append(ona_c)
    specs.append(pl.BlockSpec((tm, SG_WIDTH), row))
    args.append(osg)
    specs.append(pl.BlockSpec((tm, FT_WIDTH), lat))
    args.append(oft_l)
    if with_ctx:
        specs.append(pl.BlockSpec((tm, FT_WIDTH), const))
        args.append(oft_c)
    specs += [pl.BlockSpec((tm, D_MODEL), row),
              pl.BlockSpec((None, D_MODEL, D_MODEL), lambda i: (layer, 0, 0),
                           pipeline_mode=pl.Buffered(1)),
              modspec, vec, vec, modspec, modspec,
              pl.BlockSpec((None, 2, D_MODEL, LANES), lambda i: (layer, 0, 0, 0)),
              pl.BlockSpec((1, LANES), const)]
    args += [xall, w_out_bf, gate, lng, lnb, shift, scale, wr_split, br_pad]
    return pl.pallas_call(
        functools.partial(_out_kernel, with_ctx=with_ctx, n_lat_tiles=n_lat_tiles),
        grid=(n_rows // tm,),
        in_specs=specs,
        out_specs=[pl.BlockSpec((tm, D_MODEL), row),
                   pl.BlockSpec((tm, D_MODEL), row),
                   pl.BlockSpec((tm, LANES), row)],
        out_shape=[jax.ShapeDtypeStruct((n_rows, D_MODEL), F32),
                   jax.ShapeDtypeStruct((n_rows, D_MODEL), F32),
                   jax.ShapeDtypeStruct((n_rows, LANES), F32)],
        scratch_shapes=[pltpu.VMEM((tm, D_MODEL), F32),
                        pltpu.VMEM((2, tm, D_MODEL), BF16)],
        compiler_params=_cparams(("arbitrary",)),
        name="out_proj",
    )(*args)


def _route_kernel(lg_ref, tri_ref, eid_ref, rank_ref, gate_ref, cnt_ref, carry):
    @pl.when(pl.program_id(0) == 0)
    def _():
        carry[...] = jnp.zeros_like(carry)

    work = lg_ref[...]
    lane = lax.broadcasted_iota(jnp.int32, work.shape, 1)
    lane_f = lane.astype(F32)
    vals, ids, hots = [], [], []
    for _ in range(TOP_K):
        m = jnp.max(work, axis=-1, keepdims=True)
        idx = jnp.min(jnp.where(work == m, lane_f, float(LANES)), axis=-1, keepdims=True)
        hot = lane_f == idx
        vals.append(m)
        ids.append(idx)
        hots.append(hot)
        work = jnp.where(hot, -jnp.inf, work)
    exps = [jnp.exp(v - vals[0]) for v in vals]
    den = exps[0] + exps[1] + exps[2] + exps[3]
    multi = jnp.zeros(work.shape, F32)
    for hot in hots:
        multi = multi + hot.astype(F32)
    pref = _dot(tri_ref[...], multi.astype(BF16)) + carry[...]
    eid = jnp.zeros(work.shape, F32)
    rank = jnp.zeros(work.shape, F32)
    gate = jnp.zeros(work.shape, F32)
    for k in range(TOP_K):
        rk = jnp.sum(jnp.where(hots[k], pref, 0.0), axis=-1, keepdims=True)
        sel = lane == k
        eid = jnp.where(sel, ids[k], eid)
        rank = jnp.where(sel, rk, rank)
        gate = jnp.where(sel, exps[k] / den, gate)
    eid_ref[...] = eid.astype(jnp.int32)
    rank_ref[...] = rank.astype(jnp.int32)
    gate_ref[...] = gate
    carry[...] = carry[...] + jnp.sum(multi, axis=0, keepdims=True)
    cnt_ref[...] = jnp.broadcast_to(carry[...], cnt_ref.shape).astype(jnp.int32)


def _route(logits, tri):
    t = logits.shape[0]
    tm = ROW_TILE
    row = lambda i: (i, 0)
    return pl.pallas_call(
        _route_kernel,
        grid=(t // tm,),
        in_specs=[pl.BlockSpec((tm, LANES), row),
                  pl.BlockSpec((tm, tm), lambda i: (0, 0))],
        out_specs=[pl.BlockSpec((tm, LANES), row)] * 3
        + [pl.BlockSpec((8, LANES), lambda i: (0, 0))],
        out_shape=[jax.ShapeDtypeStruct((t, LANES), jnp.int32)] * 2
        + [jax.ShapeDtypeStruct((t, LANES), F32),
           jax.ShapeDtypeStruct((8, LANES), jnp.int32)],
        scratch_shapes=[pltpu.VMEM((1, LANES), F32)],
        compiler_params=_cparams(("arbitrary",)),
        name="route",
    )(logits, tri)


def _dispatch_kernel(pstart_ref, pcount_ref, h_ref, slot_ref, xb_ref, sem, pad_sem):
    tm = h_ref.shape[0]
    first = pl.program_id(0) == 0

    def row_copy(src_row, dst_row, s):
        return pltpu.make_async_copy(h_ref.at[pl.ds(src_row, 1)], xb_ref.at[pl.ds(dst_row, 1)], s)

    def tail_copy(b):
        return pltpu.make_async_copy(h_ref, xb_ref.at[pl.ds(b * MOE_BLOCK, MOE_BLOCK)], pad_sem)

    n_blocks = xb_ref.shape[0] // MOE_BLOCK
    n_used = pcount_ref[N_EXPERTS + 1]

    @pl.when(first)
    def _():
        def per_expert(e, carry):
            def per_row(j, c2):
                row_copy(0, pstart_ref[e] + j, pad_sem).start()
                return c2
            return lax.fori_loop(0, pcount_ref[e], per_row, carry)
        lax.fori_loop(0, N_EXPERTS, per_expert, 0)

        def per_tail(b, carry):
            tail_copy(b).start()
            return carry
        lax.fori_loop(n_used, n_blocks, per_tail, 0)

    def per_token(g, carry):
        for u in range(2):
            t = g * 2 + u
            for k in range(TOP_K):
                row_copy(t, slot_ref[t * TOP_K + k], sem).start(priority=k % 2)
        return carry

    lax.fori_loop(0, tm // 2, per_token, 0)
    for _ in range(TOP_K):
        pltpu.make_async_copy(h_ref, xb_ref.at[pl.ds(0, tm)], sem).wait()

    @pl.when(first)
    def _():
        def wait_row(j, carry):
            row_copy(0, 0, pad_sem).wait()
            return carry
        lax.fori_loop(0, pcount_ref[N_EXPERTS], wait_row, 0)

        def wait_tail(b, carry):
            tail_copy(b).wait()
            return carry
        lax.fori_loop(n_used, n_blocks, wait_tail, 0)


def _dispatch(pstart, pcount, h2, slot_flat, n_slots):
    t = h2.shape[0]
    tm = ROW_TILE
    gs = pltpu.PrefetchScalarGridSpec(
        num_scalar_prefetch=2,
        grid=(t // tm,),
        in_specs=[pl.BlockSpec((tm, h2.shape[1]), lambda i, a, b: (i, 0)),
                  pl.BlockSpec((tm * TOP_K,), lambda i, a, b: (i,), memory_space=pltpu.SMEM)],
        out_specs=pl.BlockSpec(memory_space=pl.ANY),
        scratch_shapes=[pltpu.SemaphoreType.DMA(()), pltpu.SemaphoreType.DMA(())],
    )
    return pl.pallas_call(
        _dispatch_kernel,
        grid_spec=gs,
        out_shape=jax.ShapeDtypeStruct((n_slots, h2.shape[1]), h2.dtype),
        compiler_params=_cparams(("arbitrary",), row_dma=True),
        name="dispatch",
    )(pstart, pcount, h2, slot_flat)


def _expert_kernel(bexp_ref, nused_ref, group_ref, next_ref, xb_ref, wgu_hbm, bgu_ref, wd_hbm,
                   bd_ref, perm_ref, y_ref, wgu_f, wd_f, wgu_s, wd_s, wsem, *, layer):
    b = pl.program_id(0)
    active = b < nused_ref[0]
    blk = xb_ref.shape[0]
    expert = bexp_ref[b]
    fresh = jnp.logical_or(b == 0, expert != bexp_ref[jnp.maximum(b - 1, 0)])
    half = group_ref[b] % 2

    def weight_copies(e, buf):
        return (pltpu.make_async_copy(wgu_hbm.at[layer, e], wgu_f.at[buf], wsem.at[buf, 0]),
                pltpu.make_async_copy(wd_hbm.at[layer, e], wd_f.at[buf], wsem.at[buf, 1]))

    @pl.when(jnp.logical_not(active))
    def _():
        y_ref[...] = jnp.zeros_like(y_ref)

    @pl.when(jnp.logical_and(active, b == 0))
    def _():
        for cp in weight_copies(expert, 0):
            cp.start()

    @pl.when(jnp.logical_and(active, fresh))
    def _():
        for cp in weight_copies(expert, half):
            cp.wait()
        wgu_s[...] = wgu_f[half].astype(BF16)
        wd_s[...] = _dot(perm_ref[...], wd_f[half].astype(BF16)).astype(BF16)

        @pl.when(next_ref[b] >= 0)
        def _():
            for cp in weight_copies(next_ref[b], 1 - half):
                cp.start()

    @pl.when(active)
    def _():
        gu = _dot(xb_ref[...].astype(BF16), wgu_s[...]) + bgu_ref[...]
        lane = lax.broadcasted_iota(jnp.int32, (blk, LANES), 1)
        even = (lane % 2) == 0
        prods = []
        for c in range(2 * D_FF // LANES):
            guc = gu[:, c * LANES:(c + 1) * LANES]
            glu = jnp.minimum(guc, SWIGLU_LIMIT)
            lin = jnp.clip(guc, -SWIGLU_LIMIT, SWIGLU_LIMIT) + 1.0
            prods.append(glu * jax.nn.sigmoid(SWIGLU_ALPHA * glu)
                         * pltpu.roll(lin, LANES - 1, axis=1))
        merged = [jnp.where(even, prods[2 * m], pltpu.roll(prods[2 * m + 1], 1, axis=1))
                  for m in range(D_FF // LANES)]
        act = jnp.concatenate(merged, axis=1).astype(BF16)
        y_ref[...] = _dot(act, wd_s[...]) + bd_ref[...]


def _experts(bexp, nused, group, next_expert, xb, w_gate_up, b_gate_up, w_down, b_down, layer,
             n_blocks):
    blk = MOE_BLOCK

    def bmap(b, be, nu, gr, nx):
        return (jnp.minimum(b, nu[0] - 1), 0)

    def emap(b, be, nu, gr, nx):
        return (layer, be[jnp.minimum(b, nu[0] - 1)], 0, 0)

    gs = pltpu.PrefetchScalarGridSpec(
        num_scalar_prefetch=4,
        grid=(n_blocks,),
        in_specs=[pl.BlockSpec((blk, D_MODEL), bmap),
                  pl.BlockSpec(memory_space=pl.ANY),
                  pl.BlockSpec((None, None, 1, 2 * D_FF), emap),
                  pl.BlockSpec(memory_space=pl.ANY),
                  pl.BlockSpec((None, None, 1, D_MODEL), emap),
                  pl.BlockSpec((D_FF, D_FF), lambda b, be, nu, gr, nx: (0, 0))],
        out_specs=pl.BlockSpec((blk, D_MODEL), lambda b, be, nu, gr, nx: (b, 0)),
        scratch_shapes=[pltpu.VMEM((2, D_MODEL, 2 * D_FF), F32),
                        pltpu.VMEM((2, D_FF, D_MODEL), F32),
                        pltpu.VMEM((D_MODEL, 2 * D_FF), BF16),
                        pltpu.VMEM((D_FF, D_MODEL), BF16),
                        pltpu.SemaphoreType.DMA((2, 2))],
    )
    lane = np.arange(D_FF) % LANES
    unit = (np.arange(D_FF) // LANES) * LANES + lane // 2 + (lane % 2) * (LANES // 2)
    perm = np.zeros((D_FF, D_FF), np.float32)
    perm[np.arange(D_FF), unit] = 1.0
    return pl.pallas_call(
        functools.partial(_expert_kernel, layer=layer),
        grid_spec=gs,
        out_shape=jax.ShapeDtypeStruct(xb.shape, F32),
        compiler_params=_cparams(("arbitrary",)),
        name="experts",
    )(bexp, nused, group, next_expert, xb, w_gate_up, b_gate_up[:, :, None, :],
      w_down, b_down[:, :, None, :], jnp.asarray(perm, BF16))


def _combine_kernel(x1_ref, slot_ref, slot_next_ref, gate_ref, yb_ref, g_ref, lng, lnb,
                    o_ref, gbuf, sems):
    tm = x1_ref.shape[0]
    i = pl.program_id(0)
    cur = i % 2

    def row_copy(src_row, buf, k, t):
        return pltpu.make_async_copy(yb_ref.at[pl.ds(src_row, 1)],
                                     gbuf.at[buf, k, pl.ds(t, 1)], sems.at[buf])

    def gather(slots, buf):
        def per_token(g, carry):
            for u in range(2):
                t = g * 2 + u
                for k in range(TOP_K):
                    row_copy(slots[t * TOP_K + k], buf, k, t).start(priority=k % 2)
            return carry
        lax.fori_loop(0, tm // 2, per_token, 0)

    @pl.when(i == 0)
    def _():
        gather(slot_ref, 0)

    @pl.when(i + 1 < pl.num_programs(0))
    def _():
        gather(slot_next_ref, 1 - cur)

    for k in range(TOP_K):
        pltpu.make_async_copy(yb_ref.at[pl.ds(0, tm)], gbuf.at[cur, k], sems.at[cur]).wait()

    def norm_rows(ci, carry):
        rows = pl.ds(pl.multiple_of(ci * NORM_ROWS, NORM_ROWS), NORM_ROWS)
        gate = gate_ref[rows, :]
        f = gate[:, 0:1] * gbuf[cur, 0, rows, :]
        for k in range(1, TOP_K):
            f = f + gate[:, k:k + 1] * gbuf[cur, k, rows, :]
        r = ALPHA * x1_ref[rows, :] + g_ref[0] * f
        o_ref[rows, :] = _ln_rows(r) * lng[...] + lnb[...]
        return carry

    lax.fori_loop(0, tm // NORM_ROWS, norm_rows, 0, unroll=True)


def _combine(x1, slot_flat, gates, yb, gate_mod, lng, lnb, n_lat_tiles):
    t = x1.shape[0]
    tm = ROW_TILE
    n_tiles = t // tm
    return pl.pallas_call(
        _combine_kernel,
        grid=(n_tiles,),
        in_specs=[pl.BlockSpec((tm, D_MODEL), lambda i: (i, 0)),
                  pl.BlockSpec((tm * TOP_K,), lambda i: (i,), memory_space=pltpu.SMEM),
                  pl.BlockSpec((tm * TOP_K,), lambda i: (jnp.minimum(i + 1, n_tiles - 1),),
                               memory_space=pltpu.SMEM),
                  pl.BlockSpec((tm, LANES), lambda i: (i, 0)),
                  pl.BlockSpec(memory_space=pl.ANY),
                  pl.BlockSpec((1, 1, D_MODEL),
                               lambda i: (jnp.where(i >= n_lat_tiles, 1, 0), 0, 0)),
                  pl.BlockSpec((1, D_MODEL), lambda i: (0, 0)),
                  pl.BlockSpec((1, D_MODEL), lambda i: (0, 0))],
        out_specs=pl.BlockSpec((tm, D_MODEL), lambda i: (i, 0)),
        out_shape=jax.ShapeDtypeStruct((t, D_MODEL), F32),
        scratch_shapes=[pltpu.VMEM((2, TOP_K, tm, yb.shape[1]), yb.dtype),
                        pltpu.SemaphoreType.DMA((2,))],
        compiler_params=_cparams(("arbitrary",), row_dma=True),
        name="combine",
    )(x1, slot_flat, slot_flat, gates, yb, gate_mod, lng, lnb)


def _rope_tables(s_len, cl):
    t = jnp.arange(s_len, dtype=jnp.int32)
    quarter = HEAD_DIM // 4
    inv = ROPE_BASE ** (-jnp.arange(quarter, dtype=F32) / quarter)
    ang_r = (t // GRID_W).astype(F32)[:, None] * inv
    ang_c = (t % GRID_W).astype(F32)[:, None] * inv
    cr, sr, cc, sc = jnp.cos(ang_r), jnp.sin(ang_r), jnp.cos(ang_c), jnp.sin(ang_c)
    cos_t = jnp.concatenate([cr, cr, cc, cc], axis=1)
    sin_t = jnp.concatenate([-sr, sr, -sc, sc], axis=1)
    cos_t = jnp.concatenate([cos_t, jnp.ones((cl, HEAD_DIM), F32)], axis=0)
    sin_t = jnp.concatenate([sin_t, jnp.zeros((cl, HEAD_DIM), F32)], axis=0)
    return cos_t, sin_t


def _block_plan(counts, eid, rank, n_blocks):
    blocks_per = (counts + MOE_BLOCK - 1) // MOE_BLOCK
    block_end = jnp.cumsum(blocks_per)
    base = (block_end - blocks_per) * MOE_BLOCK
    blocks = jnp.arange(n_blocks)
    bexp = jnp.minimum(jnp.sum(block_end[None, :] <= blocks[:, None], axis=1),
                       N_EXPERTS - 1).astype(jnp.int32)
    nused = block_end[-1:].astype(jnp.int32)
    experts = jnp.arange(N_EXPERTS, dtype=jnp.int32)
    first = jnp.concatenate([jnp.ones((1,), bool), bexp[1:] != bexp[:-1]])
    group = (jnp.cumsum(first) - 1).astype(jnp.int32)
    group_end = jnp.sum(jnp.where(bexp[:, None] == experts, block_end, 0), axis=-1)
    follower = jnp.sum(jnp.where(blocks[None, :] == group_end[:, None], bexp[None, :], 0), axis=-1)
    next_expert = jnp.where(group_end < nused[0], follower, -1).astype(jnp.int32)
    slot = rank + jnp.sum(jnp.where(eid[:, :, None] == experts, base, 0), axis=-1)
    pad = blocks_per * MOE_BLOCK - counts
    pcount = jnp.concatenate([pad, jnp.sum(pad)[None], nused]).astype(jnp.int32)
    pstart = (base + counts).astype(jnp.int32)
    return (bexp, nused, group, next_expert, slot.reshape(-1).astype(jnp.int32), pstart,
            pcount)


def kernel(x, c, ctx, c_ctx, w_ada, b_ada, w_in, w_out, sg_w, sg_b, na_rpb, ln1_g, ln1_b,
           ln2_g, ln2_b, w_router, b_router, w_gate_up, b_gate_up, w_down, b_down):
    bsz, s_len, dm = x.shape
    cl = ctx.shape[1]
    depth = w_ada.shape[0]
    assert bsz == 1 and dm == D_MODEL and cl == ROW_TILE and MOE_BLOCK == ROW_TILE
    assert s_len % (GRID_W * KEY_ROWS) == 0 and s_len % ROW_TILE == 0
    t_all = s_len + cl
    n_lat_tiles = s_len // ROW_TILE
    rows = s_len // GRID_W

    cond = jnp.zeros((8, dm), F32).at[0].set(c[0]).at[1].set(c_ctx)
    mod = _ada_mod(cond, w_ada, b_ada)[:, :2].reshape(depth, 2, 6, 1, dm)
    cos_t, sin_t = _rope_tables(s_len, cl)
    pb, vrow = _na_bias_tables(na_rpb, rows)
    cs, m1, tw, mc = _fourier_consts(s_len, cl)
    tri = jnp.asarray(np.tril(np.ones((ROW_TILE, ROW_TILE), np.float32), -1), BF16)
    wr_pad = jnp.pad(w_router, ((0, 0), (0, 0), (0, LANES - N_EXPERTS)))
    wr_hi = wr_pad.astype(BF16)
    wr_split = jnp.stack([wr_hi, (wr_pad - wr_hi.astype(F32)).astype(BF16)], axis=1)
    br_pad = jnp.pad(b_router, ((0, 0), (0, LANES - N_EXPERTS)), constant_values=NEG_INF)
    w_in_bf = w_in.astype(BF16)
    w_out_bf = w_out.astype(BF16)
    sgw_bf = sg_w.astype(BF16)

    xall = jnp.concatenate([x[0], ctx[0]], axis=0)
    for l in range(depth):
        last = l == depth - 1
        m = lambda j: mod[l, :, j]
        qp, qr, kr, v, z_sg, z_ft = _proj(xall, m(0), m(1), w_in_bf, cos_t, sin_t, l,
                                          n_lat_tiles)
        ona_l = _na_attention(qr, qp, kr, v, pb, vrow, l, s_len)
        osg = _spatial_gating(z_sg, sgw_bf[l], sg_b[l][:, :, None])
        oft_l = _fourier_latent(z_ft, cs, m1, tw, s_len)
        if last:
            ona_c = oft_c = None
            n_rows = s_len
        else:
            ona_c = _ctx_attention(qp, kr, v, s_len, cl)
            oft_c = _fourier_ctx(z_ft, cs, mc, s_len, cl)
            n_rows = t_all
        x1, h2, logits = _out_proj(
            ona_l, ona_c, osg, oft_l, oft_c, xall, w_out_bf, m(2),
            ln1_g[l][None], ln1_b[l][None], m(3), m(4), wr_split, br_pad[l][None],
            l, n_rows, n_lat_tiles, not last)
        eid, rank, gates, counts = _route(logits, tri)
        n_blocks = -(-n_rows * TOP_K // MOE_BLOCK) + N_EXPERTS
        bexp, nused, group, next_expert, slot_flat, pstart, pcount = _block_plan(
            counts[0, :N_EXPERTS], eid[:, :TOP_K], rank[:, :TOP_K], n_blocks)
        xb = _dispatch(pstart, pcount, h2, slot_flat, n_blocks * MOE_BLOCK)
        yb = _experts(bexp, nused, group, next_expert, xb, w_gate_up, b_gate_up, w_down, b_down,
                      l, n_blocks)
        xall = _combine(x1, slot_flat, gates, yb, m(5), ln2_g[l][None], ln2_b[l][None],
                        n_lat_tiles)
    return xall[None]
```
